```python
import jax
import jax.numpy as jnp
from jax import lax
import numpy as np

D_MODEL = 2048
BATCH = 4
SEQ = 4096
DEPTH = 2

GRID_W = 64
CTX_LEN = 256
HEAD_DIM = 128
MIX_DIM = D_MODEL
RET_HEADS = MIX_DIM // 2 // HEAD_DIM
RET_DIM = RET_HEADS * HEAD_DIM
RET_CHUNK = 128
GMLP_DIM = MIX_DIM - RET_DIM
GMLP_GROUPS = 8
GMLP_CHUNK = 128
NA_HEADS = MIX_DIM // 2 // HEAD_DIM
NA_DIM = NA_HEADS * HEAD_DIM
WIN_ROWS = 8
WIN_COLS = 16
GLA_HEADS = 8
GLA_DV = (MIX_DIM - NA_DIM) // GLA_HEADS
GLA_DK = GLA_DV // 2
GLA_QK_DIM = GLA_HEADS * GLA_DK
GLA_V_DIM = GLA_HEADS * GLA_DV
GLA_GATE_RANK = 16
GLA_TAU = 16.0
GLA_CHUNK = 64
N_EXPERTS = 16
D_EXPERT = 2048
EC_CAPACITY = 2
ROPE_BASE = 10000.0
RMS_EPS = 1e-6
N_EVEN = (DEPTH + 1) // 2
N_ODD = DEPTH // 2

EVEN_WIDTHS = (RET_DIM, RET_DIM, GMLP_DIM, GMLP_DIM, RET_DIM, RET_DIM)
EVEN_Q_COLS = 2 * RET_DIM + 2 * GMLP_DIM
EVEN_COLS = EVEN_Q_COLS + 2 * RET_DIM
ODD_Q_WIDTHS = (NA_DIM, GLA_QK_DIM, GLA_V_DIM)
ODD_KV_WIDTHS = (NA_DIM, NA_DIM, GLA_QK_DIM, GLA_V_DIM, 2 * GLA_GATE_RANK)
ODD_Q_COLS = NA_DIM + GLA_QK_DIM + GLA_V_DIM
ODD_COLS = ODD_Q_COLS + 2 * NA_DIM + GLA_QK_DIM + GLA_V_DIM + 2 * GLA_GATE_RANK

kernel_name = 'hybrid_diffusion_retention_gmlp_natten_gla_ecmoe'


def rmsnorm(x, gain):
    xf = x.astype(jnp.float32)
    xf = xf * lax.rsqrt(jnp.mean(xf * xf, axis=-1, keepdims=True) + RMS_EPS)
    return (xf * gain.astype(jnp.float32)).astype(x.dtype)


def split_cols(p, widths):
    return jnp.split(p, np.cumsum(widths)[:-1].tolist(), axis=-1)


def to_heads(z, n_heads):
    b, t, _ = z.shape
    return z.reshape(b, t, n_heads, -1).transpose(0, 2, 1, 3)


def merge_heads(z):
    b, h, t, d = z.shape
    return z.transpose(0, 2, 1, 3).reshape(b, t, h * d)


def head_rmsnorm(o, gain, dtype):
    of = o.astype(jnp.float32)
    of = of * lax.rsqrt(jnp.mean(of * of, axis=-1, keepdims=True) + RMS_EPS)
    return (merge_heads(of) * gain.astype(jnp.float32)).astype(dtype)


def axial_rotary(z, rows, cols):
    half = z.shape[-1] // 2
    nf = half // 2
    freq = ROPE_BASE ** (-jnp.arange(nf, dtype=jnp.float32) / nf)

    def rotate(u, pos):
        ang = pos.astype(jnp.float32)[:, None] * freq
        cos, sin = jnp.cos(ang).astype(u.dtype), jnp.sin(ang).astype(u.dtype)
        u1, u2 = u[..., :nf], u[..., nf:]
        return jnp.concatenate([u1 * cos - u2 * sin, u1 * sin + u2 * cos], axis=-1)

    return jnp.concatenate([rotate(z[..., :half], rows), rotate(z[..., half:], cols)], axis=-1)


def chunked_decay_attention(q, k, v, log_a, s0, chunk):
    B, H, T, dk = q.shape
    dv = v.shape[-1]
    n = T // chunk
    f32 = jnp.float32

    def blocks(z):
        return z.astype(f32).reshape(B, H, n, chunk, z.shape[-1])

    qc, kc, vc = blocks(q), blocks(k), blocks(v)
    b = jnp.cumsum(blocks(jnp.broadcast_to(log_a, q.shape)), axis=3)
    b_end = b[:, :, :, -1:, :]
    q_in = qc * jnp.exp(b)
    k_in = kc * jnp.exp(-b)
    k_out = kc * jnp.exp(b_end - b)
    tri = jnp.tril(jnp.ones((chunk, chunk), dtype=bool))
    att = jnp.where(tri, jnp.einsum('bhntd,bhnsd->bhnts', q_in, k_in), 0.0)
    o_intra = jnp.einsum('bhnts,bhnse->bhnte', att, vc)
    kv = jnp.einsum('bhnsd,bhnse->bhnde', k_out, vc)
    dec = jnp.exp(b_end[:, :, :, 0, :])

    def step(S, inp):
        kv_i, dec_i = inp
        return dec_i[..., None] * S + kv_i, S

    s_fin, s_in = lax.scan(step, s0.astype(f32), (jnp.moveaxis(kv, 2, 0), jnp.moveaxis(dec, 2, 0)))
    o_inter = jnp.einsum('bhntd,nbhde->bhnte', q_in, s_in)
    return (o_intra + o_inter).reshape(B, H, T, dv), s_fin


def decayed_final_state(k, v, log_a):
    la = jnp.broadcast_to(log_a, k.shape).astype(jnp.float32)
    cum = jnp.cumsum(la, axis=2)
    w = jnp.exp(cum[:, :, -1:, :] - cum)
    return jnp.einsum('bhtd,bhte->bhde', k.astype(jnp.float32) * w, v.astype(jnp.float32))


def bidir_decay_attention(q, k, v, la, q_c, k_c, v_c, la_c, chunk):
    B, H, _, dk = k.shape
    dv = v.shape[-1]
    s_zero = jnp.zeros((B, H, dk, dv), jnp.float32)
    o_lat = 0.0
    o_ctx = None if q_c is None else 0.0
    for d in range(2):
        fl = (lambda z: z) if d == 0 else (lambda z: jnp.flip(z, axis=2))
        if q_c is None:
            s_c = decayed_final_state(fl(k_c), fl(v_c), fl(la_c[d]))
        else:
            oc, s_c = chunked_decay_attention(fl(q_c), fl(k_c), fl(v_c), fl(la_c[d]), s_zero, chunk)
            o_ctx = o_ctx + fl(oc)
        ol, _ = chunked_decay_attention(fl(q), fl(k), fl(v), fl(la[d]), s_c, chunk)
        o_lat = o_lat + fl(ol)
    return o_ctx, o_lat


def chunk_gmlp(u, v, norm_gain, ws, bs):
    B, T, _ = u.shape
    u = jax.nn.gelu(u)
    v = rmsnorm(jax.nn.gelu(v), norm_gain)
    vc = v.reshape(B, T // GMLP_CHUNK, GMLP_CHUNK, GMLP_GROUPS, -1)
    mixed = jnp.einsum('gps,bnsgc->bnpgc', ws, vc) + bs.T[None, None, :, :, None]
    return u * mixed.reshape(B, T, -1).astype(u.dtype)


def context_attention(q, k, v):
    s = jnp.einsum('bhqd,bhkd->bhqk', q, k).astype(jnp.float32) * (q.shape[-1] ** -0.5)
    return jnp.einsum('bhqk,bhkd->bhqd', jax.nn.softmax(s, axis=-1).astype(v.dtype), v)


def neighbourhood_attention(q, k, v, k_ctx, v_ctx, rpb):
    B, H, T, dh = q.shape
    rows = T // GRID_W
    wr, wc = min(WIN_ROWS, rows), WIN_COLS
    grid = lambda z: z.reshape(B, H, rows, GRID_W, dh)
    qg, kg, vg = grid(q * (dh ** -0.5)), grid(k), grid(v)
    col = jnp.arange(GRID_W)
    cstart = jnp.clip(col - wc // 2, 0, GRID_W - wc)
    cidx = cstart[:, None] + jnp.arange(wc)[None, :]
    rpb_c = rpb[:, :, cidx - col[:, None] + (WIN_COLS - 1)]

    def row_block(args):
        r, q_r = args
        rstart = jnp.clip(r - wr // 2, 0, rows - wr)
        k_win = lax.dynamic_slice_in_dim(kg, rstart, wr, axis=2)[:, :, :, cidx, :]
        v_win = lax.dynamic_slice_in_dim(vg, rstart, wr, axis=2)[:, :, :, cidx, :]
        dr = rstart + jnp.arange(wr) - r + (WIN_ROWS - 1)
        bias = jnp.take(rpb_c, dr, axis=1).transpose(0, 2, 1, 3)
        s_win = jnp.einsum('bhqd,bhiqjd->bhqij', q_r, k_win).astype(jnp.float32) + bias[None].astype(jnp.float32)
        s_ctx = jnp.einsum('bhqd,bhld->bhql', q_r, k_ctx).astype(jnp.float32)
        s = jnp.concatenate([s_win.reshape(B, H, GRID_W, wr * wc), s_ctx], axis=-1)
        p = jax.nn.softmax(s, axis=-1).astype(v.dtype)
        p_win = p[..., :wr * wc].reshape(B, H, GRID_W, wr, wc)
        return (jnp.einsum('bhqij,bhiqjd->bhqd', p_win, v_win)
                + jnp.einsum('bhql,bhld->bhqd', p[..., wr * wc:], v_ctx))

    out = lax.map(row_block, (jnp.arange(rows), jnp.moveaxis(qg, 2, 0)))
    return jnp.moveaxis(out, 0, 2).reshape(B, H, T, dh)


def gla_log_decay(lr, w_up, b_up, d):
    z = lr[..., d * GLA_GATE_RANK:(d + 1) * GLA_GATE_RANK] @ w_up[d] + b_up[d]
    return to_heads(jax.nn.log_sigmoid(z.astype(jnp.float32)) / GLA_TAU, GLA_HEADS)


def even_mixer(h, hc, w_in, w_out, gamma_logit, ret_norm, gmlp_norm, gmlp_ws, gmlp_bs, last):
    T = h.shape[1]
    t = jnp.arange(T)
    rows, cols = t // GRID_W, t % GRID_W
    scale = HEAD_DIM ** -0.5
    rq, rg, u, vg, rk, rv = split_cols(h @ w_in, EVEN_WIDTHS)
    log_g = jax.nn.log_sigmoid(gamma_logit.astype(jnp.float32))
    la = [log_g[d][None, :, None, None] for d in range(2)]
    q_l = axial_rotary(to_heads(rq, RET_HEADS) * scale, rows, cols)
    k_l = axial_rotary(to_heads(rk, RET_HEADS), rows, cols)
    v_l = to_heads(rv, RET_HEADS)
    if last:
        ck, cv = split_cols(hc @ w_in[:, EVEN_Q_COLS:], EVEN_WIDTHS[4:])
        cq = None
    else:
        cq, cg, cu, cvg, ck, cv = split_cols(hc @ w_in, EVEN_WIDTHS)
        cq = to_heads(cq, RET_HEADS) * scale
    o_c, o_l = bidir_decay_attention(q_l, k_l, v_l, la, cq, to_heads(ck, RET_HEADS), to_heads(cv, RET_HEADS), la, RET_CHUNK)
    ret_lat = head_rmsnorm(o_l, ret_norm, h.dtype) * jax.nn.silu(rg)
    y = jnp.concatenate([ret_lat, chunk_gmlp(u, vg, gmlp_norm, gmlp_ws, gmlp_bs)], axis=-1) @ w_out
    if last:
        return None, y
    ret_ctx = head_rmsnorm(o_c, ret_norm, hc.dtype) * jax.nn.silu(cg)
    yc = jnp.concatenate([ret_ctx, chunk_gmlp(cu, cvg, gmlp_norm, gmlp_ws, gmlp_bs)], axis=-1) @ w_out
    return yc, y


def odd_mixer(h, hc, w_in, w_out, rpb, w_up, b_up, gla_norm, last):
    p = h @ w_in
    nq, gq, gr = split_cols(p[..., :ODD_Q_COLS], ODD_Q_WIDTHS)
    nk, nv, gk, gv, glr = split_cols(p[..., ODD_Q_COLS:], ODD_KV_WIDTHS)
    if last:
        pc_kv = hc @ w_in[:, ODD_Q_COLS:]
    else:
        pc = hc @ w_in
        cnq, cgq, cgr = split_cols(pc[..., :ODD_Q_COLS], ODD_Q_WIDTHS)
        pc_kv = pc[..., ODD_Q_COLS:]
    cnk, cnv, cgk, cgv, clr = split_cols(pc_kv, ODD_KV_WIDTHS)
    nk_c, nv_c = to_heads(cnk, NA_HEADS), to_heads(cnv, NA_HEADS)
    o_na = neighbourhood_attention(to_heads(nq, NA_HEADS), to_heads(nk, NA_HEADS), to_heads(nv, NA_HEADS), nk_c, nv_c, rpb)
    la_l = [gla_log_decay(glr, w_up, b_up, d) for d in range(2)]
    la_c = [gla_log_decay(clr, w_up, b_up, d) for d in range(2)]
    qscale = GLA_DK ** -0.5
    cq = None if last else to_heads(cgq, GLA_HEADS) * qscale
    o_c, o_l = bidir_decay_attention(to_heads(gq, GLA_HEADS) * qscale, to_heads(gk, GLA_HEADS), to_heads(gv, GLA_HEADS), la_l,
                                     cq, to_heads(cgk, GLA_HEADS), to_heads(cgv, GLA_HEADS), la_c, GLA_CHUNK)
    gla_lat = head_rmsnorm(o_l, gla_norm, h.dtype) * jax.nn.silu(gr)
    y = jnp.concatenate([merge_heads(o_na), gla_lat], axis=-1) @ w_out
    if last:
        return None, y
    o_na_c = context_attention(to_heads(cnq, NA_HEADS), nk_c, nv_c)
    gla_ctx = head_rmsnorm(o_c, gla_norm, hc.dtype) * jax.nn.silu(cgr)
    yc = jnp.concatenate([merge_heads(o_na_c), gla_ctx], axis=-1) @ w_out
    return yc, y


def expert_choice_ffn(h, router_w, w_gate, w_up, w_down):
    B, T, _ = h.shape
    cap = EC_CAPACITY * T // N_EXPERTS
    aff = jax.nn.softmax((h @ router_w).astype(jnp.float32), axis=-1)
    g, idx = lax.top_k(aff.transpose(0, 2, 1), cap)
    b_idx = jnp.arange(B)[:, None, None]
    xs = h[b_idx, idx]
    a = jnp.einsum('becd,edf->becf', xs, w_gate)
    up = jnp.einsum('becd,edf->becf', xs, w_up)
    y = jnp.einsum('becf,efd->becd', jax.nn.silu(a) * up, w_down) * g[..., None].astype(h.dtype)
    return jnp.zeros_like(h).at[b_idx, idx].add(y.astype(h.dtype))


def setup_inputs(seed: int = 0) -> dict:
    key = jax.random.key(seed)
    ks = iter(jax.random.split(key, 32))
    f32 = jnp.float32
    nrm = lambda shape, s: jax.random.normal(next(ks), shape, f32) * s
    D = D_MODEL
    base_logit = jnp.log(2.0 ** (5.0 + jnp.arange(RET_HEADS, dtype=f32)) - 1.0)
    return {
        'x': nrm((BATCH, SEQ, D), 1.0),
        'c': nrm((BATCH, D), 1.0),
        'ctx': nrm((BATCH, CTX_LEN, D), 1.0),
        'c_ctx': nrm((D,), 1.0),
        'ada_w': nrm((DEPTH, D, 6 * D), 0.5 * D ** -0.5),
        'ada_b': nrm((DEPTH, 6 * D), 0.01),
        'norm_mix': 1.0 + nrm((DEPTH, D), 0.05),
        'norm_ffn': 1.0 + nrm((DEPTH, D), 0.05),
        'norm_final': 1.0 + nrm((D,), 0.05),
        'ev_w_in': nrm((N_EVEN, D, EVEN_COLS), D ** -0.5),
        'ev_w_out': nrm((N_EVEN, MIX_DIM, D), MIX_DIM ** -0.5),
        'ret_gamma_logit': base_logit + nrm((N_EVEN, 2, RET_HEADS), 0.1),
        'ret_norm': 1.0 + nrm((N_EVEN, RET_DIM), 0.05),
        'gmlp_norm': 1.0 + nrm((N_EVEN, GMLP_DIM), 0.05),
        'gmlp_ws': nrm((N_EVEN, GMLP_GROUPS, GMLP_CHUNK, GMLP_CHUNK), GMLP_CHUNK ** -0.5),
        'gmlp_bs': 1.0 + nrm((N_EVEN, GMLP_GROUPS, GMLP_CHUNK), 0.1),
        'od_w_in': nrm((N_ODD, D, ODD_COLS), D ** -0.5),
        'od_w_out': nrm((N_ODD, MIX_DIM, D), MIX_DIM ** -0.5),
        'na_rpb': nrm((N_ODD, NA_HEADS, 2 * WIN_ROWS - 1, 2 * WIN_COLS - 1), 0.1),
        'gla_w_up': nrm((N_ODD, 2, GLA_GATE_RANK, GLA_QK_DIM), GLA_GATE_RANK ** -0.5),
        'gla_b_up': nrm((N_ODD, 2, GLA_QK_DIM), 0.1),
        'gla_norm': 1.0 + nrm((N_ODD, GLA_V_DIM), 0.05),
        'router_w': nrm((DEPTH, D, N_EXPERTS), D ** -0.5),
        'moe_w_gate': nrm((DEPTH, N_EXPERTS, D, D_EXPERT), D ** -0.5),
        'moe_w_up': nrm((DEPTH, N_EXPERTS, D, D_EXPERT), D ** -0.5),
        'moe_w_down': nrm((DEPTH, N_EXPERTS, D_EXPERT, D), D_EXPERT ** -0.5),
    }


def reference(x, c, ctx, c_ctx, ada_w, ada_b, norm_mix, norm_ffn, norm_final,
              ev_w_in, ev_w_out, ret_gamma_logit, ret_norm, gmlp_norm, gmlp_ws, gmlp_bs,
              od_w_in, od_w_out, na_rpb, gla_w_up, gla_b_up, gla_norm,
              router_w, moe_w_gate, moe_w_up, moe_w_down):
    for l in range(DEPTH):
        last = l == DEPTH - 1
        li = l // 2
        mod = jax.nn.silu(c) @ ada_w[l] + ada_b[l]
        sh1, sc1, g1, sh2, sc2, g2 = [m[:, None, :] for m in jnp.split(mod, 6, axis=-1)]
        n_ctx = 2 if last else 6
        mc = jnp.split(jax.nn.silu(c_ctx) @ ada_w[l][:, :n_ctx * D_MODEL] + ada_b[l][:n_ctx * D_MODEL], n_ctx)
        h = rmsnorm(x, norm_mix[l]) * (1.0 + sc1) + sh1
        hc = rmsnorm(ctx, norm_mix[l]) * (1.0 + mc[1]) + mc[0]
        if l % 2 == 0:
            yc, y = even_mixer(h, hc, ev_w_in[li], ev_w_out[li], ret_gamma_logit[li], ret_norm[li],
                               gmlp_norm[li], gmlp_ws[li], gmlp_bs[li], last)
        else:
            yc, y = odd_mixer(h, hc, od_w_in[li], od_w_out[li], na_rpb[li], gla_w_up[li], gla_b_up[li],
                              gla_norm[li], last)
        x = x + g1 * y
        h2 = rmsnorm(x, norm_ffn[l]) * (1.0 + sc2) + sh2
        x = x + g2 * expert_choice_ffn(h2, router_w[l], moe_w_gate[l], moe_w_up[l], moe_w_down[l])
        if not last:
            ctx = ctx + mc[2] * yc
            hc2 = rmsnorm(ctx, norm_ffn[l]) * (1.0 + mc[4]) + mc[3]
            ctx = ctx + mc[5] * expert_choice_ffn(hc2, router_w[l], moe_w_gate[l], moe_w_up[l], moe_w_down[l])
    return rmsnorm(x, norm_final)
```

```python
import functools

import numpy as np
import jax
import jax.numpy as jnp
from jax import lax
from jax.experimental import pallas as pl
from jax.experimental.pallas import tpu as pltpu

D_MODEL = 2048
BATCH = 4
T_LAT = 4096
T_CTX = 256
T_ALL = T_LAT + T_CTX
GRID_W = 64
GRID_ROWS = T_LAT // GRID_W
HEAD_DIM = 128
RET_HEADS = 8
RET_DIM = 1024
GMLP_DIM = 1024
GMLP_GROUPS = 8
GMLP_CHUNK = 128
NA_HEADS = 8
NA_DIM = 1024
WIN_ROWS = 8
WIN_COLS = 16
GLA_HEADS = 8
GLA_DK = 64
GLA_DV = 128
GLA_QK_DIM = 512
GLA_V_DIM = 1024
GLA_GATE_RANK = 16
GLA_TAU = 16.0
N_EXPERTS = 16
D_EXPERT = 2048
EC_CAPACITY = 2
ROPE_BASE = 10000.0
RMS_EPS = 1e-6
EVEN_COLS = 6144
ODD_COLS = 6176
ODD_COLS_PAD = 6272
LANE = 128

ROW_TILE = 256
N_LAT_TILES = T_LAT // ROW_TILE
N_ALL_TILES = T_ALL // ROW_TILE
RET_CHUNK = 128
GLA_CHUNK = 64
NA_QROWS = 4
NA_QT = NA_QROWS * GRID_W
NA_KROWS = NA_QROWS + WIN_ROWS - 1
NA_KT = NA_KROWS * GRID_W
NEG_BIG = -1e30
VMEM_LIMIT = 56 * 1024 * 1024

F32 = jnp.float32
BF16 = jnp.bfloat16


def _cparams(n_axes):
    return pltpu.CompilerParams(dimension_semantics=("arbitrary",) * n_axes,
                                vmem_limit_bytes=VMEM_LIMIT)


def _dot(a, b):
    return jnp.dot(a, b, preferred_element_type=F32)


def _dot_nt(a, b):
    return lax.dot_general(a, b, (((1,), (1,)), ((), ())), preferred_element_type=F32)


def _dot_tn(a, b):
    return lax.dot_general(a, b, (((0,), (0,)), ((), ())), preferred_element_type=F32)


def _silu(x):
    return x * jax.nn.sigmoid(x)


def _rms(x):
    return x * lax.rsqrt(jnp.mean(x * x, axis=-1, keepdims=True) + RMS_EPS)


def _mm_kernel(a_ref, w_ref, o_ref):
    o_ref[...] = _dot(a_ref[...], w_ref[...]).astype(o_ref.dtype)


def _matmul(a, w, tm, tn, out_dtype):
    m, k = a.shape
    n = w.shape[1]
    return pl.pallas_call(
        _mm_kernel,
        grid=(n // tn, m // tm),
        in_specs=[pl.BlockSpec((tm, k), lambda j, i: (i, 0)),
                  pl.BlockSpec((k, tn), lambda j, i: (0, j))],
        out_specs=pl.BlockSpec((tm, tn), lambda j, i: (i, j)),
        out_shape=jax.ShapeDtypeStruct((m, n), out_dtype),
        compiler_params=_cparams(2),
        name="matmul",
    )(a, w)


def _modnorm_kernel(x_ref, gain_ref, sc_ref, sh_ref, o_ref):
    x = x_ref[...]
    o_ref[...] = (_rms(x) * gain_ref[...] * (1.0 + sc_ref[...]) + sh_ref[...]).astype(o_ref.dtype)


def _mod_spec(which):
    return pl.BlockSpec((None, None, None, 1, D_MODEL),
                        lambda b, i: (b, i // N_LAT_TILES, which, 0, 0))


def _modnorm(x, gain, mod, sc_i, sh_i):
    return pl.pallas_call(
        _modnorm_kernel,
        grid=(BATCH, N_ALL_TILES),
        in_specs=[pl.BlockSpec((None, ROW_TILE, D_MODEL), lambda b, i: (b, i, 0)),
                  pl.BlockSpec((1, D_MODEL), lambda b, i: (0, 0)),
                  _mod_spec(sc_i), _mod_spec(sh_i)],
        out_specs=pl.BlockSpec((None, ROW_TILE, D_MODEL), lambda b, i: (b, i, 0)),
        out_shape=jax.ShapeDtypeStruct((BATCH, T_ALL, D_MODEL), BF16),
        compiler_params=_cparams(2),
        name="modnorm",
    )(x, gain, mod, mod)


def _rotary(z, cos, s1, s2):
    return z * cos + pltpu.roll(z, 96, 1) * s1 + pltpu.roll(z, 32, 1) * s2


def _ret_kernel(lg_ref, q_ref, g_ref, k_ref, v_ref, cos_ref, s1_ref, s2_ref, gain_ref,
                o_ref, acc_ref):
    c = RET_CHUNK
    h = pl.program_id(1)
    scale = HEAD_DIM ** -0.5
    row = lax.broadcasted_iota(jnp.int32, (c, c), 0)
    col = lax.broadcasted_iota(jnp.int32, (c, c), 1)
    tcol = lax.broadcasted_iota(jnp.int32, (c, 1), 0).astype(F32)
    gain = gain_ref[...]

    for d in range(2):
        la = lg_ref[d, h]
        if d == 0:
            dist = (row - col).astype(F32)
            qdec = jnp.exp(la * (tcol + 1.0))
            kdec = jnp.exp(la * (c - 1.0 - tcol))
        else:
            dist = (col - row).astype(F32)
            qdec = jnp.exp(la * (c - tcol))
            kdec = jnp.exp(la * tcol)
        dmat = jnp.where(dist >= 0.0, jnp.exp(la * jnp.maximum(dist, 0.0)), 0.0)
        cdec = jnp.exp(jnp.full((1, LANE), la * c, F32))

        def chunk(i, st, base, n, rot, d=d, dmat=dmat, qdec=qdec, kdec=kdec, cdec=cdec):
            idx = i if d == 0 else n - 1 - i
            start = pl.multiple_of(base + idx * c, c)
            q = q_ref[pl.ds(start, c), :].astype(F32) * scale
            k = k_ref[pl.ds(start, c), :].astype(F32)
            v = v_ref[pl.ds(start, c), :]
            if rot:
                cos, s1, s2 = (cos_ref[pl.ds(start, c), :], s1_ref[pl.ds(start, c), :],
                               s2_ref[pl.ds(start, c), :])
                q = _rotary(q, cos, s1, s2)
                k = _rotary(k, cos, s1, s2)
            att = _dot_nt(q.astype(BF16), k.astype(BF16)) * dmat
            o = _dot(att.astype(BF16), v) + _dot_nt((q * qdec).astype(BF16), st.astype(BF16))
            st = cdec * st + _dot_tn(v, (k * kdec).astype(BF16))
            if d == 0:
                acc_ref[pl.ds(start, c), :] = o
            else:
                o = acc_ref[pl.ds(start, c), :] + o
                gate = g_ref[pl.ds(start, c), :].astype(F32)
                o_ref[pl.ds(start, c), :] = (_rms(o) * gain * _silu(gate)).astype(o_ref.dtype)
            return st

        st = jnp.zeros((HEAD_DIM, HEAD_DIM), F32)
        st = lax.fori_loop(0, T_CTX // c,
                           functools.partial(chunk, base=T_LAT, n=T_CTX // c, rot=False), st)
        lax.fori_loop(0, T_LAT // c,
                      functools.partial(chunk, base=0, n=T_LAT // c, rot=True), st)


def _retention(p, log_g, cos, s1, s2, ret_norm):
    col = lambda off: pl.BlockSpec((None, T_ALL, LANE), lambda b, h, lg: (b, 0, off + h))
    tab = pl.BlockSpec((T_LAT, LANE), lambda b, h, lg: (0, 0))
    return pl.pallas_call(
        _ret_kernel,
        grid_spec=pltpu.PrefetchScalarGridSpec(
            num_scalar_prefetch=1,
            grid=(BATCH, RET_HEADS),
            in_specs=[col(0), col(8), col(32), col(40), tab, tab, tab,
                      pl.BlockSpec((1, LANE), lambda b, h, lg: (0, h))],
            out_specs=pl.BlockSpec((None, T_ALL, LANE), lambda b, h, lg: (b, 0, h)),
            scratch_shapes=[pltpu.VMEM((T_ALL, LANE), F32)]),
        out_shape=jax.ShapeDtypeStruct((BATCH, T_ALL, RET_DIM), BF16),
        compiler_params=_cparams(2),
        name="retention",
    )(log_g, p, p, p, p, cos, s1, s2, ret_norm)


def _gmlp_kernel(u_ref, v_ref, gain_ref, ws_ref, bst_ref, o_ref):
    u = jax.nn.gelu(u_ref[...].astype(F32))
    v = _rms(jax.nn.gelu(v_ref[...].astype(F32))) * gain_ref[...]
    vb = v.astype(BF16)
    for n in range(ROW_TILE // GMLP_CHUNK):
        r0 = n * GMLP_CHUNK
        for g in range(GMLP_GROUPS):
            c0 = g * LANE
            mixed = (_dot(ws_ref[g], vb[r0:r0 + GMLP_CHUNK, c0:c0 + LANE])
                     + bst_ref[:, g:g + 1])
            o_ref[r0:r0 + GMLP_CHUNK, c0:c0 + LANE] = (
                u[r0:r0 + GMLP_CHUNK, c0:c0 + LANE] * mixed).astype(o_ref.dtype)


def _gmlp(p, gmlp_norm, ws_bf16, bs_t):
    return pl.pallas_call(
        _gmlp_kernel,
        grid=(BATCH, N_ALL_TILES),
        in_specs=[pl.BlockSpec((None, ROW_TILE, GMLP_DIM), lambda b, i: (b, i, 2)),
                  pl.BlockSpec((None, ROW_TILE, GMLP_DIM), lambda b, i: (b, i, 3)),
                  pl.BlockSpec((1, GMLP_DIM), lambda b, i: (0, 0)),
                  pl.BlockSpec((GMLP_GROUPS, GMLP_CHUNK, GMLP_CHUNK), lambda b, i: (0, 0, 0)),
                  pl.BlockSpec((GMLP_CHUNK, GMLP_GROUPS), lambda b, i: (0, 0))],
        out_specs=pl.BlockSpec((None, ROW_TILE, GMLP_DIM), lambda b, i: (b, i, 0)),
        out_shape=jax.ShapeDtypeStruct((BATCH, T_ALL, GMLP_DIM), BF16),
        compiler_params=_cparams(2),
        name="gmlp",
    )(p, p, gmlp_norm, ws_bf16, bs_t)


def _na_kernel(q_ref, k_ref, v_ref, bias_ref, o_ref):
    i = pl.program_id(2)
    w0 = jnp.clip(i * NA_QROWS - WIN_ROWS // 2, 0, GRID_ROWS - NA_KROWS)
    start = pl.multiple_of(w0 * GRID_W, GRID_W)
    scale = HEAD_DIM ** -0.5
    q = q_ref[...]
    s_win = _dot_nt(q, k_ref[pl.ds(start, NA_KT), :]) * scale + bias_ref[...]
    s_ctx = _dot_nt(q, k_ref[pl.ds(T_LAT, T_CTX), :]) * scale
    m = jnp.maximum(jnp.max(s_win, axis=-1, keepdims=True), jnp.max(s_ctx, axis=-1, keepdims=True))
    p_win = jnp.exp(s_win - m)
    p_ctx = jnp.exp(s_ctx - m)
    denom = jnp.sum(p_win, axis=-1, keepdims=True) + jnp.sum(p_ctx, axis=-1, keepdims=True)
    o = (_dot(p_win.astype(BF16), v_ref[pl.ds(start, NA_KT), :])
         + _dot(p_ctx.astype(BF16), v_ref[pl.ds(T_LAT, T_CTX), :]))
    o_ref[...] = (o / denom).astype(o_ref.dtype)


def _na_bias_index():
    out = []
    for i_rep in (0, 1, GRID_ROWS // NA_QROWS - 1):
        r0 = i_rep * NA_QROWS
        w0 = int(np.clip(r0 - WIN_ROWS // 2, 0, GRID_ROWS - NA_KROWS))
        r = r0 + np.arange(NA_QROWS)[:, None, None, None]
        c = np.arange(GRID_W)[None, :, None, None]
        kr = w0 + np.arange(NA_KROWS)[None, None, :, None]
        kc = np.arange(GRID_W)[None, None, None, :]
        rstart = np.clip(r - WIN_ROWS // 2, 0, GRID_ROWS - WIN_ROWS)
        cstart = np.clip(c - WIN_COLS // 2, 0, GRID_W - WIN_COLS)
        valid = (kr >= rstart) & (kr < rstart + WIN_ROWS) & (kc >= cstart) & (kc < cstart + WIN_COLS)
        dr = np.clip(kr - r + WIN_ROWS - 1, 0, 2 * WIN_ROWS - 2)
        dc = np.clip(kc - c + WIN_COLS - 1, 0, 2 * WIN_COLS - 2)
        shape = (NA_QROWS, GRID_W, NA_KROWS, GRID_W)
        flat = np.broadcast_to(dr * (2 * WIN_COLS - 1) + dc, shape).reshape(NA_QT, NA_KT)
        out.append((flat, np.broadcast_to(valid, shape).reshape(NA_QT, NA_KT)))
    idx = np.stack([o[0] for o in out]).astype(np.int32)
    valid = np.stack([o[1] for o in out])
    return idx, valid


def _na_bias_table(rpb):
    idx, valid = _na_bias_index()
    flat = rpb.reshape(NA_HEADS, -1)
    tab = jnp.take(flat, jnp.asarray(idx), axis=1)
    tab = jnp.where(jnp.asarray(valid)[None], tab, NEG_BIG)
    return tab.transpose(1, 0, 2, 3)


def _na(p, bias):
    n_blk = T_LAT // NA_QT
    kind = lambda i: jnp.where(i == 0, 0, jnp.where(i == n_blk - 1, 2, 1))
    return pl.pallas_call(
        _na_kernel,
        grid=(BATCH, NA_HEADS, n_blk),
        in_specs=[pl.BlockSpec((None, NA_QT, LANE), lambda b, h, i: (b, i, h)),
                  pl.BlockSpec((None, T_ALL, LANE), lambda b, h, i: (b, 0, 20 + h)),
                  pl.BlockSpec((None, T_ALL, LANE), lambda b, h, i: (b, 0, 28 + h)),
                  pl.BlockSpec((None, None, NA_QT, NA_KT), lambda b, h, i: (kind(i), h, 0, 0))],
        out_specs=pl.BlockSpec((None, NA_QT, LANE), lambda b, h, i: (b, i, h)),
        out_shape=jax.ShapeDtypeStruct((BATCH, T_ALL, NA_DIM), BF16),
        compiler_params=_cparams(3),
        name="nbr_attention",
    )(p, p, p, bias)


def _gla_kernel(q_ref, k_ref, v_ref, g_ref, lr_ref, whi_ref, wlo_ref, bup_ref, gain_ref,
                o_ref, acc_ref):
    c = GLA_CHUNK
    dv2 = 2 * GLA_DV
    qscale = GLA_DK ** -0.5
    row = lax.broadcasted_iota(jnp.int32, (c, c), 0)
    col = lax.broadcasted_iota(jnp.int32, (c, c), 1)
    lane = lax.broadcasted_iota(jnp.int32, (1, LANE), 1)
    second_head_lane = lane >= GLA_DK
    st_mask = ((lax.broadcasted_iota(jnp.int32, (dv2, LANE), 0) >= GLA_DV)
               == (lax.broadcasted_iota(jnp.int32, (dv2, LANE), 1) >= GLA_DK))
    gain = gain_ref[...]

    for d in range(2):
        tri = (row >= col) if d == 0 else (col >= row)
        tri_b = jnp.where(tri, 1.0, 0.0).astype(BF16)

        def chunk(i, st, base, n, d=d, tri=tri, tri_b=tri_b):
            idx = i if d == 0 else n - 1 - i
            start = pl.multiple_of(base + idx * c, c)
            lr = lr_ref[pl.ds(start, c), :]
            z = _dot(lr, whi_ref[d]) + _dot(lr, wlo_ref[d]) + bup_ref[d]
            la = (jnp.minimum(z, 0.0) - jnp.log(1.0 + jnp.exp(-jnp.abs(z)))) * (1.0 / GLA_TAU)
            la_hi = la.astype(BF16)
            r1 = la - la_hi.astype(F32)
            la_mid = r1.astype(BF16)
            la_lo = (r1 - la_mid.astype(F32)).astype(BF16)
            b = _dot(tri_b, la_hi) + _dot(tri_b, la_mid) + _dot(tri_b, la_lo)
            btot = b[c - 1:c, :] if d == 0 else b[0:1, :]
            q = q_ref[pl.ds(start, c), :].astype(F32) * qscale
            k = k_ref[pl.ds(start, c), :].astype(F32)
            v = v_ref[pl.ds(start, c), :]
            qi = q * jnp.exp(b)
            ki = (k * jnp.exp(-b)).astype(BF16)
            ko = (k * jnp.exp(btot - b)).astype(BF16)
            outs = []
            for j in range(2):
                qm = jnp.where(second_head_lane == (j == 1), qi, 0.0).astype(BF16)
                att = jnp.where(tri, _dot_nt(qm, ki), 0.0)
                outs.append(_dot(att.astype(BF16), v[:, j * GLA_DV:(j + 1) * GLA_DV]))
            o = jnp.concatenate(outs, axis=1) + _dot_nt(qi.astype(BF16), st.astype(BF16))
            st = st * jnp.exp(btot) + jnp.where(st_mask, _dot_tn(v, ko), 0.0)
            if d == 0:
                acc_ref[pl.ds(start, c), :] = o
            else:
                o = acc_ref[pl.ds(start, c), :] + o
                gate = g_ref[pl.ds(start, c), :].astype(F32)
                on = jnp.concatenate([_rms(o[:, :GLA_DV]), _rms(o[:, GLA_DV:])], axis=1) * gain
                o_ref[pl.ds(start, c), :] = (on * _silu(gate)).astype(o_ref.dtype)
            return st

        st = jnp.zeros((dv2, LANE), F32)
        st = lax.fori_loop(0, T_CTX // c,
                           functools.partial(chunk, base=T_LAT, n=T_CTX // c), st)
        lax.fori_loop(0, T_LAT // c, functools.partial(chunk, base=0, n=T_LAT // c), st)


def _gla(p, w_hi, w_lo, b_up, gla_norm):
    dv2 = 2 * GLA_DV
    return pl.pallas_call(
        _gla_kernel,
        grid=(BATCH, GLA_HEADS // 2),
        in_specs=[pl.BlockSpec((None, T_ALL, LANE), lambda b, j: (b, 0, 8 + j)),
                  pl.BlockSpec((None, T_ALL, LANE), lambda b, j: (b, 0, 36 + j)),
                  pl.BlockSpec((None, T_ALL, dv2), lambda b, j: (b, 0, 20 + j)),
                  pl.BlockSpec((None, T_ALL, dv2), lambda b, j: (b, 0, 6 + j)),
                  pl.BlockSpec((None, T_ALL, LANE), lambda b, j: (b, 0, 48)),
                  pl.BlockSpec((2, LANE, LANE), lambda b, j: (0, 0, j)),
                  pl.BlockSpec((2, LANE, LANE), lambda b, j: (0, 0, j)),
                  pl.BlockSpec((2, 1, LANE), lambda b, j: (0, 0, j)),
                  pl.BlockSpec((1, dv2), lambda b, j: (0, j))],
        out_specs=pl.BlockSpec((None, T_ALL, dv2), lambda b, j: (b, 0, j)),
        out_shape=jax.ShapeDtypeStruct((BATCH, T_ALL, GLA_V_DIM), BF16),
        scratch_shapes=[pltpu.VMEM((T_ALL, dv2), F32)],
        compiler_params=_cparams(2),
        name="gla",
    )(p, p, p, p, p, w_hi, w_lo, b_up, gla_norm)


def _out_kernel(a1_ref, a2_ref, w1_ref, w2_ref, x_ref, g_ref, gain_ref, sc_ref, sh_ref, rw_ref,
                xo_ref, h_ref, lg_ref):
    y = _dot(a1_ref[...], w1_ref[...]) + _dot(a2_ref[...], w2_ref[...])
    xn = x_ref[...] + g_ref[...] * y
    xo_ref[...] = xn
    h = _rms(xn) * gain_ref[...] * (1.0 + sc_ref[...]) + sh_ref[...]
    h_ref[...] = h.astype(h_ref.dtype)
    lg_ref[...] = jnp.dot(h, rw_ref[...], preferred_element_type=F32,
                          precision=lax.Precision.HIGHEST)


def _out_proj(a1, a2, w_out_bf16, x, mod, norm_gain, router_w, layer, n_tiles):
    half = w_out_bf16.shape[0] // 2
    t_out = n_tiles * ROW_TILE
    row = lambda width: pl.BlockSpec((None, ROW_TILE, width), lambda b, i: (b, i, 0))
    return pl.pallas_call(
        _out_kernel,
        grid=(BATCH, n_tiles),
        in_specs=[row(half), row(half),
                  pl.BlockSpec((half, D_MODEL), lambda b, i: (0, 0)),
                  pl.BlockSpec((half, D_MODEL), lambda b, i: (1, 0)),
                  row(D_MODEL), _mod_spec(2),
                  pl.BlockSpec((1, D_MODEL), lambda b, i: (0, 0)),
                  _mod_spec(4), _mod_spec(3),
                  pl.BlockSpec((None, D_MODEL, N_EXPERTS), lambda b, i: (layer, 0, 0))],
        out_specs=[row(D_MODEL), row(D_MODEL), row(N_EXPERTS)],
        out_shape=[jax.ShapeDtypeStruct((BATCH, t_out, D_MODEL), F32),
                   jax.ShapeDtypeStruct((BATCH, t_out, D_MODEL), BF16),
                   jax.ShapeDtypeStruct((BATCH, t_out, N_EXPERTS), F32)],
        compiler_params=_cparams(2),
        name="out_proj",
    )(a1, a2, w_out_bf16, w_out_bf16, x, mod, norm_gain, mod, mod, router_w)


def _moe_up_kernel(x_ref, wg_ref, wu_ref, o_ref):
    x = x_ref[...]
    a = _dot(x, wg_ref[...].astype(BF16))
    u = _dot(x, wu_ref[...].astype(BF16))
    o_ref[...] = (_silu(a) * u).astype(o_ref.dtype)


def _moe_down_kernel(h_ref, wd_ref, g_ref, o_ref):
    o_ref[...] = _dot(h_ref[...], wd_ref[...].astype(BF16)) * g_ref[...]


def _moe(xs, gates, w_gate, w_up, w_down, layer, tm, tn):
    e, m, _ = xs.shape
    wspec = pl.BlockSpec((None, None, D_MODEL, tn), lambda e, f, i: (layer, e, 0, f))
    hmid = pl.pallas_call(
        _moe_up_kernel,
        grid=(e, D_EXPERT // tn, m // tm),
        in_specs=[pl.BlockSpec((None, tm, D_MODEL), lambda e, f, i: (e, i, 0)), wspec, wspec],
        out_specs=pl.BlockSpec((None, tm, tn), lambda e, f, i: (e, i, f)),
        out_shape=jax.ShapeDtypeStruct((e, m, D_EXPERT), BF16),
        compiler_params=_cparams(3),
        name="moe_up",
    )(xs, w_gate, w_up)
    return pl.pallas_call(
        _moe_down_kernel,
        grid=(e, D_MODEL // tn, m // tm),
        in_specs=[pl.BlockSpec((None, tm, D_EXPERT), lambda e, f, i: (e, i, 0)),
                  pl.BlockSpec((None, None, D_EXPERT, tn), lambda e, f, i: (layer, e, 0, f)),
                  pl.BlockSpec((None, tm, 1), lambda e, f, i: (e, i, 0))],
        out_specs=pl.BlockSpec((None, tm, tn), lambda e, f, i: (e, i, f)),
        out_shape=jax.ShapeDtypeStruct((e, m, D_MODEL), F32),
        compiler_params=_cparams(3),
        name="moe_down",
    )(hmid, w_down, gates)


def _res_mid_kernel(x_ref, m_ref, g_ref, gain_ref, sc_ref, sh_ref, xo_ref, h_ref):
    xn = x_ref[...] + g_ref[...] * m_ref[...]
    xo_ref[...] = xn
    h_ref[...] = (_rms(xn) * gain_ref[...] * (1.0 + sc_ref[...]) + sh_ref[...]).astype(h_ref.dtype)


def _res_mid(x, moe_out, mod, mod_next, norm_gain_next):
    row = pl.BlockSpec((None, ROW_TILE, D_MODEL), lambda b, i: (b, i, 0))
    return pl.pallas_call(
        _res_mid_kernel,
        grid=(BATCH, N_ALL_TILES),
        in_specs=[row, row, _mod_spec(5), pl.BlockSpec((1, D_MODEL), lambda b, i: (0, 0)),
                  _mod_spec(1), _mod_spec(0)],
        out_specs=[row, row],
        out_shape=[jax.ShapeDtypeStruct((BATCH, T_ALL, D_MODEL), F32),
                   jax.ShapeDtypeStruct((BATCH, T_ALL, D_MODEL), BF16)],
        compiler_params=_cparams(2),
        name="ffn_residual_prenorm",
    )(x, moe_out, mod, norm_gain_next, mod_next, mod_next)


def _res_final_kernel(x_ref, m_ref, g_ref, gain_ref, o_ref):
    xn = x_ref[...] + g_ref[...] * m_ref[...]
    o_ref[...] = _rms(xn) * gain_ref[...]


def _res_final(x, moe_out, mod, norm_final):
    row = pl.BlockSpec((None, ROW_TILE, D_MODEL), lambda b, i: (b, i, 0))
    return pl.pallas_call(
        _res_final_kernel,
        grid=(BATCH, N_LAT_TILES),
        in_specs=[row, row, _mod_spec(5), pl.BlockSpec((1, D_MODEL), lambda b, i: (0, 0))],
        out_specs=row,
        out_shape=jax.ShapeDtypeStruct((BATCH, T_LAT, D_MODEL), F32),
        compiler_params=_cparams(2),
        name="ffn_residual_final_norm",
    )(x, moe_out, mod, norm_final)


def _rope_tables():
    half = HEAD_DIM // 2
    nf = half // 2
    t = np.arange(T_LAT)
    rows, cols = t // GRID_W, t % GRID_W
    freq = ROPE_BASE ** (-jnp.arange(nf, dtype=F32) / nf)
    ang_r = jnp.asarray(rows, F32)[:, None] * freq
    ang_c = jnp.asarray(cols, F32)[:, None] * freq
    zeros = jnp.zeros_like(ang_r)
    cos = jnp.concatenate([jnp.cos(ang_r), jnp.cos(ang_r), jnp.cos(ang_c), jnp.cos(ang_c)], axis=1)
    s1 = jnp.concatenate([-jnp.sin(ang_r), zeros, -jnp.sin(ang_c), zeros], axis=1)
    s2 = jnp.concatenate([zeros, jnp.sin(ang_r), zeros, jnp.sin(ang_c)], axis=1)
    return cos, s1, s2


def _route(logits, t0, t1, cap):
    aff = jax.nn.softmax(logits[:, t0:t1], axis=-1)
    g, idx = lax.top_k(aff.transpose(0, 2, 1), cap)
    return g, idx + t0


def _expert_ffn(h, logits, w_gate, w_up, w_down, layer, with_ctx):
    g, idx = _route(logits, 0, T_LAT, EC_CAPACITY * T_LAT // N_EXPERTS)
    if with_ctx:
        gc, idxc = _route(logits, T_LAT, T_ALL, EC_CAPACITY * T_CTX // N_EXPERTS)
        g = jnp.concatenate([g, gc], axis=-1)
        idx = jnp.concatenate([idx, idxc], axis=-1)
    cap = idx.shape[-1]
    t_rows = h.shape[1]
    flat = (idx + jnp.arange(BATCH)[:, None, None] * t_rows).transpose(1, 0, 2).reshape(N_EXPERTS, -1)
    gates = g.transpose(1, 0, 2).reshape(N_EXPERTS, BATCH * cap, 1)
    xs = h.reshape(BATCH * t_rows, D_MODEL)[flat]
    tm = BATCH * cap // 2
    y = _moe(xs, gates, w_gate, w_up, w_down, layer, tm, 512)
    out = jnp.zeros((BATCH * t_rows, D_MODEL), F32).at[flat.reshape(-1)].add(
        y.reshape(-1, D_MODEL))
    return out.reshape(BATCH, t_rows, D_MODEL)


def kernel(x, c, ctx, c_ctx, ada_w, ada_b, norm_mix, norm_ffn, norm_final, ev_w_in, ev_w_out,
           ret_gamma_logit, ret_norm, gmlp_norm, gmlp_ws, gmlp_bs, od_w_in, od_w_out, na_rpb,
           gla_w_up, gla_b_up, gla_norm, router_w, moe_w_gate, moe_w_up, moe_w_down):
    assert x.shape == (BATCH, T_LAT, D_MODEL) and ctx.shape == (BATCH, T_CTX, D_MODEL)
    assert ada_w.shape[0] == 2 and ev_w_in.shape[0] == 1 and od_w_in.shape[0] == 1

    cond = jnp.concatenate([jax.nn.silu(c), jax.nn.silu(c_ctx)[None],
                            jnp.zeros((11, D_MODEL), F32)], axis=0).astype(BF16)
    mods = []
    for l in range(2):
        m = _matmul(cond, ada_w[l].astype(BF16), 16, 2048, F32) + ada_b[l]
        m = m[:BATCH + 1].reshape(BATCH + 1, 6, D_MODEL)
        m_ctx = jnp.broadcast_to(m[BATCH][None], (BATCH, 6, D_MODEL))
        mods.append(jnp.stack([m[:BATCH], m_ctx], axis=1)[:, :, :, None, :])

    xa = jnp.concatenate([x, ctx], axis=1)
    cos, s1, s2 = _rope_tables()

    h = _modnorm(xa, norm_mix[0:1], mods[0], 1, 0)
    p = _matmul(h.reshape(BATCH * T_ALL, D_MODEL), ev_w_in[0].astype(BF16), 1024, 1024, BF16)
    p = p.reshape(BATCH, T_ALL, EVEN_COLS)
    log_g = jax.nn.log_sigmoid(ret_gamma_logit[0].astype(F32))
    ret = _retention(p, log_g, cos, s1, s2, ret_norm[0:1])
    gm = _gmlp(p, gmlp_norm[0:1], gmlp_ws[0].astype(BF16), gmlp_bs[0].T)
    xa, h2, logits = _out_proj(ret, gm, ev_w_out[0].astype(BF16), xa, mods[0], norm_ffn[0:1],
                               router_w, 0, N_ALL_TILES)
    moe_out = _expert_ffn(h2, logits, moe_w_gate, moe_w_up, moe_w_down, 0, True)
    xa, h = _res_mid(xa, moe_out, mods[0], mods[1], norm_mix[1:2])

    w_in = jnp.pad(od_w_in[0], ((0, 0), (0, ODD_COLS_PAD - ODD_COLS))).astype(BF16)
    p = _matmul(h.reshape(BATCH * T_ALL, D_MODEL), w_in, 1024, 896, BF16)
    p = p.reshape(BATCH, T_ALL, ODD_COLS_PAD)
    o_na = _na(p, _na_bias_table(na_rpb[0]))
    w_up_full = jnp.zeros((2, LANE, GLA_QK_DIM), F32)
    for d in range(2):
        w_up_full = w_up_full.at[d, d * GLA_GATE_RANK:(d + 1) * GLA_GATE_RANK].set(gla_w_up[0, d])
    w_hi = w_up_full.astype(BF16)
    w_lo = (w_up_full - w_hi.astype(F32)).astype(BF16)
    gla = _gla(p, w_hi, w_lo, gla_b_up[0][:, None, :], gla_norm[0:1])
    xl, h2, logits = _out_proj(o_na, gla, od_w_out[0].astype(BF16), xa, mods[1], norm_ffn[1:2],
                               router_w, 1, N_LAT_TILES)
    moe_out = _expert_ffn(h2, logits, moe_w_gate, moe_w_up, moe_w_down, 1, False)
    return _res_final(xl, moe_out, mods[1], norm_final[None])
```

```python
import functools

import numpy as np
import jax
import jax.numpy as jnp
from jax import lax
from jax.experimental import pallas as pl
from jax.experimental.pallas import tpu as pltpu

D_MODEL = 2048
BATCH = 4
T_LAT = 4096
T_CTX = 256
T_ALL = T_LAT + T_CTX
GRID_W = 64
GRID_ROWS = T_LAT // GRID_W
HEAD_DIM = 128
RET_HEADS = 8
RET_DIM = 1024
GMLP_DIM = 1024
GMLP_GROUPS = 8
GMLP_CHUNK = 128
NA_HEADS = 8
NA_DIM = 1024
WIN_ROWS = 8
WIN_COLS = 16
GLA_HEADS = 8
GLA_DK = 64
GLA_DV = 128
GLA_QK_DIM = 512
GLA_V_DIM = 1024
GLA_GATE_RANK = 16
GLA_TAU = 16.0
N_EXPERTS = 16
D_EXPERT = 2048
EC_CAPACITY = 2
ROPE_BASE = 10000.0
RMS_EPS = 1e-6
EVEN_COLS = 6144
ODD_COLS = 6176
ODD_COLS_PAD = 6272
LANE = 128

ROW_TILE = 256
MIX_TILE = 544
LAT_TILE = 512
RET_CHUNK = 256
GLA_SUB = 64
GLA_SUPER = 256
NA_QROWS = 4
NA_QT = NA_QROWS * GRID_W
NA_KROWS = NA_QROWS + WIN_ROWS - 1
NA_KT = NA_KROWS * GRID_W
NEG_BIG = -1e30
MOE_TN = 512
VMEM_LIMIT = 56 * 1024 * 1024

F32 = jnp.float32
BF16 = jnp.bfloat16


def _cparams(n_axes):
    return pltpu.CompilerParams(dimension_semantics=("arbitrary",) * n_axes,
                                vmem_limit_bytes=VMEM_LIMIT)


def _dot(a, b):
    return jnp.dot(a, b, preferred_element_type=F32)


def _dot_nt(a, b):
    return lax.dot_general(a, b, (((1,), (1,)), ((), ())), preferred_element_type=F32)


def _dot_tn(a, b):
    return lax.dot_general(a, b, (((0,), (0,)), ((), ())), preferred_element_type=F32)


def _silu(x):
    return x * jax.nn.sigmoid(x)


def _rms(x):
    return x * lax.rsqrt(jnp.mean(x * x, axis=-1, keepdims=True) + RMS_EPS)


def _split3(x):
    hi = x.astype(BF16)
    r1 = x - hi.astype(F32)
    mid = r1.astype(BF16)
    lo = (r1 - mid.astype(F32)).astype(BF16)
    return hi, mid, lo


def _mm_kernel(a_ref, w_ref, o_ref):
    o_ref[...] = _dot(a_ref[...], w_ref[...]).astype(o_ref.dtype)


def _matmul(a, w, tm, tn, out_dtype):
    m, k = a.shape
    n = w.shape[1]
    return pl.pallas_call(
        _mm_kernel,
        grid=(n // tn, m // tm),
        in_specs=[pl.BlockSpec((tm, k), lambda j, i: (i, 0)),
                  pl.BlockSpec((k, tn), lambda j, i: (0, j))],
        out_specs=pl.BlockSpec((tm, tn), lambda j, i: (i, j)),
        out_shape=jax.ShapeDtypeStruct((m, n), out_dtype),
        compiler_params=_cparams(2),
        name="matmul",
    )(a, w)


def _lat_spec(which):
    return pl.BlockSpec((None, None, 1, D_MODEL), lambda b, i: (b, which, 0, 0))


def _ctx_spec(which):
    return pl.BlockSpec((None, 1, D_MODEL), lambda b, i: (which, 0, 0))


def _region_select(tile, lat_ref, ctx_ref):
    rows = pl.program_id(1) * tile + lax.broadcasted_iota(jnp.int32, (tile, 1), 0)
    return jnp.where(rows >= T_LAT, ctx_ref[...], lat_ref[...])


def _modnorm_kernel(x_ref, gain_ref, scl_ref, scc_ref, shl_ref, shc_ref, o_ref, *, tile):
    sc = _region_select(tile, scl_ref, scc_ref)
    sh = _region_select(tile, shl_ref, shc_ref)
    o_ref[...] = (_rms(x_ref[...]) * gain_ref[...] * (1.0 + sc) + sh).astype(o_ref.dtype)


def _modnorm(x, gain, mod_lat, mod_ctx, sc_i, sh_i):
    tile = MIX_TILE
    row = pl.BlockSpec((None, tile, D_MODEL), lambda b, i: (b, i, 0))
    return pl.pallas_call(
        functools.partial(_modnorm_kernel, tile=tile),
        grid=(BATCH, T_ALL // tile),
        in_specs=[row, pl.BlockSpec((1, D_MODEL), lambda b, i: (0, 0)),
                  _lat_spec(sc_i), _ctx_spec(sc_i), _lat_spec(sh_i), _ctx_spec(sh_i)],
        out_specs=row,
        out_shape=jax.ShapeDtypeStruct((BATCH, T_ALL, D_MODEL), BF16),
        compiler_params=_cparams(2),
        name="modnorm",
    )(x, gain, mod_lat, mod_ctx, mod_lat, mod_ctx)


def _rotary(z, cos, s1, s2):
    return z * cos + pltpu.roll(z, 96, 1) * s1 + pltpu.roll(z, 32, 1) * s2


def _ret_kernel(lg_ref, q_ref, g_ref, k_ref, v_ref, cos_ref, s1_ref, s2_ref, gain_ref,
                o_ref, acc_ref, qs_ref, ks_ref):
    c = RET_CHUNK
    h = pl.program_id(1)
    scale = HEAD_DIM ** -0.5
    la_f = lg_ref[0, h]
    la_b = lg_ref[1, h]
    dist = (lax.broadcasted_iota(jnp.int32, (c, c), 0)
            - lax.broadcasted_iota(jnp.int32, (c, c), 1)).astype(F32)
    tcol = lax.broadcasted_iota(jnp.int32, (c, 1), 0).astype(F32)
    dmat = (jnp.where(dist >= 0.0, jnp.exp(la_f * jnp.maximum(dist, 0.0)), 0.0)
            + jnp.where(dist <= 0.0, jnp.exp(la_b * jnp.maximum(-dist, 0.0)), 0.0))
    qdec_f = jnp.exp(la_f * (tcol + 1.0))
    kdec_f = jnp.exp(la_f * (c - 1.0 - tcol))
    cdec_f = jnp.exp(jnp.full((1, LANE), la_f * c, F32))
    qdec_b = jnp.exp(la_b * (c - tcol))
    kdec_b = jnp.exp(la_b * tcol)
    cdec_b = jnp.exp(jnp.full((1, LANE), la_b * c, F32))
    gain = gain_ref[...]

    def fwd(i, st, base, rot):
        start = pl.multiple_of(base + i * c, c)
        q = q_ref[pl.ds(start, c), :].astype(F32) * scale
        k = k_ref[pl.ds(start, c), :].astype(F32)
        v = v_ref[pl.ds(start, c), :]
        if rot:
            cos, s1, s2 = (cos_ref[pl.ds(start, c), :], s1_ref[pl.ds(start, c), :],
                           s2_ref[pl.ds(start, c), :])
            q = _rotary(q, cos, s1, s2)
            k = _rotary(k, cos, s1, s2)
        qs_ref[pl.ds(start, c), :] = q
        ks_ref[pl.ds(start, c), :] = k
        att = _dot_nt(q.astype(BF16), k.astype(BF16)) * dmat
        acc_ref[pl.ds(start, c), :] = (_dot(att.astype(BF16), v)
                                       + _dot_nt((q * qdec_f).astype(BF16), st.astype(BF16)))
        return cdec_f * st + _dot_tn(v, (k * kdec_f).astype(BF16))

    def bwd(i, st, base, n):
        start = pl.multiple_of(base + (n - 1 - i) * c, c)
        q = qs_ref[pl.ds(start, c), :]
        k = ks_ref[pl.ds(start, c), :]
        v = v_ref[pl.ds(start, c), :]
        o = acc_ref[pl.ds(start, c), :] + _dot_nt((q * qdec_b).astype(BF16), st.astype(BF16))
        gate = g_ref[pl.ds(start, c), :].astype(F32)
        o_ref[pl.ds(start, c), :] = (_rms(o) * gain * _silu(gate)).astype(o_ref.dtype)
        return cdec_b * st + _dot_tn(v, (k * kdec_b).astype(BF16))

    zero = jnp.zeros((HEAD_DIM, HEAD_DIM), F32)
    n_ctx, n_lat = T_CTX // c, T_LAT // c
    st = lax.fori_loop(0, n_ctx, functools.partial(fwd, base=T_LAT, rot=False), zero)
    lax.fori_loop(0, n_lat, functools.partial(fwd, base=0, rot=True), st, unroll=2)
    st = lax.fori_loop(0, n_ctx, functools.partial(bwd, base=T_LAT, n=n_ctx), zero)
    lax.fori_loop(0, n_lat, functools.partial(bwd, base=0, n=n_lat), st, unroll=2)


def _retention(p, log_g, cos, s1, s2, ret_norm):
    col = lambda off: pl.BlockSpec((None, T_ALL, LANE), lambda b, h, lg: (b, 0, off + h))
    tab = pl.BlockSpec((T_LAT, LANE), lambda b, h, lg: (0, 0))
    return pl.pallas_call(
        _ret_kernel,
        grid_spec=pltpu.PrefetchScalarGridSpec(
            num_scalar_prefetch=1,
            grid=(BATCH, RET_HEADS),
            in_specs=[col(0), col(8), col(32), col(40), tab, tab, tab,
                      pl.BlockSpec((1, LANE), lambda b, h, lg: (0, h))],
            out_specs=pl.BlockSpec((None, T_ALL, LANE), lambda b, h, lg: (b, 0, h)),
            scratch_shapes=[pltpu.VMEM((T_ALL, LANE), F32), pltpu.VMEM((T_ALL, LANE), F32),
                            pltpu.VMEM((T_ALL, LANE), F32)]),
        out_shape=jax.ShapeDtypeStruct((BATCH, T_ALL, RET_DIM), BF16),
        compiler_params=_cparams(2),
        name="retention",
    )(log_g, p, p, p, p, cos, s1, s2, ret_norm)


def _gmlp_kernel(u_ref, v_ref, gain_ref, ws_ref, bst_ref, o_ref):
    u = jax.nn.gelu(u_ref[...].astype(F32))
    v = _rms(jax.nn.gelu(v_ref[...].astype(F32))) * gain_ref[...]
    vb = v.astype(BF16)
    for n in range(ROW_TILE // GMLP_CHUNK):
        r0 = n * GMLP_CHUNK
        for g in range(GMLP_GROUPS):
            c0 = g * LANE
            mixed = (_dot(ws_ref[g], vb[r0:r0 + GMLP_CHUNK, c0:c0 + LANE])
                     + bst_ref[:, g:g + 1])
            o_ref[r0:r0 + GMLP_CHUNK, c0:c0 + LANE] = (
                u[r0:r0 + GMLP_CHUNK, c0:c0 + LANE] * mixed).astype(o_ref.dtype)


def _gmlp(p, gmlp_norm, ws_bf16, bs_t):
    return pl.pallas_call(
        _gmlp_kernel,
        grid=(BATCH, T_ALL // ROW_TILE),
        in_specs=[pl.BlockSpec((None, ROW_TILE, GMLP_DIM), lambda b, i: (b, i, 2)),
                  pl.BlockSpec((None, ROW_TILE, GMLP_DIM), lambda b, i: (b, i, 3)),
                  pl.BlockSpec((1, GMLP_DIM), lambda b, i: (0, 0)),
                  pl.BlockSpec((GMLP_GROUPS, GMLP_CHUNK, GMLP_CHUNK), lambda b, i: (0, 0, 0)),
                  pl.BlockSpec((GMLP_CHUNK, GMLP_GROUPS), lambda b, i: (0, 0))],
        out_specs=pl.BlockSpec((None, ROW_TILE, GMLP_DIM), lambda b, i: (b, i, 0)),
        out_shape=jax.ShapeDtypeStruct((BATCH, T_ALL, GMLP_DIM), BF16),
        compiler_params=_cparams(2),
        name="gmlp",
    )(p, p, gmlp_norm, ws_bf16, bs_t)


def _na_kernel(q_ref, k_ref, v_ref, bias_ref, o_ref):
    i = pl.program_id(2)
    w0 = jnp.clip(i * NA_QROWS - WIN_ROWS // 2, 0, GRID_ROWS - NA_KROWS)
    start = pl.multiple_of(w0 * GRID_W, GRID_W)
    scale = HEAD_DIM ** -0.5
    q = q_ref[...]
    s_win = _dot_nt(q, k_ref[pl.ds(start, NA_KT), :]) * scale + bias_ref[...]
    s_ctx = _dot_nt(q, k_ref[pl.ds(T_LAT, T_CTX), :]) * scale
    m = jnp.maximum(jnp.max(s_win, axis=-1, keepdims=True), jnp.max(s_ctx, axis=-1, keepdims=True))
    p_win = jnp.exp(s_win - m)
    p_ctx = jnp.exp(s_ctx - m)
    denom = jnp.sum(p_win, axis=-1, keepdims=True) + jnp.sum(p_ctx, axis=-1, keepdims=True)
    o = (_dot(p_win.astype(BF16), v_ref[pl.ds(start, NA_KT), :])
         + _dot(p_ctx.astype(BF16), v_ref[pl.ds(T_LAT, T_CTX), :]))
    o_ref[...] = (o / denom).astype(o_ref.dtype)


def _na_geometry():
    n_blk = GRID_ROWS // NA_QROWS
    dr_idx = np.zeros((3, NA_QROWS, NA_KROWS), np.int32)
    row_ok = np.zeros((3, NA_QROWS, NA_KROWS), bool)
    for kind, i_rep in enumerate((0, 1, n_blk - 1)):
        r0 = i_rep * NA_QROWS
        w0 = int(np.clip(r0 - WIN_ROWS // 2, 0, GRID_ROWS - NA_KROWS))
        r = r0 + np.arange(NA_QROWS)[:, None]
        kr = w0 + np.arange(NA_KROWS)[None, :]
        rstart = np.clip(r - WIN_ROWS // 2, 0, GRID_ROWS - WIN_ROWS)
        row_ok[kind] = (kr >= rstart) & (kr < rstart + WIN_ROWS)
        dr_idx[kind] = np.clip(kr - r + WIN_ROWS - 1, 0, 2 * WIN_ROWS - 2)
    c = np.arange(GRID_W)[:, None]
    kc = np.arange(GRID_W)[None, :]
    cstart = np.clip(c - WIN_COLS // 2, 0, GRID_W - WIN_COLS)
    col_ok = (kc >= cstart) & (kc < cstart + WIN_COLS)
    valid = row_ok[:, :, None, :, None] & col_ok[None, None, :, None, :]
    return dr_idx, valid.reshape(3, NA_QT, NA_KT)


def _na_bias_table(rpb):
    n_dr, n_dc = 2 * WIN_ROWS - 1, 2 * WIN_COLS - 1
    w = jnp.concatenate([rpb[..., WIN_COLS - 1:], jnp.zeros((NA_HEADS, n_dr, LANE - n_dc), F32),
                         rpb[..., :WIN_COLS - 1]], axis=-1)
    toep = jnp.tile(w, (1, 1, GRID_W))[..., :GRID_W * (LANE - 1)]
    toep = toep.reshape(NA_HEADS, n_dr, GRID_W, LANE - 1)[..., :GRID_W]
    dr_idx, valid = _na_geometry()
    tab = toep[:, dr_idx.reshape(-1)].reshape(NA_HEADS, 3, NA_QROWS, NA_KROWS, GRID_W, GRID_W)
    tab = tab.transpose(1, 0, 2, 4, 3, 5).reshape(3, NA_HEADS, NA_QT, NA_KT)
    return jnp.where(jnp.asarray(valid)[:, None], tab, NEG_BIG)


def _na(p, bias):
    n_blk = T_LAT // NA_QT
    kind = lambda i: jnp.where(i == 0, 0, jnp.where(i == n_blk - 1, 2, 1))
    return pl.pallas_call(
        _na_kernel,
        grid=(BATCH, NA_HEADS, n_blk),
        in_specs=[pl.BlockSpec((None, NA_QT, LANE), lambda b, h, i: (b, i, h)),
                  pl.BlockSpec((None, T_ALL, LANE), lambda b, h, i: (b, 0, 20 + h)),
                  pl.BlockSpec((None, T_ALL, LANE), lambda b, h, i: (b, 0, 28 + h)),
                  pl.BlockSpec((None, None, NA_QT, NA_KT), lambda b, h, i: (kind(i), h, 0, 0))],
        out_specs=pl.BlockSpec((None, NA_QT, LANE), lambda b, h, i: (b, i, h)),
        out_shape=jax.ShapeDtypeStruct((BATCH, T_ALL, NA_DIM), BF16),
        compiler_params=_cparams(3),
        name="nbr_attention",
    )(p, p, p, bias)


def _gla_kernel(q_ref, k_ref, v_ref, g_ref, lr_ref, whi_ref, wlo_ref, bup_ref, gain_ref,
                o_ref, acc_ref):
    r, c = GLA_SUPER, GLA_SUB
    n_sub = r // c
    dv2 = 2 * GLA_DV
    qscale = GLA_DK ** -0.5
    row = lax.broadcasted_iota(jnp.int32, (r, r), 0)
    col = lax.broadcasted_iota(jnp.int32, (r, r), 1)
    sub_shift = c.bit_length() - 1
    same = (row >> sub_shift) == (col >> sub_shift)
    blk_ones = jnp.where(same, 1.0, 0.0).astype(BF16)
    second_head_lane = lax.broadcasted_iota(jnp.int32, (1, LANE), 1) >= GLA_DK
    st_mask = ((lax.broadcasted_iota(jnp.int32, (dv2, LANE), 0) >= GLA_DV)
               == (lax.broadcasted_iota(jnp.int32, (dv2, LANE), 1) >= GLA_DK))
    gain = gain_ref[...]

    for d in range(2):
        tri = same & ((row >= col) if d == 0 else (col >= row))
        tri_b = jnp.where(tri, 1.0, 0.0).astype(BF16)

        def step(i, st, base, n, d=d, tri=tri, tri_b=tri_b):
            idx = i if d == 0 else n - 1 - i
            start = pl.multiple_of(base + idx * r, r)
            lr = lr_ref[pl.ds(start, r), :]
            z = _dot(lr, whi_ref[d]) + _dot(lr, wlo_ref[d]) + bup_ref[d]
            la = (jnp.minimum(z, 0.0) - jnp.log(1.0 + jnp.exp(-jnp.abs(z)))) * (1.0 / GLA_TAU)
            hi, mid, lo = _split3(la)
            b = _dot(tri_b, hi) + _dot(tri_b, mid) + _dot(tri_b, lo)
            bt = _dot(blk_ones, hi) + _dot(blk_ones, mid) + _dot(blk_ones, lo)
            q = q_ref[pl.ds(start, r), :].astype(F32) * qscale
            k = k_ref[pl.ds(start, r), :].astype(F32)
            v = v_ref[pl.ds(start, r), :]
            qi = q * jnp.exp(b)
            qib = qi.astype(BF16)
            ki = (k * jnp.exp(-b)).astype(BF16)
            ko = (k * jnp.exp(bt - b)).astype(BF16)
            dec = jnp.exp(bt)
            outs = []
            for j in range(2):
                qm = jnp.where(second_head_lane == (j == 1), qi, 0.0).astype(BF16)
                att = jnp.where(tri, _dot_nt(qm, ki), 0.0)
                outs.append(_dot(att.astype(BF16), v[:, j * GLA_DV:(j + 1) * GLA_DV]))
            o_intra = jnp.concatenate(outs, axis=1)
            parts = [None] * n_sub
            for s in (range(n_sub) if d == 0 else reversed(range(n_sub))):
                lo_r, hi_r = s * c, (s + 1) * c
                parts[s] = o_intra[lo_r:hi_r] + _dot_nt(qib[lo_r:hi_r], st.astype(BF16))
                st = (st * dec[lo_r:lo_r + 1, :]
                      + jnp.where(st_mask, _dot_tn(v[lo_r:hi_r], ko[lo_r:hi_r]), 0.0))
            o = jnp.concatenate(parts, axis=0)
            if d == 0:
                acc_ref[pl.ds(start, r), :] = o
            else:
                o = acc_ref[pl.ds(start, r), :] + o
                gate = g_ref[pl.ds(start, r), :].astype(F32)
                on = jnp.concatenate([_rms(o[:, :GLA_DV]), _rms(o[:, GLA_DV:])], axis=1) * gain
                o_ref[pl.ds(start, r), :] = (on * _silu(gate)).astype(o_ref.dtype)
            return st

        st = jnp.zeros((dv2, LANE), F32)
        st = lax.fori_loop(0, T_CTX // r, functools.partial(step, base=T_LAT, n=T_CTX // r), st)
        lax.fori_loop(0, T_LAT // r, functools.partial(step, base=0, n=T_LAT // r), st)


def _gla(p, w_hi, w_lo, b_up, gla_norm):
    dv2 = 2 * GLA_DV
    return pl.pallas_call(
        _gla_kernel,
        grid=(BATCH, GLA_HEADS // 2),
        in_specs=[pl.BlockSpec((None, T_ALL, LANE), lambda b, j: (b, 0, 8 + j)),
                  pl.BlockSpec((None, T_ALL, LANE), lambda b, j: (b, 0, 36 + j)),
                  pl.BlockSpec((None, T_ALL, dv2), lambda b, j: (b, 0, 20 + j)),
                  pl.BlockSpec((None, T_ALL, dv2), lambda b, j: (b, 0, 6 + j)),
                  pl.BlockSpec((None, T_ALL, LANE), lambda b, j: (b, 0, 48)),
                  pl.BlockSpec((2, LANE, LANE), lambda b, j: (0, 0, j)),
                  pl.BlockSpec((2, LANE, LANE), lambda b, j: (0, 0, j)),
                  pl.BlockSpec((2, 1, LANE), lambda b, j: (0, 0, j)),
                  pl.BlockSpec((1, dv2), lambda b, j: (0, j))],
        out_specs=pl.BlockSpec((None, T_ALL, dv2), lambda b, j: (b, 0, j)),
        out_shape=jax.ShapeDtypeStruct((BATCH, T_ALL, GLA_V_DIM), BF16),
        scratch_shapes=[pltpu.VMEM((T_ALL, dv2), F32)],
        compiler_params=_cparams(2),
        name="gla",
    )(p, p, p, p, p, w_hi, w_lo, b_up, gla_norm)


def _out_kernel(a1_ref, a2_ref, w1_ref, w2_ref, x_ref, gl_ref, gc_ref, gain_ref,
                scl_ref, scc_ref, shl_ref, shc_ref, rw_ref, xo_ref, h_ref, lg_ref, *, tile):
    y = _dot(a1_ref[...], w1_ref[...]) + _dot(a2_ref[...], w2_ref[...])
    xn = x_ref[...] + _region_select(tile, gl_ref, gc_ref) * y
    xo_ref[...] = xn
    h = (_rms(xn) * gain_ref[...] * (1.0 + _region_select(tile, scl_ref, scc_ref))
         + _region_select(tile, shl_ref, shc_ref))
    h_hi = h.astype(BF16)
    h_ref[...] = h_hi
    h_lo = (h - h_hi.astype(F32)).astype(BF16)
    lg_ref[...] = _dot(h_hi, rw_ref[...]) + _dot(h_lo, rw_ref[...])


def _out_proj(a1, a2, w_out_bf16, x, mod_lat, mod_ctx, norm_gain, rw_split, tile, n_tiles):
    half = w_out_bf16.shape[0] // 2
    t_out = n_tiles * tile
    row = lambda width: pl.BlockSpec((None, tile, width), lambda b, i: (b, i, 0))
    return pl.pallas_call(
        functools.partial(_out_kernel, tile=tile),
        grid=(BATCH, n_tiles),
        in_specs=[row(half), row(half),
                  pl.BlockSpec((half, D_MODEL), lambda b, i: (0, 0)),
                  pl.BlockSpec((half, D_MODEL), lambda b, i: (1, 0)),
                  row(D_MODEL), _lat_spec(2), _ctx_spec(2),
                  pl.BlockSpec((1, D_MODEL), lambda b, i: (0, 0)),
                  _lat_spec(4), _ctx_spec(4), _lat_spec(3), _ctx_spec(3),
                  pl.BlockSpec((D_MODEL, 2 * N_EXPERTS), lambda b, i: (0, 0))],
        out_specs=[row(D_MODEL), row(D_MODEL), row(2 * N_EXPERTS)],
        out_shape=[jax.ShapeDtypeStruct((BATCH, t_out, D_MODEL), F32),
                   jax.ShapeDtypeStruct((BATCH, t_out, D_MODEL), BF16),
                   jax.ShapeDtypeStruct((BATCH, t_out, 2 * N_EXPERTS), F32)],
        compiler_params=_cparams(2),
        name="out_proj",
    )(a1, a2, w_out_bf16, w_out_bf16, x, mod_lat, mod_ctx, norm_gain,
      mod_lat, mod_ctx, mod_lat, mod_ctx, rw_split)


def _moe_up_kernel(x_ref, wg_ref, wu_ref, o_ref):
    x = x_ref[...]
    a = _dot(x, wg_ref[...].astype(BF16))
    u = _dot(x, wu_ref[...].astype(BF16))
    o_ref[...] = (_silu(a) * u).astype(o_ref.dtype)


def _moe_down_kernel(h_ref, wd_ref, g_ref, o_ref):
    o_ref[...] = (_dot(h_ref[...], wd_ref[...].astype(BF16)) * g_ref[...]).astype(o_ref.dtype)


def _moe(xs, gates, w_gate, w_up, w_down, layer):
    e, m, _ = xs.shape
    tn = MOE_TN
    wspec = pl.BlockSpec((None, None, D_MODEL, tn), lambda e, f: (layer, e, 0, f))
    hmid = pl.pallas_call(
        _moe_up_kernel,
        grid=(e, D_EXPERT // tn),
        in_specs=[pl.BlockSpec((None, m, D_MODEL), lambda e, f: (e, 0, 0)), wspec, wspec],
        out_specs=pl.BlockSpec((None, m, tn), lambda e, f: (e, 0, f)),
        out_shape=jax.ShapeDtypeStruct((e, m, D_EXPERT), BF16),
        compiler_params=_cparams(2),
        name="moe_up",
    )(xs, w_gate, w_up)
    return pl.pallas_call(
        _moe_down_kernel,
        grid=(e, D_MODEL // tn),
        in_specs=[pl.BlockSpec((None, m, D_EXPERT), lambda e, f: (e, 0, 0)),
                  pl.BlockSpec((None, None, D_EXPERT, tn), lambda e, f: (layer, e, 0, f)),
                  pl.BlockSpec((None, m, 1), lambda e, f: (e, 0, 0))],
        out_specs=pl.BlockSpec((None, m, tn), lambda e, f: (e, 0, f)),
        out_shape=jax.ShapeDtypeStruct((e, m, D_MODEL), BF16),
        compiler_params=_cparams(2),
        name="moe_down",
    )(hmid, w_down, gates)


def _res_mid_kernel(x_ref, m_ref, gl_ref, gc_ref, gain_ref, scl_ref, scc_ref, shl_ref, shc_ref,
                    xo_ref, h_ref, *, tile):
    xn = x_ref[...] + _region_select(tile, gl_ref, gc_ref) * m_ref[...]
    xo_ref[...] = xn
    h_ref[...] = (_rms(xn) * gain_ref[...] * (1.0 + _region_select(tile, scl_ref, scc_ref))
                  + _region_select(tile, shl_ref, shc_ref)).astype(h_ref.dtype)


def _res_mid(x, moe_out, mod_lat, mod_ctx, next_lat, next_ctx, norm_gain_next):
    tile = MIX_TILE
    row = pl.BlockSpec((None, tile, D_MODEL), lambda b, i: (b, i, 0))
    return pl.pallas_call(
        functools.partial(_res_mid_kernel, tile=tile),
        grid=(BATCH, T_ALL // tile),
        in_specs=[row, row, _lat_spec(5), _ctx_spec(5),
                  pl.BlockSpec((1, D_MODEL), lambda b, i: (0, 0)),
                  _lat_spec(1), _ctx_spec(1), _lat_spec(0), _ctx_spec(0)],
        out_specs=[row, row],
        out_shape=[jax.ShapeDtypeStruct((BATCH, T_ALL, D_MODEL), F32),
                   jax.ShapeDtypeStruct((BATCH, T_ALL, D_MODEL), BF16)],
        compiler_params=_cparams(2),
        name="ffn_residual_prenorm",
    )(x, moe_out, mod_lat, mod_ctx, norm_gain_next, next_lat, next_ctx, next_lat, next_ctx)


def _res_final_kernel(x_ref, m_ref, g_ref, gain_ref, o_ref):
    xn = x_ref[...] + g_ref[...] * m_ref[...]
    o_ref[...] = _rms(xn) * gain_ref[...]


def _res_final(x, moe_out, mod_lat, norm_final):
    row = pl.BlockSpec((None, LAT_TILE, D_MODEL), lambda b, i: (b, i, 0))
    return pl.pallas_call(
        _res_final_kernel,
        grid=(BATCH, T_LAT // LAT_TILE),
        in_specs=[row, row, _lat_spec(5), pl.BlockSpec((1, D_MODEL), lambda b, i: (0, 0))],
        out_specs=row,
        out_shape=jax.ShapeDtypeStruct((BATCH, T_LAT, D_MODEL), F32),
        compiler_params=_cparams(2),
        name="ffn_residual_final_norm",
    )(x, moe_out, mod_lat, norm_final)


def _rope_tables():
    half = HEAD_DIM // 2
    nf = half // 2
    t = np.arange(T_LAT)
    rows, cols = t // GRID_W, t % GRID_W
    freq = ROPE_BASE ** (-jnp.arange(nf, dtype=F32) / nf)
    ang_r = jnp.asarray(rows, F32)[:, None] * freq
    ang_c = jnp.asarray(cols, F32)[:, None] * freq
    zeros = jnp.zeros_like(ang_r)
    cos = jnp.concatenate([jnp.cos(ang_r), jnp.cos(ang_r), jnp.cos(ang_c), jnp.cos(ang_c)], axis=1)
    s1 = jnp.concatenate([-jnp.sin(ang_r), zeros, -jnp.sin(ang_c), zeros], axis=1)
    s2 = jnp.concatenate([zeros, jnp.sin(ang_r), zeros, jnp.sin(ang_c)], axis=1)
    return cos, s1, s2


def _router_split(rw):
    hi = rw.astype(BF16)
    lo = (rw - hi.astype(F32)).astype(BF16)
    return jnp.concatenate([hi, lo], axis=1)


def _route(logits, t0, t1, cap):
    aff = jax.nn.softmax(logits[:, t0:t1], axis=-1)
    g, idx = lax.top_k(aff.transpose(0, 2, 1), cap)
    return g, idx + t0


def _expert_ffn(h, logits2, w_gate, w_up, w_down, layer, with_ctx):
    logits = logits2[..., :N_EXPERTS] + logits2[..., N_EXPERTS:]
    g, idx = _route(logits, 0, T_LAT, EC_CAPACITY * T_LAT // N_EXPERTS)
    if with_ctx:
        gc, idxc = _route(logits, T_LAT, T_ALL, EC_CAPACITY * T_CTX // N_EXPERTS)
        g = jnp.concatenate([g, gc], axis=-1)
        idx = jnp.concatenate([idx, idxc], axis=-1)
    cap = idx.shape[-1]
    t_rows = h.shape[1]
    flat = (idx + jnp.arange(BATCH)[:, None, None] * t_rows).transpose(1, 0, 2).reshape(N_EXPERTS, -1)
    gates = g.transpose(1, 0, 2).reshape(N_EXPERTS, BATCH * cap, 1)
    xs = h.reshape(BATCH * t_rows, D_MODEL)[flat]
    y = _moe(xs, gates, w_gate, w_up, w_down, layer)
    out = jnp.zeros((BATCH * t_rows, D_MODEL), F32).at[flat.reshape(-1)].add(
        y.reshape(-1, D_MODEL).astype(F32))
    return out.reshape(BATCH, t_rows, D_MODEL)


def kernel(x, c, ctx, c_ctx, ada_w, ada_b, norm_mix, norm_ffn, norm_final, ev_w_in, ev_w_out,
           ret_gamma_logit, ret_norm, gmlp_norm, gmlp_ws, gmlp_bs, od_w_in, od_w_out, na_rpb,
           gla_w_up, gla_b_up, gla_norm, router_w, moe_w_gate, moe_w_up, moe_w_down):
    assert x.shape == (BATCH, T_LAT, D_MODEL) and ctx.shape == (BATCH, T_CTX, D_MODEL)
    assert ada_w.shape[0] == 2 and ev_w_in.shape[0] == 1 and od_w_in.shape[0] == 1

    cond = jnp.concatenate([jax.nn.silu(c), jax.nn.silu(c_ctx)[None],
                            jnp.zeros((11, D_MODEL), F32)], axis=0).astype(BF16)
    mod_lat, mod_ctx = [], []
    for l in range(2):
        m = _matmul(cond, ada_w[l].astype(BF16), 16, 2048, F32) + ada_b[l]
        m = m[:BATCH + 1].reshape(BATCH + 1, 6, 1, D_MODEL)
        mod_lat.append(m[:BATCH])
        mod_ctx.append(m[BATCH])

    xa = jnp.concatenate([x, ctx], axis=1)
    cos, s1, s2 = _rope_tables()

    h = _modnorm(xa, norm_mix[0:1], mod_lat[0], mod_ctx[0], 1, 0)
    p = _matmul(h.reshape(BATCH * T_ALL, D_MODEL), ev_w_in[0].astype(BF16), 1024, 1024, BF16)
    p = p.reshape(BATCH, T_ALL, EVEN_COLS)
    log_g = jax.nn.log_sigmoid(ret_gamma_logit[0].astype(F32))
    ret = _retention(p, log_g, cos, s1, s2, ret_norm[0:1])
    gm = _gmlp(p, gmlp_norm[0:1], gmlp_ws[0].astype(BF16), gmlp_bs[0].T)
    xa, h2, logits = _out_proj(ret, gm, ev_w_out[0].astype(BF16), xa, mod_lat[0], mod_ctx[0],
                               norm_ffn[0:1], _router_split(router_w[0]), MIX_TILE,
                               T_ALL // MIX_TILE)
    moe_out = _expert_ffn(h2, logits, moe_w_gate, moe_w_up, moe_w_down, 0, True)
    xa, h = _res_mid(xa, moe_out, mod_lat[0], mod_ctx[0], mod_lat[1], mod_ctx[1], norm_mix[1:2])

    w_in = jnp.pad(od_w_in[0], ((0, 0), (0, ODD_COLS_PAD - ODD_COLS))).astype(BF16)
    p = _matmul(h.reshape(BATCH * T_ALL, D_MODEL), w_in, 1024, 896, BF16)
    p = p.reshape(BATCH, T_ALL, ODD_COLS_PAD)
    o_na = _na(p, _na_bias_table(na_rpb[0]))
    w_up_full = jnp.zeros((2, LANE, GLA_QK_DIM), F32)
    for d in range(2):
        w_up_full = w_up_full.at[d, d * GLA_GATE_RANK:(d + 1) * GLA_GATE_RANK].set(gla_w_up[0, d])
    w_hi = w_up_full.astype(BF16)
    w_lo = (w_up_full - w_hi.astype(F32)).astype(BF16)
    gla = _gla(p, w_hi, w_lo, gla_b_up[0][:, None, :], gla_norm[0:1])
    xl, h2, logits = _out_proj(o_na, gla, od_w_out[0].astype(BF16), xa, mod_lat[1], mod_ctx[1],
                               norm_ffn[1:2], _router_split(router_w[1]), LAT_TILE,
                               T_LAT // LAT_TILE)
    moe_out = _expert_ffn(h2, logits, moe_w_gate, moe_w_up, moe_w_down, 1, False)
    return _res_final(xl, moe_out, mod_lat[1], norm_final[None])
```

```python
import functools

import numpy as np
import jax
import jax.numpy as jnp
from jax import lax
from jax.experimental import pallas as pl
from jax.experimental.pallas import tpu as pltpu

D_MODEL = 2048
BATCH = 4
T_LAT = 4096
T_CTX = 256
T_ALL = T_LAT + T_CTX
GRID_W = 64
GRID_ROWS = T_LAT // GRID_W
HEAD_DIM = 128
RET_HEADS = 8
RET_DIM = 1024
GMLP_DIM = 1024
GMLP_GROUPS = 8
GMLP_CHUNK = 128
NA_HEADS = 8
NA_DIM = 1024
WIN_ROWS = 8
WIN_COLS = 16
GLA_HEADS = 8
GLA_DK = 64
GLA_DV = 128
GLA_QK_DIM = 512
GLA_V_DIM = 1024
GLA_GATE_RANK = 16
GLA_TAU = 16.0
N_EXPERTS = 16
D_EXPERT = 2048
EC_CAPACITY = 2
ROPE_BASE = 10000.0
RMS_EPS = 1e-6
EVEN_COLS = 6144
ODD_COLS = 6176
ODD_COLS_PAD = 6272
LANE = 128

ROW_TILE = 256
MIX_TILE = 544
LAT_TILE = 512
RET_CHUNK = 256
GLA_SUB = 64
GLA_SUPER = 256
NA_QROWS = 4
NA_QT = NA_QROWS * GRID_W
NA_KROWS = NA_QROWS + WIN_ROWS - 1
NA_KT = NA_KROWS * GRID_W
NEG_BIG = -1e30
MOE_TN = 512
PAIR_WIN = 256
VMEM_LIMIT = 56 * 1024 * 1024

F32 = jnp.float32
BF16 = jnp.bfloat16


def _cparams(n_axes):
    return pltpu.CompilerParams(dimension_semantics=("arbitrary",) * n_axes,
                                vmem_limit_bytes=VMEM_LIMIT)


def _dot(a, b):
    return jnp.dot(a, b, preferred_element_type=F32)


def _dot_nt(a, b):
    return lax.dot_general(a, b, (((1,), (1,)), ((), ())), preferred_element_type=F32)


def _dot_tn(a, b):
    return lax.dot_general(a, b, (((0,), (0,)), ((), ())), preferred_element_type=F32)


def _silu(x):
    return x * jax.nn.sigmoid(x)


def _rms(x):
    return x * lax.rsqrt(jnp.mean(x * x, axis=-1, keepdims=True) + RMS_EPS)


def _split3(x):
    hi = x.astype(BF16)
    r1 = x - hi.astype(F32)
    mid = r1.astype(BF16)
    lo = (r1 - mid.astype(F32)).astype(BF16)
    return hi, mid, lo


def _mm_kernel(a_ref, w_ref, o_ref):
    o_ref[...] = _dot(a_ref[...], w_ref[...]).astype(o_ref.dtype)


def _matmul(a, w, tm, tn, out_dtype):
    m, k = a.shape
    n = w.shape[1]
    return pl.pallas_call(
        _mm_kernel,
        grid=(n // tn, m // tm),
        in_specs=[pl.BlockSpec((tm, k), lambda j, i: (i, 0)),
                  pl.BlockSpec((k, tn), lambda j, i: (0, j))],
        out_specs=pl.BlockSpec((tm, tn), lambda j, i: (i, j)),
        out_shape=jax.ShapeDtypeStruct((m, n), out_dtype),
        compiler_params=_cparams(2),
        name="matmul",
    )(a, w)


def _lat_spec(which):
    return pl.BlockSpec((None, None, 1, D_MODEL), lambda b, i, *_: (b, which, 0, 0))


def _ctx_spec(which):
    return pl.BlockSpec((None, 1, D_MODEL), lambda b, i, *_: (which, 0, 0))


def _region_select(tile, lat_ref, ctx_ref):
    rows = pl.program_id(1) * tile + lax.broadcasted_iota(jnp.int32, (tile, 1), 0)
    return jnp.where(rows >= T_LAT, ctx_ref[...], lat_ref[...])


def _modnorm_kernel(x_ref, gain_ref, scl_ref, scc_ref, shl_ref, shc_ref, o_ref, *, tile):
    sc = _region_select(tile, scl_ref, scc_ref)
    sh = _region_select(tile, shl_ref, shc_ref)
    o_ref[...] = (_rms(x_ref[...]) * gain_ref[...] * (1.0 + sc) + sh).astype(o_ref.dtype)


def _modnorm(x, gain, mod_lat, mod_ctx, sc_i, sh_i):
    tile = MIX_TILE
    row = pl.BlockSpec((None, tile, D_MODEL), lambda b, i: (b, i, 0))
    return pl.pallas_call(
        functools.partial(_modnorm_kernel, tile=tile),
        grid=(BATCH, T_ALL // tile),
        in_specs=[row, pl.BlockSpec((1, D_MODEL), lambda b, i: (0, 0)),
                  _lat_spec(sc_i), _ctx_spec(sc_i), _lat_spec(sh_i), _ctx_spec(sh_i)],
        out_specs=row,
        out_shape=jax.ShapeDtypeStruct((BATCH, T_ALL, D_MODEL), BF16),
        compiler_params=_cparams(2),
        name="modnorm",
    )(x, gain, mod_lat, mod_ctx, mod_lat, mod_ctx)


def _rotary(z, cos, s1, s2):
    return z * cos + pltpu.roll(z, 96, 1) * s1 + pltpu.roll(z, 32, 1) * s2


def _ret_kernel(lg_ref, q_ref, g_ref, k_ref, v_ref, cos_ref, s1_ref, s2_ref, gain_ref,
                o_ref, acc_ref, qs_ref, ks_ref):
    c = RET_CHUNK
    h = pl.program_id(1)
    scale = HEAD_DIM ** -0.5
    la_f = lg_ref[0, h]
    la_b = lg_ref[1, h]
    dist = (lax.broadcasted_iota(jnp.int32, (c, c), 0)
            - lax.broadcasted_iota(jnp.int32, (c, c), 1)).astype(F32)
    tcol = lax.broadcasted_iota(jnp.int32, (c, 1), 0).astype(F32)
    dmat = (jnp.where(dist >= 0.0, jnp.exp(la_f * jnp.maximum(dist, 0.0)), 0.0)
            + jnp.where(dist <= 0.0, jnp.exp(la_b * jnp.maximum(-dist, 0.0)), 0.0))
    qdec_f = jnp.exp(la_f * (tcol + 1.0))
    kdec_f = jnp.exp(la_f * (c - 1.0 - tcol))
    cdec_f = jnp.exp(jnp.full((1, LANE), la_f * c, F32))
    qdec_b = jnp.exp(la_b * (c - tcol))
    kdec_b = jnp.exp(la_b * tcol)
    cdec_b = jnp.exp(jnp.full((1, LANE), la_b * c, F32))
    gain = gain_ref[...]

    def fwd(i, st, base, rot):
        start = pl.multiple_of(base + i * c, c)
        q = q_ref[pl.ds(start, c), :].astype(F32) * scale
        k = k_ref[pl.ds(start, c), :].astype(F32)
        v = v_ref[pl.ds(start, c), :]
        if rot:
            cos, s1, s2 = (cos_ref[pl.ds(start, c), :], s1_ref[pl.ds(start, c), :],
                           s2_ref[pl.ds(start, c), :])
            q = _rotary(q, cos, s1, s2)
            k = _rotary(k, cos, s1, s2)
        qs_ref[pl.ds(start, c), :] = q
        ks_ref[pl.ds(start, c), :] = k
        att = _dot_nt(q.astype(BF16), k.astype(BF16)) * dmat
        acc_ref[pl.ds(start, c), :] = (_dot(att.astype(BF16), v)
                                       + _dot_nt((q * qdec_f).astype(BF16), st.astype(BF16)))
        return cdec_f * st + _dot_tn(v, (k * kdec_f).astype(BF16))

    def bwd(i, st, base, n):
        start = pl.multiple_of(base + (n - 1 - i) * c, c)
        q = qs_ref[pl.ds(start, c), :]
        k = ks_ref[pl.ds(start, c), :]
        v = v_ref[pl.ds(start, c), :]
        o = acc_ref[pl.ds(start, c), :] + _dot_nt((q * qdec_b).astype(BF16), st.astype(BF16))
        gate = g_ref[pl.ds(start, c), :].astype(F32)
        o_ref[pl.ds(start, c), :] = (_rms(o) * gain * _silu(gate)).astype(o_ref.dtype)
        return cdec_b * st + _dot_tn(v, (k * kdec_b).astype(BF16))

    zero = jnp.zeros((HEAD_DIM, HEAD_DIM), F32)
    n_ctx, n_lat = T_CTX // c, T_LAT // c
    st = lax.fori_loop(0, n_ctx, functools.partial(fwd, base=T_LAT, rot=False), zero)
    lax.fori_loop(0, n_lat, functools.partial(fwd, base=0, rot=True), st, unroll=2)
    st = lax.fori_loop(0, n_ctx, functools.partial(bwd, base=T_LAT, n=n_ctx), zero)
    lax.fori_loop(0, n_lat, functools.partial(bwd, base=0, n=n_lat), st, unroll=2)


def _retention(p, log_g, cos, s1, s2, ret_norm):
    col = lambda off: pl.BlockSpec((None, T_ALL, LANE), lambda b, h, lg: (b, 0, off + h))
    tab = pl.BlockSpec((T_LAT, LANE), lambda b, h, lg: (0, 0))
    return pl.pallas_call(
        _ret_kernel,
        grid_spec=pltpu.PrefetchScalarGridSpec(
            num_scalar_prefetch=1,
            grid=(BATCH, RET_HEADS),
            in_specs=[col(0), col(8), col(32), col(40), tab, tab, tab,
                      pl.BlockSpec((1, LANE), lambda b, h, lg: (0, h))],
            out_specs=pl.BlockSpec((None, T_ALL, LANE), lambda b, h, lg: (b, 0, h)),
            scratch_shapes=[pltpu.VMEM((T_ALL, LANE), F32), pltpu.VMEM((T_ALL, LANE), F32),
                            pltpu.VMEM((T_ALL, LANE), F32)]),
        out_shape=jax.ShapeDtypeStruct((BATCH, T_ALL, RET_DIM), BF16),
        compiler_params=_cparams(2),
        name="retention",
    )(log_g, p, p, p, p, cos, s1, s2, ret_norm)


def _gmlp_kernel(u_ref, v_ref, gain_ref, ws_ref, bst_ref, o_ref):
    u = jax.nn.gelu(u_ref[...].astype(F32))
    v = _rms(jax.nn.gelu(v_ref[...].astype(F32))) * gain_ref[...]
    vb = v.astype(BF16)
    for n in range(ROW_TILE // GMLP_CHUNK):
        r0 = n * GMLP_CHUNK
        for g in range(GMLP_GROUPS):
            c0 = g * LANE
            mixed = (_dot(ws_ref[g], vb[r0:r0 + GMLP_CHUNK, c0:c0 + LANE])
                     + bst_ref[:, g:g + 1])
            o_ref[r0:r0 + GMLP_CHUNK, c0:c0 + LANE] = (
                u[r0:r0 + GMLP_CHUNK, c0:c0 + LANE] * mixed).astype(o_ref.dtype)


def _gmlp(p, gmlp_norm, ws_bf16, bs_t):
    return pl.pallas_call(
        _gmlp_kernel,
        grid=(BATCH, T_ALL // ROW_TILE),
        in_specs=[pl.BlockSpec((None, ROW_TILE, GMLP_DIM), lambda b, i: (b, i, 2)),
                  pl.BlockSpec((None, ROW_TILE, GMLP_DIM), lambda b, i: (b, i, 3)),
                  pl.BlockSpec((1, GMLP_DIM), lambda b, i: (0, 0)),
                  pl.BlockSpec((GMLP_GROUPS, GMLP_CHUNK, GMLP_CHUNK), lambda b, i: (0, 0, 0)),
                  pl.BlockSpec((GMLP_CHUNK, GMLP_GROUPS), lambda b, i: (0, 0))],
        out_specs=pl.BlockSpec((None, ROW_TILE, GMLP_DIM), lambda b, i: (b, i, 0)),
        out_shape=jax.ShapeDtypeStruct((BATCH, T_ALL, GMLP_DIM), BF16),
        compiler_params=_cparams(2),
        name="gmlp",
    )(p, p, gmlp_norm, ws_bf16, bs_t)


def _na_kernel(q_ref, k_ref, v_ref, bias_ref, o_ref):
    i = pl.program_id(2)
    w0 = jnp.clip(i * NA_QROWS - WIN_ROWS // 2, 0, GRID_ROWS - NA_KROWS)
    start = pl.multiple_of(w0 * GRID_W, GRID_W)
    scale = HEAD_DIM ** -0.5
    q = q_ref[...]
    s_win = _dot_nt(q, k_ref[pl.ds(start, NA_KT), :]) * scale + bias_ref[...]
    s_ctx = _dot_nt(q, k_ref[pl.ds(T_LAT, T_CTX), :]) * scale
    m = jnp.maximum(jnp.max(s_win, axis=-1, keepdims=True), jnp.max(s_ctx, axis=-1, keepdims=True))
    p_win = jnp.exp(s_win - m)
    p_ctx = jnp.exp(s_ctx - m)
    denom = jnp.sum(p_win, axis=-1, keepdims=True) + jnp.sum(p_ctx, axis=-1, keepdims=True)
    o = (_dot(p_win.astype(BF16), v_ref[pl.ds(start, NA_KT), :])
         + _dot(p_ctx.astype(BF16), v_ref[pl.ds(T_LAT, T_CTX), :]))
    o_ref[...] = (o / denom).astype(o_ref.dtype)


def _na_geometry():
    n_blk = GRID_ROWS // NA_QROWS
    dr_idx = np.zeros((3, NA_QROWS, NA_KROWS), np.int32)
    row_ok = np.zeros((3, NA_QROWS, NA_KROWS), bool)
    for kind, i_rep in enumerate((0, 1, n_blk - 1)):
        r0 = i_rep * NA_QROWS
        w0 = int(np.clip(r0 - WIN_ROWS // 2, 0, GRID_ROWS - NA_KROWS))
        r = r0 + np.arange(NA_QROWS)[:, None]
        kr = w0 + np.arange(NA_KROWS)[None, :]
        rstart = np.clip(r - WIN_ROWS // 2, 0, GRID_ROWS - WIN_ROWS)
        row_ok[kind] = (kr >= rstart) & (kr < rstart + WIN_ROWS)
        dr_idx[kind] = np.clip(kr - r + WIN_ROWS - 1, 0, 2 * WIN_ROWS - 2)
    c = np.arange(GRID_W)[:, None]
    kc = np.arange(GRID_W)[None, :]
    cstart = np.clip(c - WIN_COLS // 2, 0, GRID_W - WIN_COLS)
    col_ok = (kc >= cstart) & (kc < cstart + WIN_COLS)
    valid = row_ok[:, :, None, :, None] & col_ok[None, None, :, None, :]
    return dr_idx, valid.reshape(3, NA_QT, NA_KT)


def _na_bias_table(rpb):
    n_dr, n_dc = 2 * WIN_ROWS - 1, 2 * WIN_COLS - 1
    w = jnp.concatenate([rpb[..., WIN_COLS - 1:], jnp.zeros((NA_HEADS, n_dr, LANE - n_dc), F32),
                         rpb[..., :WIN_COLS - 1]], axis=-1)
    toep = jnp.tile(w, (1, 1, GRID_W))[..., :GRID_W * (LANE - 1)]
    toep = toep.reshape(NA_HEADS, n_dr, GRID_W, LANE - 1)[..., :GRID_W]
    dr_idx, valid = _na_geometry()
    tab = toep[:, dr_idx.reshape(-1)].reshape(NA_HEADS, 3, NA_QROWS, NA_KROWS, GRID_W, GRID_W)
    tab = tab.transpose(1, 0, 2, 4, 3, 5).reshape(3, NA_HEADS, NA_QT, NA_KT)
    return jnp.where(jnp.asarray(valid)[:, None], tab, NEG_BIG)


def _na(p, bias):
    n_blk = T_LAT // NA_QT
    kind = lambda i: jnp.where(i == 0, 0, jnp.where(i == n_blk - 1, 2, 1))
    return pl.pallas_call(
        _na_kernel,
        grid=(BATCH, NA_HEADS, n_blk),
        in_specs=[pl.BlockSpec((None, NA_QT, LANE), lambda b, h, i: (b, i, h)),
                  pl.BlockSpec((None, T_ALL, LANE), lambda b, h, i: (b, 0, 20 + h)),
                  pl.BlockSpec((None, T_ALL, LANE), lambda b, h, i: (b, 0, 28 + h)),
                  pl.BlockSpec((None, None, NA_QT, NA_KT), lambda b, h, i: (kind(i), h, 0, 0))],
        out_specs=pl.BlockSpec((None, NA_QT, LANE), lambda b, h, i: (b, i, h)),
        out_shape=jax.ShapeDtypeStruct((BATCH, T_LAT, NA_DIM), BF16),
        compiler_params=_cparams(3),
        name="nbr_attention",
    )(p, p, p, bias)


def _gla_kernel(q_ref, k_ref, v_ref, g_ref, lr_ref, whi_ref, wlo_ref, bup_ref, gain_ref,
                o_ref, acc_ref):
    r, c = GLA_SUPER, GLA_SUB
    n_sub = r // c
    dv2 = 2 * GLA_DV
    qscale = GLA_DK ** -0.5
    row = lax.broadcasted_iota(jnp.int32, (r, r), 0)
    col = lax.broadcasted_iota(jnp.int32, (r, r), 1)
    sub_shift = c.bit_length() - 1
    same = (row >> sub_shift) == (col >> sub_shift)
    blk_ones = jnp.where(same, 1.0, 0.0).astype(BF16)
    second_head_lane = lax.broadcasted_iota(jnp.int32, (1, LANE), 1) >= GLA_DK
    st_mask = ((lax.broadcasted_iota(jnp.int32, (dv2, LANE), 0) >= GLA_DV)
               == (lax.broadcasted_iota(jnp.int32, (dv2, LANE), 1) >= GLA_DK))
    gain = gain_ref[...]

    for d in range(2):
        tri = same & ((row >= col) if d == 0 else (col >= row))
        tri_b = jnp.where(tri, 1.0, 0.0).astype(BF16)

        def step(i, st, base, n, d=d, tri=tri, tri_b=tri_b):
            idx = i if d == 0 else n - 1 - i
            start = pl.multiple_of(base + idx * r, r)
            lr = lr_ref[pl.ds(start, r), :]
            z = _dot(lr, whi_ref[d]) + _dot(lr, wlo_ref[d]) + bup_ref[d]
            la = (jnp.minimum(z, 0.0) - jnp.log(1.0 + jnp.exp(-jnp.abs(z)))) * (1.0 / GLA_TAU)
            hi, mid, lo = _split3(la)
            b = _dot(tri_b, hi) + _dot(tri_b, mid) + _dot(tri_b, lo)
            bt = _dot(blk_ones, hi) + _dot(blk_ones, mid) + _dot(blk_ones, lo)
            q = q_ref[pl.ds(start, r), :].astype(F32) * qscale
            k = k_ref[pl.ds(start, r), :].astype(F32)
            v = v_ref[pl.ds(start, r), :]
            qi = q * jnp.exp(b)
            qib = qi.astype(BF16)
            ki = (k * jnp.exp(-b)).astype(BF16)
            ko = (k * jnp.exp(bt - b)).astype(BF16)
            dec = jnp.exp(bt)
            outs = []
            for j in range(2):
                qm = jnp.where(second_head_lane == (j == 1), qi, 0.0).astype(BF16)
                att = jnp.where(tri, _dot_nt(qm, ki), 0.0)
                outs.append(_dot(att.astype(BF16), v[:, j * GLA_DV:(j + 1) * GLA_DV]))
            o_intra = jnp.concatenate(outs, axis=1)
            parts = [None] * n_sub
            for s in (range(n_sub) if d == 0 else reversed(range(n_sub))):
                lo_r, hi_r = s * c, (s + 1) * c
                parts[s] = o_intra[lo_r:hi_r] + _dot_nt(qib[lo_r:hi_r], st.astype(BF16))
                st = (st * dec[lo_r:lo_r + 1, :]
                      + jnp.where(st_mask, _dot_tn(v[lo_r:hi_r], ko[lo_r:hi_r]), 0.0))
            o = jnp.concatenate(parts, axis=0)
            if d == 0:
                acc_ref[pl.ds(start, r), :] = o
            else:
                o = acc_ref[pl.ds(start, r), :] + o
                gate = g_ref[pl.ds(start, r), :].astype(F32)
                on = jnp.concatenate([_rms(o[:, :GLA_DV]), _rms(o[:, GLA_DV:])], axis=1) * gain
                o_ref[pl.ds(start, r), :] = (on * _silu(gate)).astype(o_ref.dtype)
            return st

        st = jnp.zeros((dv2, LANE), F32)
        st = lax.fori_loop(0, T_CTX // r, functools.partial(step, base=T_LAT, n=T_CTX // r), st)
        lax.fori_loop(0, T_LAT // r, functools.partial(step, base=0, n=T_LAT // r), st)


def _gla(p, w_hi, w_lo, b_up, gla_norm):
    dv2 = 2 * GLA_DV
    return pl.pallas_call(
        _gla_kernel,
        grid=(BATCH, GLA_HEADS // 2),
        in_specs=[pl.BlockSpec((None, T_ALL, LANE), lambda b, j: (b, 0, 8 + j)),
                  pl.BlockSpec((None, T_ALL, LANE), lambda b, j: (b, 0, 36 + j)),
                  pl.BlockSpec((None, T_ALL, dv2), lambda b, j: (b, 0, 20 + j)),
                  pl.BlockSpec((None, T_ALL, dv2), lambda b, j: (b, 0, 6 + j)),
                  pl.BlockSpec((None, T_ALL, LANE), lambda b, j: (b, 0, 48)),
                  pl.BlockSpec((2, LANE, LANE), lambda b, j: (0, 0, j)),
                  pl.BlockSpec((2, LANE, LANE), lambda b, j: (0, 0, j)),
                  pl.BlockSpec((2, 1, LANE), lambda b, j: (0, 0, j)),
                  pl.BlockSpec((1, dv2), lambda b, j: (0, j))],
        out_specs=pl.BlockSpec((None, T_ALL, dv2), lambda b, j: (b, 0, j)),
        out_shape=jax.ShapeDtypeStruct((BATCH, T_ALL, GLA_V_DIM), BF16),
        scratch_shapes=[pltpu.VMEM((T_ALL, dv2), F32)],
        compiler_params=_cparams(2),
        name="gla",
    )(p, p, p, p, p, w_hi, w_lo, b_up, gla_norm)


def _out_kernel(a1_ref, a2_ref, w1_ref, w2_ref, x_ref, gl_ref, gc_ref, gain_ref,
                scl_ref, scc_ref, shl_ref, shc_ref, rw_ref, xo_ref, h_ref, lg_ref, *, tile):
    y = _dot(a1_ref[...], w1_ref[...]) + _dot(a2_ref[...], w2_ref[...])
    xn = x_ref[...] + _region_select(tile, gl_ref, gc_ref) * y
    xo_ref[...] = xn
    h = (_rms(xn) * gain_ref[...] * (1.0 + _region_select(tile, scl_ref, scc_ref))
         + _region_select(tile, shl_ref, shc_ref))
    h_hi = h.astype(BF16)
    h_ref[...] = h_hi
    h_lo = (h - h_hi.astype(F32)).astype(BF16)
    lg_ref[...] = _dot(h_hi, rw_ref[...]) + _dot(h_lo, rw_ref[...])


def _out_proj(a1, a2, w_out_bf16, x, mod_lat, mod_ctx, norm_gain, rw_split, tile, n_tiles):
    half = w_out_bf16.shape[0] // 2
    t_out = n_tiles * tile
    row = lambda width: pl.BlockSpec((None, tile, width), lambda b, i: (b, i, 0))
    return pl.pallas_call(
        functools.partial(_out_kernel, tile=tile),
        grid=(BATCH, n_tiles),
        in_specs=[row(half), row(half),
                  pl.BlockSpec((half, D_MODEL), lambda b, i: (0, 0)),
                  pl.BlockSpec((half, D_MODEL), lambda b, i: (1, 0)),
                  row(D_MODEL), _lat_spec(2), _ctx_spec(2),
                  pl.BlockSpec((1, D_MODEL), lambda b, i: (0, 0)),
                  _lat_spec(4), _ctx_spec(4), _lat_spec(3), _ctx_spec(3),
                  pl.BlockSpec((D_MODEL, 2 * N_EXPERTS), lambda b, i: (0, 0))],
        out_specs=[row(D_MODEL), row(D_MODEL), row(2 * N_EXPERTS)],
        out_shape=[jax.ShapeDtypeStruct((BATCH, t_out, D_MODEL), F32),
                   jax.ShapeDtypeStruct((BATCH, t_out, D_MODEL), BF16),
                   jax.ShapeDtypeStruct((BATCH, t_out, 2 * N_EXPERTS), F32)],
        compiler_params=_cparams(2),
        name="out_proj",
    )(a1, a2, w_out_bf16, w_out_bf16, x, mod_lat, mod_ctx, norm_gain,
      mod_lat, mod_ctx, mod_lat, mod_ctx, rw_split)


def _moe_up_kernel(x_ref, wg_ref, wu_ref, o_ref):
    x = x_ref[...]
    a = _dot(x, wg_ref[...].astype(BF16))
    u = _dot(x, wu_ref[...].astype(BF16))
    o_ref[...] = (_silu(a) * u).astype(o_ref.dtype)


def _moe_down_kernel(h_ref, wd_ref, g_ref, o_ref):
    o_ref[...] = (_dot(h_ref[...], wd_ref[...].astype(BF16)) * g_ref[...]).astype(o_ref.dtype)


def _moe(xs, gates, w_gate, w_up, w_down, layer):
    e, m, _ = xs.shape
    tn = MOE_TN
    wspec = pl.BlockSpec((None, None, D_MODEL, tn), lambda e, f: (layer, e, 0, f))
    hmid = pl.pallas_call(
        _moe_up_kernel,
        grid=(e, D_EXPERT // tn),
        in_specs=[pl.BlockSpec((None, m, D_MODEL), lambda e, f: (e, 0, 0)), wspec, wspec],
        out_specs=pl.BlockSpec((None, m, tn), lambda e, f: (e, 0, f)),
        out_shape=jax.ShapeDtypeStruct((e, m, D_EXPERT), BF16),
        compiler_params=_cparams(2),
        name="moe_up",
    )(xs, w_gate, w_up)
    return pl.pallas_call(
        _moe_down_kernel,
        grid=(e, D_MODEL // tn),
        in_specs=[pl.BlockSpec((None, m, D_EXPERT), lambda e, f: (e, 0, 0)),
                  pl.BlockSpec((None, None, D_EXPERT, tn), lambda e, f: (layer, e, 0, f)),
                  pl.BlockSpec((None, m, 1), lambda e, f: (e, 0, 0))],
        out_specs=pl.BlockSpec((None, m, tn), lambda e, f: (e, 0, f)),
        out_shape=jax.ShapeDtypeStruct((e, m, D_MODEL), BF16),
        compiler_params=_cparams(2),
        name="moe_down",
    )(hmid, w_down, gates)


def _combine_moe(span_ref, pair_tok_ref, ys_hbm, acc_ref, buf_ref, sem, *, tile, n_pairs):
    b, i = pl.program_id(0), pl.program_id(1)
    n_tiles = pl.num_programs(1)
    lo = span_ref[(b * n_tiles + i) * 2]
    hi = span_ref[(b * n_tiles + i) * 2 + 1]
    w = PAIR_WIN
    shift = w.bit_length() - 1
    j0 = lo >> shift
    n_win = jnp.where(hi > lo, ((hi - 1) >> shift) - j0 + 1, 0)

    def window_copy(j, slot):
        return pltpu.make_async_copy(ys_hbm.at[pl.ds(b * n_pairs + j * w, w), :],
                                     buf_ref.at[slot], sem.at[slot])

    acc_ref[...] = jnp.zeros_like(acc_ref)

    @pl.when(n_win > 0)
    def _():
        window_copy(j0, 0).start()

    tok = i * tile + lax.broadcasted_iota(jnp.int32, (tile, 1), 0)

    def body(t, carry):
        j = j0 + t
        slot = t & 1

        @pl.when(t + 1 < n_win)
        def _():
            window_copy(j + 1, 1 - slot).start()

        window_copy(j, slot).wait()
        sel = jnp.where(tok == pair_tok_ref[pl.ds(j, 1), :], 1.0, 0.0).astype(BF16)
        acc_ref[...] += _dot(sel, buf_ref[slot])
        return carry

    lax.fori_loop(0, n_win, body, 0)
    return acc_ref[...]


def _res_mid_kernel(span_ref, x_ref, pair_tok_ref, ys_hbm, gl_ref, gc_ref, gain_ref,
                    scl_ref, scc_ref, shl_ref, shc_ref, xo_ref, h_ref, acc_ref, buf_ref, sem,
                    *, tile, n_pairs):
    moe = _combine_moe(span_ref, pair_tok_ref, ys_hbm, acc_ref, buf_ref, sem,
                       tile=tile, n_pairs=n_pairs)
    xn = x_ref[...] + _region_select(tile, gl_ref, gc_ref) * moe
    xo_ref[...] = xn
    h_ref[...] = (_rms(xn) * gain_ref[...] * (1.0 + _region_select(tile, scl_ref, scc_ref))
                  + _region_select(tile, shl_ref, shc_ref)).astype(h_ref.dtype)


def _res_final_kernel(span_ref, x_ref, pair_tok_ref, ys_hbm, g_ref, gain_ref, o_ref,
                      acc_ref, buf_ref, sem, *, tile, n_pairs):
    moe = _combine_moe(span_ref, pair_tok_ref, ys_hbm, acc_ref, buf_ref, sem,
                       tile=tile, n_pairs=n_pairs)
    o_ref[...] = _rms(x_ref[...] + g_ref[...] * moe) * gain_ref[...]


def _combine_call(kernel_fn, name, tile, t_rows, spans, x, pair_tok, ys, params, param_specs,
                  out_dtypes):
    n_pairs = pair_tok.shape[1] * PAIR_WIN
    row = pl.BlockSpec((None, tile, D_MODEL), lambda b, i, *_: (b, i, 0))
    return pl.pallas_call(
        functools.partial(kernel_fn, tile=tile, n_pairs=n_pairs),
        grid_spec=pltpu.PrefetchScalarGridSpec(
            num_scalar_prefetch=1,
            grid=(BATCH, t_rows // tile),
            in_specs=[row,
                      pl.BlockSpec((None,) + pair_tok.shape[1:], lambda b, i, *_: (b, 0, 0)),
                      pl.BlockSpec(memory_space=pl.ANY)] + param_specs,
            out_specs=[row] * len(out_dtypes),
            scratch_shapes=[pltpu.VMEM((tile, D_MODEL), F32),
                            pltpu.VMEM((2, PAIR_WIN, D_MODEL), BF16),
                            pltpu.SemaphoreType.DMA((2,))]),
        out_shape=[jax.ShapeDtypeStruct((BATCH, t_rows, D_MODEL), dt) for dt in out_dtypes],
        compiler_params=_cparams(2),
        name=name,
    )(spans, x, pair_tok, ys, *params)


def _res_mid(x, spans, pair_tok, ys, mod_lat, mod_ctx, next_lat, next_ctx, norm_gain_next):
    specs = [_lat_spec(5), _ctx_spec(5), pl.BlockSpec((1, D_MODEL), lambda b, i, *_: (0, 0)),
             _lat_spec(1), _ctx_spec(1), _lat_spec(0), _ctx_spec(0)]
    params = (mod_lat, mod_ctx, norm_gain_next, next_lat, next_ctx, next_lat, next_ctx)
    return _combine_call(_res_mid_kernel, "ffn_combine_residual_prenorm", MIX_TILE, T_ALL, spans,
                         x, pair_tok, ys, params, specs, (F32, BF16))


def _res_final(x, spans, pair_tok, ys, mod_lat, norm_final):
    specs = [_lat_spec(5), pl.BlockSpec((1, D_MODEL), lambda b, i, *_: (0, 0))]
    return _combine_call(_res_final_kernel, "ffn_combine_residual_final_norm", LAT_TILE, T_LAT,
                         spans, x, pair_tok, ys, (mod_lat, norm_final), specs, (F32,))[0]


def _rope_tables():
    half = HEAD_DIM // 2
    nf = half // 2
    t = np.arange(T_LAT)
    rows, cols = t // GRID_W, t % GRID_W
    freq = ROPE_BASE ** (-jnp.arange(nf, dtype=F32) / nf)
    ang_r = jnp.asarray(rows, F32)[:, None] * freq
    ang_c = jnp.asarray(cols, F32)[:, None] * freq
    zeros = jnp.zeros_like(ang_r)
    cos = jnp.concatenate([jnp.cos(ang_r), jnp.cos(ang_r), jnp.cos(ang_c), jnp.cos(ang_c)], axis=1)
    s1 = jnp.concatenate([-jnp.sin(ang_r), zeros, -jnp.sin(ang_c), zeros], axis=1)
    s2 = jnp.concatenate([zeros, jnp.sin(ang_r), zeros, jnp.sin(ang_c)], axis=1)
    return cos, s1, s2


def _router_split(rw):
    hi = rw.astype(BF16)
    lo = (rw - hi.astype(F32)).astype(BF16)
    return jnp.concatenate([hi, lo], axis=1)


def _route(logits, t0, t1, cap):
    aff = jax.nn.softmax(logits[:, t0:t1], axis=-1)
    g, idx = lax.top_k(aff.transpose(0, 2, 1), cap)
    return g, idx + t0


def _expert_ffn(h, logits2, w_gate, w_up, w_down, layer, with_ctx, tile):
    logits = logits2[..., :N_EXPERTS] + logits2[..., N_EXPERTS:]
    g, idx = _route(logits, 0, T_LAT, EC_CAPACITY * T_LAT // N_EXPERTS)
    if with_ctx:
        gc, idxc = _route(logits, T_LAT, T_ALL, EC_CAPACITY * T_CTX // N_EXPERTS)
        g = jnp.concatenate([g, gc], axis=-1)
        idx = jnp.concatenate([idx, idxc], axis=-1)
    cap = idx.shape[-1]
    t_rows = h.shape[1]
    n_pairs = N_EXPERTS * cap
    assert n_pairs % PAIR_WIN == 0
    flat = (idx + jnp.arange(BATCH)[:, None, None] * t_rows).transpose(1, 0, 2).reshape(N_EXPERTS, -1)
    gates = g.transpose(1, 0, 2).reshape(N_EXPERTS, BATCH * cap, 1)
    xs = h.reshape(BATCH * t_rows, D_MODEL)[flat]
    y = _moe(xs, gates, w_gate, w_up, w_down, layer)
    src = (jnp.arange(N_EXPERTS)[None, :, None] * (BATCH * cap)
           + jnp.arange(BATCH)[:, None, None] * cap + jnp.arange(cap)[None, None, :])
    pair_tok, src = lax.sort((idx.reshape(BATCH, n_pairs), src.reshape(BATCH, n_pairs)),
                             dimension=1, num_keys=1)
    ys = y.reshape(-1, D_MODEL)[src.reshape(-1)]
    bounds = jnp.arange(0, t_rows + tile, tile)
    below = jnp.sum(pair_tok[:, :, None] < bounds[None, None, :], axis=1)
    spans = jnp.stack([below[:, :-1], below[:, 1:]], axis=-1).reshape(-1).astype(jnp.int32)
    return spans, pair_tok.reshape(BATCH, n_pairs // PAIR_WIN, PAIR_WIN), ys


def kernel(x, c, ctx, c_ctx, ada_w, ada_b, norm_mix, norm_ffn, norm_final, ev_w_in, ev_w_out,
           ret_gamma_logit, ret_norm, gmlp_norm, gmlp_ws, gmlp_bs, od_w_in, od_w_out, na_rpb,
           gla_w_up, gla_b_up, gla_norm, router_w, moe_w_gate, moe_w_up, moe_w_down):
    assert x.shape == (BATCH, T_LAT, D_MODEL) and ctx.shape == (BATCH, T_CTX, D_MODEL)
    assert ada_w.shape[0] == 2 and ev_w_in.shape[0] == 1 and od_w_in.shape[0] == 1

    cond = jnp.concatenate([jax.nn.silu(c), jax.nn.silu(c_ctx)[None],
                            jnp.zeros((11, D_MODEL), F32)], axis=0).astype(BF16)
    mod_lat, mod_ctx = [], []
    for l in range(2):
        m = _matmul(cond, ada_w[l].astype(BF16), 16, 2048, F32) + ada_b[l]
        m = m[:BATCH + 1].reshape(BATCH + 1, 6, 1, D_MODEL)
        mod_lat.append(m[:BATCH])
        mod_ctx.append(m[BATCH])

    xa = jnp.concatenate([x, ctx], axis=1)
    cos, s1, s2 = _rope_tables()

    h = _modnorm(xa, norm_mix[0:1], mod_lat[0], mod_ctx[0], 1, 0)
    p = _matmul(h.reshape(BATCH * T_ALL, D_MODEL), ev_w_in[0].astype(BF16), 1024, 1024, BF16)
    p = p.reshape(BATCH, T_ALL, EVEN_COLS)
    log_g = jax.nn.log_sigmoid(ret_gamma_logit[0].astype(F32))
    ret = _retention(p, log_g, cos, s1, s2, ret_norm[0:1])
    gm = _gmlp(p, gmlp_norm[0:1], gmlp_ws[0].astype(BF16), gmlp_bs[0].T)
    xa, h2, logits = _out_proj(ret, gm, ev_w_out[0].astype(BF16), xa, mod_lat[0], mod_ctx[0],
                               norm_ffn[0:1], _router_split(router_w[0]), MIX_TILE,
                               T_ALL // MIX_TILE)
    spans, pair_tok, ys = _expert_ffn(h2, logits, moe_w_gate, moe_w_up, moe_w_down, 0, True,
                                      MIX_TILE)
    xa, h = _res_mid(xa, spans, pair_tok, ys, mod_lat[0], mod_ctx[0], mod_lat[1], mod_ctx[1],
                     norm_mix[1:2])

    w_in = jnp.pad(od_w_in[0], ((0, 0), (0, ODD_COLS_PAD - ODD_COLS))).astype(BF16)
    p = _matmul(h.reshape(BATCH * T_ALL, D_MODEL), w_in, 1024, 896, BF16)
    p = p.reshape(BATCH, T_ALL, ODD_COLS_PAD)
    o_na = _na(p, _na_bias_table(na_rpb[0]))
    w_up_full = jnp.zeros((2, LANE, GLA_QK_DIM), F32)
    for d in range(2):
        w_up_full = w_up_full.at[d, d * GLA_GATE_RANK:(d + 1) * GLA_GATE_RANK].set(gla_w_up[0, d])
    w_hi = w_up_full.astype(BF16)
    w_lo = (w_up_full - w_hi.astype(F32)).astype(BF16)
    gla = _gla(p, w_hi, w_lo, gla_b_up[0][:, None, :], gla_norm[0:1])
    xl, h2, logits = _out_proj(o_na, gla, od_w_out[0].astype(BF16), xa, mod_lat[1], mod_ctx[1],
                               norm_ffn[1:2], _router_split(router_w[1]), LAT_TILE,
                               T_LAT // LAT_TILE)
    spans, pair_tok, ys = _expert_ffn(h2, logits, moe_w_gate, moe_w_up, moe_w_down, 1, False,
                                      LAT_TILE)
    return _res_final(xl, spans, pair_tok, ys, mod_lat[1], norm_final[None])
```

```python
import functools

import numpy as np
import jax
import jax.numpy as jnp
from jax import lax
from jax.experimental import pallas as pl
from jax.experimental.pallas import tpu as pltpu

D_MODEL = 2048
BATCH = 4
T_LAT = 4096
T_CTX = 256
T_ALL = T_LAT + T_CTX
GRID_W = 64
GRID_ROWS = T_LAT // GRID_W
HEAD_DIM = 128
RET_HEADS = 8
RET_DIM = 1024
GMLP_DIM = 1024
GMLP_GROUPS = 8
GMLP_CHUNK = 128
NA_HEADS = 8
NA_DIM = 1024
WIN_ROWS = 8
WIN_COLS = 16
GLA_HEADS = 8
GLA_DK = 64
GLA_DV = 128
GLA_QK_DIM = 512
GLA_V_DIM = 1024
GLA_GATE_RANK = 16
GLA_TAU = 16.0
N_EXPERTS = 16
D_EXPERT = 2048
EC_CAPACITY = 2
ROPE_BASE = 10000.0
RMS_EPS = 1e-6
EVEN_COLS = 6144
ODD_COLS = 6176
ODD_COLS_PAD = 6272
LANE = 128

ROW_TILE = 256
MIX_TILE = 544
LAT_TILE = 512
RET_CHUNK = 256
GLA_SUB = 64
GLA_SUPER = 256
NA_QROWS = 4
NA_QT = NA_QROWS * GRID_W
NA_KROWS = NA_QROWS + WIN_ROWS - 1
NA_KT = NA_KROWS * GRID_W
NEG_BIG = -1e30
MOE_TN = 512
PAIR_WIN = 256
PAIR_CHUNK = 3
COMB_TILE = 256
VMEM_LIMIT = 56 * 1024 * 1024

F32 = jnp.float32
BF16 = jnp.bfloat16


def _cparams(n_axes):
    return pltpu.CompilerParams(dimension_semantics=("arbitrary",) * n_axes,
                                vmem_limit_bytes=VMEM_LIMIT)


def _dot(a, b):
    return jnp.dot(a, b, preferred_element_type=F32)


def _dot_nt(a, b):
    return lax.dot_general(a, b, (((1,), (1,)), ((), ())), preferred_element_type=F32)


def _dot_tn(a, b):
    return lax.dot_general(a, b, (((0,), (0,)), ((), ())), preferred_element_type=F32)


def _silu(x):
    return x * jax.nn.sigmoid(x)


def _rms(x):
    return x * lax.rsqrt(jnp.mean(x * x, axis=-1, keepdims=True) + RMS_EPS)


def _split3(x):
    hi = x.astype(BF16)
    r1 = x - hi.astype(F32)
    mid = r1.astype(BF16)
    lo = (r1 - mid.astype(F32)).astype(BF16)
    return hi, mid, lo


def _mm_kernel(a_ref, w_ref, o_ref):
    o_ref[...] = _dot(a_ref[...], w_ref[...]).astype(o_ref.dtype)


def _matmul(a, w, tm, tn, out_dtype):
    m, k = a.shape
    n = w.shape[1]
    return pl.pallas_call(
        _mm_kernel,
        grid=(n // tn, m // tm),
        in_specs=[pl.BlockSpec((tm, k), lambda j, i: (i, 0)),
                  pl.BlockSpec((k, tn), lambda j, i: (0, j))],
        out_specs=pl.BlockSpec((tm, tn), lambda j, i: (i, j)),
        out_shape=jax.ShapeDtypeStruct((m, n), out_dtype),
        compiler_params=_cparams(2),
        name="matmul",
    )(a, w)


def _mod_mm_kernel(a_ref, w_ref, o_ref):
    o_ref[...] = _dot(a_ref[...], w_ref[...].astype(BF16))


def _mod_matmul(cond, ada_w, layer):
    m, k = cond.shape
    n = ada_w.shape[2]
    tn = 1024
    return pl.pallas_call(
        _mod_mm_kernel,
        grid=(n // tn,),
        in_specs=[pl.BlockSpec((m, k), lambda j: (0, 0)),
                  pl.BlockSpec((None, k, tn), lambda j: (layer, 0, j))],
        out_specs=pl.BlockSpec((m, tn), lambda j: (0, j)),
        out_shape=jax.ShapeDtypeStruct((m, n), F32),
        compiler_params=_cparams(1),
        name="mod_matmul",
    )(cond, ada_w)


def _lat_spec(which):
    return pl.BlockSpec((None, None, 1, D_MODEL), lambda b, i, *_: (b, which, 0, 0))


def _ctx_spec(which):
    return pl.BlockSpec((None, 1, D_MODEL), lambda b, i, *_: (which, 0, 0))


def _region_select(tile, lat_ref, ctx_ref, row0=0, n_rows=None):
    n_rows = tile if n_rows is None else n_rows
    rows = pl.program_id(1) * tile + row0 + lax.broadcasted_iota(jnp.int32, (n_rows, 1), 0)
    return jnp.where(rows >= T_LAT, ctx_ref[...], lat_ref[...])


def _modnorm_kernel(x_ref, gain_ref, scl_ref, scc_ref, shl_ref, shc_ref, o_ref, *, tile):
    sc = _region_select(tile, scl_ref, scc_ref)
    sh = _region_select(tile, shl_ref, shc_ref)
    o_ref[...] = (_rms(x_ref[...]) * gain_ref[...] * (1.0 + sc) + sh).astype(o_ref.dtype)


def _modnorm(x, gain, mod_lat, mod_ctx, sc_i, sh_i):
    tile = MIX_TILE
    row = pl.BlockSpec((None, tile, D_MODEL), lambda b, i: (b, i, 0))
    return pl.pallas_call(
        functools.partial(_modnorm_kernel, tile=tile),
        grid=(BATCH, T_ALL // tile),
        in_specs=[row, pl.BlockSpec((1, D_MODEL), lambda b, i: (0, 0)),
                  _lat_spec(sc_i), _ctx_spec(sc_i), _lat_spec(sh_i), _ctx_spec(sh_i)],
        out_specs=row,
        out_shape=jax.ShapeDtypeStruct((BATCH, T_ALL, D_MODEL), BF16),
        compiler_params=_cparams(2),
        name="modnorm",
    )(x, gain, mod_lat, mod_ctx, mod_lat, mod_ctx)


def _rotary(z, cos, s1, s2):
    return z * cos + pltpu.roll(z, 96, 1) * s1 + pltpu.roll(z, 32, 1) * s2


def _ret_kernel(lg_ref, q_ref, g_ref, k_ref, v_ref, cos_ref, s1_ref, s2_ref, gain_ref,
                o_ref, acc_ref, qs_ref, ks_ref):
    c = RET_CHUNK
    h = pl.program_id(1)
    scale = HEAD_DIM ** -0.5
    la_f = lg_ref[0, h]
    la_b = lg_ref[1, h]
    dist = (lax.broadcasted_iota(jnp.int32, (c, c), 0)
            - lax.broadcasted_iota(jnp.int32, (c, c), 1)).astype(F32)
    tcol = lax.broadcasted_iota(jnp.int32, (c, 1), 0).astype(F32)
    dmat = (jnp.where(dist >= 0.0, jnp.exp(la_f * jnp.maximum(dist, 0.0)), 0.0)
            + jnp.where(dist <= 0.0, jnp.exp(la_b * jnp.maximum(-dist, 0.0)), 0.0))
    qdec_f = jnp.exp(la_f * (tcol + 1.0))
    kdec_f = jnp.exp(la_f * (c - 1.0 - tcol))
    cdec_f = jnp.exp(jnp.full((1, LANE), la_f * c, F32))
    qdec_b = jnp.exp(la_b * (c - tcol))
    kdec_b = jnp.exp(la_b * tcol)
    cdec_b = jnp.exp(jnp.full((1, LANE), la_b * c, F32))
    gain = gain_ref[...]

    def fwd(i, st, base, rot):
        start = pl.multiple_of(base + i * c, c)
        q = q_ref[pl.ds(start, c), :].astype(F32) * scale
        k = k_ref[pl.ds(start, c), :].astype(F32)
        v = v_ref[pl.ds(start, c), :]
        if rot:
            cos, s1, s2 = (cos_ref[pl.ds(start, c), :], s1_ref[pl.ds(start, c), :],
                           s2_ref[pl.ds(start, c), :])
            q = _rotary(q, cos, s1, s2)
            k = _rotary(k, cos, s1, s2)
        qs_ref[pl.ds(start, c), :] = q
        ks_ref[pl.ds(start, c), :] = k
        att = _dot_nt(q.astype(BF16), k.astype(BF16)) * dmat
        acc_ref[pl.ds(start, c), :] = (_dot(att.astype(BF16), v)
                                       + _dot_nt((q * qdec_f).astype(BF16), st.astype(BF16)))
        return cdec_f * st + _dot_tn(v, (k * kdec_f).astype(BF16))

    def bwd(i, st, base, n):
        start = pl.multiple_of(base + (n - 1 - i) * c, c)
        q = qs_ref[pl.ds(start, c), :]
        k = ks_ref[pl.ds(start, c), :]
        v = v_ref[pl.ds(start, c), :]
        o = acc_ref[pl.ds(start, c), :] + _dot_nt((q * qdec_b).astype(BF16), st.astype(BF16))
        gate = g_ref[pl.ds(start, c), :].astype(F32)
        o_ref[pl.ds(start, c), :] = (_rms(o) * gain * _silu(gate)).astype(o_ref.dtype)
        return cdec_b * st + _dot_tn(v, (k * kdec_b).astype(BF16))

    zero = jnp.zeros((HEAD_DIM, HEAD_DIM), F32)
    n_ctx, n_lat = T_CTX // c, T_LAT // c
    st = lax.fori_loop(0, n_ctx, functools.partial(fwd, base=T_LAT, rot=False), zero)
    lax.fori_loop(0, n_lat, functools.partial(fwd, base=0, rot=True), st, unroll=2)
    st = lax.fori_loop(0, n_ctx, functools.partial(bwd, base=T_LAT, n=n_ctx), zero)
    lax.fori_loop(0, n_lat, functools.partial(bwd, base=0, n=n_lat), st, unroll=2)


def _retention(p, log_g, cos, s1, s2, ret_norm):
    col = lambda off: pl.BlockSpec((None, T_ALL, LANE), lambda b, h, lg: (b, 0, off + h))
    tab = pl.BlockSpec((T_LAT, LANE), lambda b, h, lg: (0, 0))
    return pl.pallas_call(
        _ret_kernel,
        grid_spec=pltpu.PrefetchScalarGridSpec(
            num_scalar_prefetch=1,
            grid=(BATCH, RET_HEADS),
            in_specs=[col(0), col(8), col(32), col(40), tab, tab, tab,
                      pl.BlockSpec((1, LANE), lambda b, h, lg: (0, h))],
            out_specs=pl.BlockSpec((None, T_ALL, LANE), lambda b, h, lg: (b, 0, h)),
            scratch_shapes=[pltpu.VMEM((T_ALL, LANE), F32), pltpu.VMEM((T_ALL, LANE), F32),
                            pltpu.VMEM((T_ALL, LANE), F32)]),
        out_shape=jax.ShapeDtypeStruct((BATCH, T_ALL, RET_DIM), BF16),
        compiler_params=_cparams(2),
        name="retention",
    )(log_g, p, p, p, p, cos, s1, s2, ret_norm)


def _gmlp_kernel(u_ref, v_ref, gain_ref, ws_ref, bst_ref, o_ref):
    u = jax.nn.gelu(u_ref[...].astype(F32))
    v = _rms(jax.nn.gelu(v_ref[...].astype(F32))) * gain_ref[...]
    vb = v.astype(BF16)
    for n in range(ROW_TILE // GMLP_CHUNK):
        r0 = n * GMLP_CHUNK
        for g in range(GMLP_GROUPS):
            c0 = g * LANE
            mixed = (_dot(ws_ref[g], vb[r0:r0 + GMLP_CHUNK, c0:c0 + LANE])
                     + bst_ref[:, g:g + 1])
            o_ref[r0:r0 + GMLP_CHUNK, c0:c0 + LANE] = (
                u[r0:r0 + GMLP_CHUNK, c0:c0 + LANE] * mixed).astype(o_ref.dtype)


def _gmlp(p, gmlp_norm, ws_bf16, bs_t):
    return pl.pallas_call(
        _gmlp_kernel,
        grid=(BATCH, T_ALL // ROW_TILE),
        in_specs=[pl.BlockSpec((None, ROW_TILE, GMLP_DIM), lambda b, i: (b, i, 2)),
                  pl.BlockSpec((None, ROW_TILE, GMLP_DIM), lambda b, i: (b, i, 3)),
                  pl.BlockSpec((1, GMLP_DIM), lambda b, i: (0, 0)),
                  pl.BlockSpec((GMLP_GROUPS, GMLP_CHUNK, GMLP_CHUNK), lambda b, i: (0, 0, 0)),
                  pl.BlockSpec((GMLP_CHUNK, GMLP_GROUPS), lambda b, i: (0, 0))],
        out_specs=pl.BlockSpec((None, ROW_TILE, GMLP_DIM), lambda b, i: (b, i, 0)),
        out_shape=jax.ShapeDtypeStruct((BATCH, T_ALL, GMLP_DIM), BF16),
        compiler_params=_cparams(2),
        name="gmlp",
    )(p, p, gmlp_norm, ws_bf16, bs_t)


def _na_kernel(q_ref, k_ref, v_ref, bias_ref, o_ref):
    i = pl.program_id(2)
    w0 = jnp.clip(i * NA_QROWS - WIN_ROWS // 2, 0, GRID_ROWS - NA_KROWS)
    start = pl.multiple_of(w0 * GRID_W, GRID_W)
    scale = HEAD_DIM ** -0.5
    q = q_ref[...]
    s_win = _dot_nt(q, k_ref[pl.ds(start, NA_KT), :]) * scale + bias_ref[...]
    s_ctx = _dot_nt(q, k_ref[pl.ds(T_LAT, T_CTX), :]) * scale
    m = jnp.maximum(jnp.max(s_win, axis=-1, keepdims=True), jnp.max(s_ctx, axis=-1, keepdims=True))
    p_win = jnp.exp(s_win - m)
    p_ctx = jnp.exp(s_ctx - m)
    denom = jnp.sum(p_win, axis=-1, keepdims=True) + jnp.sum(p_ctx, axis=-1, keepdims=True)
    o = (_dot(p_win.astype(BF16), v_ref[pl.ds(start, NA_KT), :])
         + _dot(p_ctx.astype(BF16), v_ref[pl.ds(T_LAT, T_CTX), :]))
    o_ref[...] = (o / denom).astype(o_ref.dtype)


def _na_geometry():
    n_blk = GRID_ROWS // NA_QROWS
    dr_idx = np.zeros((3, NA_QROWS, NA_KROWS), np.int32)
    row_ok = np.zeros((3, NA_QROWS, NA_KROWS), bool)
    for kind, i_rep in enumerate((0, 1, n_blk - 1)):
        r0 = i_rep * NA_QROWS
        w0 = int(np.clip(r0 - WIN_ROWS // 2, 0, GRID_ROWS - NA_KROWS))
        r = r0 + np.arange(NA_QROWS)[:, None]
        kr = w0 + np.arange(NA_KROWS)[None, :]
        rstart = np.clip(r - WIN_ROWS // 2, 0, GRID_ROWS - WIN_ROWS)
        row_ok[kind] = (kr >= rstart) & (kr < rstart + WIN_ROWS)
        dr_idx[kind] = np.clip(kr - r + WIN_ROWS - 1, 0, 2 * WIN_ROWS - 2)
    c = np.arange(GRID_W)[:, None]
    kc = np.arange(GRID_W)[None, :]
    cstart = np.clip(c - WIN_COLS // 2, 0, GRID_W - WIN_COLS)
    col_ok = (kc >= cstart) & (kc < cstart + WIN_COLS)
    valid = row_ok[:, :, None, :, None] & col_ok[None, None, :, None, :]
    return dr_idx, valid.reshape(3, NA_QT, NA_KT)


def _na_bias_table(rpb):
    n_dr, n_dc = 2 * WIN_ROWS - 1, 2 * WIN_COLS - 1
    w = jnp.concatenate([rpb[..., WIN_COLS - 1:], jnp.zeros((NA_HEADS, n_dr, LANE - n_dc), F32),
                         rpb[..., :WIN_COLS - 1]], axis=-1)
    toep = jnp.tile(w, (1, 1, GRID_W))[..., :GRID_W * (LANE - 1)]
    toep = toep.reshape(NA_HEADS, n_dr, GRID_W, LANE - 1)[..., :GRID_W]
    dr_idx, valid = _na_geometry()
    tab = toep[:, dr_idx.reshape(-1)].reshape(NA_HEADS, 3, NA_QROWS, NA_KROWS, GRID_W, GRID_W)
    tab = tab.transpose(1, 0, 2, 4, 3, 5).reshape(3, NA_HEADS, NA_QT, NA_KT)
    return jnp.where(jnp.asarray(valid)[:, None], tab, NEG_BIG)


def _na(p, bias):
    n_blk = T_LAT // NA_QT
    kind = lambda i: jnp.where(i == 0, 0, jnp.where(i == n_blk - 1, 2, 1))
    return pl.pallas_call(
        _na_kernel,
        grid=(BATCH, NA_HEADS, n_blk),
        in_specs=[pl.BlockSpec((None, NA_QT, LANE), lambda b, h, i: (b, i, h)),
                  pl.BlockSpec((None, T_ALL, LANE), lambda b, h, i: (b, 0, 20 + h)),
                  pl.BlockSpec((None, T_ALL, LANE), lambda b, h, i: (b, 0, 28 + h)),
                  pl.BlockSpec((None, None, NA_QT, NA_KT), lambda b, h, i: (kind(i), h, 0, 0))],
        out_specs=pl.BlockSpec((None, NA_QT, LANE), lambda b, h, i: (b, i, h)),
        out_shape=jax.ShapeDtypeStruct((BATCH, T_LAT, NA_DIM), BF16),
        compiler_params=_cparams(3),
        name="nbr_attention",
    )(p, p, p, bias)


def _gla_kernel(q_ref, k_ref, v_ref, g_ref, lr_ref, wcat_ref, bup_ref, gain_ref,
                o_ref, acc_ref):
    r, c = GLA_SUPER, GLA_SUB
    n_sub = r // c
    dv2 = 2 * GLA_DV
    qscale = GLA_DK ** -0.5
    row = lax.broadcasted_iota(jnp.int32, (r, r), 0)
    col = lax.broadcasted_iota(jnp.int32, (r, r), 1)
    sub_shift = c.bit_length() - 1
    same = (row >> sub_shift) == (col >> sub_shift)
    blk_ones = jnp.where(same, 1.0, 0.0).astype(BF16)
    second_head_lane = lax.broadcasted_iota(jnp.int32, (1, LANE), 1) >= GLA_DK
    st_mask = ((lax.broadcasted_iota(jnp.int32, (dv2, LANE), 0) >= GLA_DV)
               == (lax.broadcasted_iota(jnp.int32, (dv2, LANE), 1) >= GLA_DK))
    gain = gain_ref[...]

    for d in range(2):
        tri = same & ((row >= col) if d == 0 else (col >= row))
        sum_lhs = jnp.concatenate([jnp.where(tri, 1.0, 0.0).astype(BF16), blk_ones], axis=0)

        def step(i, st, base, n, d=d, tri=tri, sum_lhs=sum_lhs):
            idx = i if d == 0 else n - 1 - i
            start = pl.multiple_of(base + idx * r, r)
            lr = lr_ref[pl.ds(start, r), :]
            z = _dot(jnp.concatenate([lr, lr], axis=1), wcat_ref[d]) + bup_ref[d]
            la = (jnp.minimum(z, 0.0) - jnp.log(1.0 + jnp.exp(-jnp.abs(z)))) * (1.0 / GLA_TAU)
            hi, mid, lo = _split3(la)
            sums = _dot(sum_lhs, hi) + _dot(sum_lhs, mid) + _dot(sum_lhs, lo)
            b = sums[:r]
            bt = sums[r:]
            q = q_ref[pl.ds(start, r), :].astype(F32) * qscale
            k = k_ref[pl.ds(start, r), :].astype(F32)
            v = v_ref[pl.ds(start, r), :]
            qi = q * jnp.exp(b)
            qib = qi.astype(BF16)
            ki = (k * jnp.exp(-b)).astype(BF16)
            ko = (k * jnp.exp(bt - b)).astype(BF16)
            dec = jnp.exp(bt)
            qm = jnp.concatenate([jnp.where(second_head_lane, 0.0, qi),
                                  jnp.where(second_head_lane, qi, 0.0)], axis=0).astype(BF16)
            att = _dot_nt(qm, ki)
            outs = []
            for j in range(2):
                att_j = jnp.where(tri, att[j * r:(j + 1) * r], 0.0).astype(BF16)
                outs.append(_dot(att_j, v[:, j * GLA_DV:(j + 1) * GLA_DV]))
            o_intra = jnp.concatenate(outs, axis=1)
            parts = [None] * n_sub
            for s in (range(n_sub) if d == 0 else reversed(range(n_sub))):
                lo_r, hi_r = s * c, (s + 1) * c
                parts[s] = o_intra[lo_r:hi_r] + _dot_nt(qib[lo_r:hi_r], st.astype(BF16))
                st = (st * dec[lo_r:lo_r + 1, :]
                      + jnp.where(st_mask, _dot_tn(v[lo_r:hi_r], ko[lo_r:hi_r]), 0.0))
            o = jnp.concatenate(parts, axis=0)
            if d == 0:
                acc_ref[pl.ds(start, r), :] = o
            else:
                o = acc_ref[pl.ds(start, r), :] + o
                gate = g_ref[pl.ds(start, r), :].astype(F32)
                on = jnp.concatenate([_rms(o[:, :GLA_DV]), _rms(o[:, GLA_DV:])], axis=1) * gain
                o_ref[pl.ds(start, r), :] = (on * _silu(gate)).astype(o_ref.dtype)
            return st

        st = jnp.zeros((dv2, LANE), F32)
        st = lax.fori_loop(0, T_CTX // r, functools.partial(step, base=T_LAT, n=T_CTX // r), st)
        lax.fori_loop(0, T_LAT // r, functools.partial(step, base=0, n=T_LAT // r), st)


def _gla(p, w_cat, b_up, gla_norm):
    dv2 = 2 * GLA_DV
    return pl.pallas_call(
        _gla_kernel,
        grid=(BATCH, GLA_HEADS // 2),
        in_specs=[pl.BlockSpec((None, T_ALL, LANE), lambda b, j: (b, 0, 8 + j)),
                  pl.BlockSpec((None, T_ALL, LANE), lambda b, j: (b, 0, 36 + j)),
                  pl.BlockSpec((None, T_ALL, dv2), lambda b, j: (b, 0, 20 + j)),
                  pl.BlockSpec((None, T_ALL, dv2), lambda b, j: (b, 0, 6 + j)),
                  pl.BlockSpec((None, T_ALL, LANE), lambda b, j: (b, 0, 48)),
                  pl.BlockSpec((2, 2 * LANE, LANE), lambda b, j: (0, 0, j)),
                  pl.BlockSpec((2, 1, LANE), lambda b, j: (0, 0, j)),
                  pl.BlockSpec((1, dv2), lambda b, j: (0, j))],
        out_specs=pl.BlockSpec((None, T_ALL, dv2), lambda b, j: (b, 0, j)),
        out_shape=jax.ShapeDtypeStruct((BATCH, T_ALL, GLA_V_DIM), BF16),
        scratch_shapes=[pltpu.VMEM((T_ALL, dv2), F32)],
        compiler_params=_cparams(2),
        name="gla",
    )(p, p, p, p, p, w_cat, b_up, gla_norm)


def _out_kernel(a1_ref, a2_ref, w1_ref, w2_ref, x_ref, gl_ref, gc_ref, gain_ref,
                scl_ref, scc_ref, shl_ref, shc_ref, rw_ref, xo_ref, h_ref, lg_ref, *, tile):
    half = tile // 2
    for r0 in (0, half):
        rows = pl.ds(r0, half)
        sel = functools.partial(_region_select, tile, row0=r0, n_rows=half)
        y = _dot(a1_ref[rows, :], w1_ref[...]) + _dot(a2_ref[rows, :], w2_ref[...])
        xn = x_ref[rows, :] + sel(gl_ref, gc_ref) * y
        xo_ref[rows, :] = xn
        h = _rms(xn) * gain_ref[...] * (1.0 + sel(scl_ref, scc_ref)) + sel(shl_ref, shc_ref)
        h_hi = h.astype(BF16)
        h_ref[rows, :] = h_hi
        h_lo = (h - h_hi.astype(F32)).astype(BF16)
        lg_ref[rows, :] = _dot(h_hi, rw_ref[...]) + _dot(h_lo, rw_ref[...])


def _out_proj(a1, a2, w_out_bf16, x, mod_lat, mod_ctx, norm_gain, rw_split, tile, n_tiles):
    half = w_out_bf16.shape[0] // 2
    t_out = n_tiles * tile
    row = lambda width: pl.BlockSpec((None, tile, width), lambda b, i: (b, i, 0))
    return pl.pallas_call(
        functools.partial(_out_kernel, tile=tile),
        grid=(BATCH, n_tiles),
        in_specs=[row(half), row(half),
                  pl.BlockSpec((half, D_MODEL), lambda b, i: (0, 0)),
                  pl.BlockSpec((half, D_MODEL), lambda b, i: (1, 0)),
                  row(D_MODEL), _lat_spec(2), _ctx_spec(2),
                  pl.BlockSpec((1, D_MODEL), lambda b, i: (0, 0)),
                  _lat_spec(4), _ctx_spec(4), _lat_spec(3), _ctx_spec(3),
                  pl.BlockSpec((D_MODEL, 2 * N_EXPERTS), lambda b, i: (0, 0))],
        out_specs=[row(D_MODEL), row(D_MODEL), row(2 * N_EXPERTS)],
        out_shape=[jax.ShapeDtypeStruct((BATCH, t_out, D_MODEL), F32),
                   jax.ShapeDtypeStruct((BATCH, t_out, D_MODEL), BF16),
                   jax.ShapeDtypeStruct((BATCH, t_out, 2 * N_EXPERTS), F32)],
        compiler_params=_cparams(2),
        name="out_proj",
    )(a1, a2, w_out_bf16, w_out_bf16, x, mod_lat, mod_ctx, norm_gain,
      mod_lat, mod_ctx, mod_lat, mod_ctx, rw_split)


def _moe_up_kernel(x_ref, wg_ref, wu_ref, o_ref):
    x = x_ref[...]
    a = _dot(x, wg_ref[...].astype(BF16))
    u = _dot(x, wu_ref[...].astype(BF16))
    o_ref[...] = (_silu(a) * u).astype(o_ref.dtype)


def _moe_down_kernel(h_ref, wd_ref, g_ref, o_ref):
    o_ref[...] = (_dot(h_ref[...], wd_ref[...].astype(BF16)) * g_ref[...]).astype(o_ref.dtype)


def _moe(xs, gates, w_gate, w_up, w_down, layer):
    e, m, _ = xs.shape
    tn = MOE_TN
    wspec = pl.BlockSpec((None, None, D_MODEL, tn), lambda e, f: (layer, e, 0, f))
    hmid = pl.pallas_call(
        _moe_up_kernel,
        grid=(e, D_EXPERT // tn),
        in_specs=[pl.BlockSpec((None, m, D_MODEL), lambda e, f: (e, 0, 0)), wspec, wspec],
        out_specs=pl.BlockSpec((None, m, tn), lambda e, f: (e, 0, f)),
        out_shape=jax.ShapeDtypeStruct((e, m, D_EXPERT), BF16),
        compiler_params=_cparams(2),
        name="moe_up",
    )(xs, w_gate, w_up)
    return pl.pallas_call(
        _moe_down_kernel,
        grid=(e, D_MODEL // tn),
        in_specs=[pl.BlockSpec((None, m, D_EXPERT), lambda e, f: (e, 0, 0)),
                  pl.BlockSpec((None, None, D_EXPERT, tn), lambda e, f: (layer, e, 0, f)),
                  pl.BlockSpec((None, m, 1), lambda e, f: (e, 0, 0))],
        out_specs=pl.BlockSpec((None, m, tn), lambda e, f: (e, 0, f)),
        out_shape=jax.ShapeDtypeStruct((e, m, D_MODEL), BF16),
        compiler_params=_cparams(2),
        name="moe_down",
    )(hmid, w_down, gates)


def _combine_moe(span_ref, pair_tok_ref, ys_hbm, acc_ref, buf_ref, sem, *, tile, n_pairs):
    b, i = pl.program_id(0), pl.program_id(1)
    n_tiles = pl.num_programs(1)
    step = b * n_tiles + i
    n_steps = pl.num_programs(0) * n_tiles
    w = PAIR_WIN
    shift = w.bit_length() - 1
    last_start = n_pairs // w - PAIR_CHUNK

    def first_window(s):
        return jnp.minimum(span_ref[2 * s] >> shift, last_start)

    def chunk_copy(sample, win, slot):
        return pltpu.make_async_copy(
            ys_hbm.at[pl.ds(sample * n_pairs + win * w, PAIR_CHUNK * w), :],
            buf_ref.at[slot], sem.at[slot])

    def window_copy(win, slot):
        return pltpu.make_async_copy(ys_hbm.at[pl.ds(b * n_pairs + win * w, w), :],
                                     buf_ref.at[slot, pl.ds(0, w), :], sem.at[slot])

    slot = step & 1
    w0 = first_window(step)

    @pl.when(step == 0)
    def _():
        chunk_copy(b, w0, slot).start()

    @pl.when(step + 1 < n_steps)
    def _():
        next_sample = jnp.where(i + 1 < n_tiles, b, b + 1)
        chunk_copy(next_sample, first_window(step + 1), 1 - slot).start()

    chunk_copy(b, w0, slot).wait()
    tok = i * tile + lax.broadcasted_iota(jnp.int32, (tile, 1), 0)

    def select(win):
        return jnp.where(tok == pair_tok_ref[pl.ds(win, 1), :], 1.0, 0.0).astype(BF16)

    sel = jnp.concatenate([select(w0 + c) for c in range(PAIR_CHUNK)], axis=1)
    acc_ref[...] = _dot(sel, buf_ref[slot])

    hi = span_ref[2 * step + 1]
    end_win = jnp.where(hi > 0, ((hi - 1) >> shift) + 1, 0)

    def body(win, carry):
        cp = window_copy(win, slot)
        cp.start()
        cp.wait()
        acc_ref[...] += _dot(select(win), buf_ref[slot, pl.ds(0, w), :])
        return carry

    lax.fori_loop(w0 + PAIR_CHUNK, end_win, body, 0)
    return acc_ref[...]


def _res_mid_kernel(span_ref, x_ref, pair_tok_ref, ys_hbm, gl_ref, gc_ref, gain_ref,
                    scl_ref, scc_ref, shl_ref, shc_ref, xo_ref, h_ref, acc_ref, buf_ref, sem,
                    *, tile, n_pairs):
    moe = _combine_moe(span_ref, pair_tok_ref, ys_hbm, acc_ref, buf_ref, sem,
                       tile=tile, n_pairs=n_pairs)
    xn = x_ref[...] + _region_select(tile, gl_ref, gc_ref) * moe
    xo_ref[...] = xn
    h_ref[...] = (_rms(xn) * gain_ref[...] * (1.0 + _region_select(tile, scl_ref, scc_ref))
                  + _region_select(tile, shl_ref, shc_ref)).astype(h_ref.dtype)


def _res_final_kernel(span_ref, x_ref, pair_tok_ref, ys_hbm, g_ref, gain_ref, o_ref,
                      acc_ref, buf_ref, sem, *, tile, n_pairs):
    moe = _combine_moe(span_ref, pair_tok_ref, ys_hbm, acc_ref, buf_ref, sem,
                       tile=tile, n_pairs=n_pairs)
    o_ref[...] = _rms(x_ref[...] + g_ref[...] * moe) * gain_ref[...]


def _combine_call(kernel_fn, name, tile, t_rows, spans, x, pair_tok, ys, params, param_specs,
                  out_dtypes):
    n_pairs = pair_tok.shape[1] * PAIR_WIN
    row = pl.BlockSpec((None, tile, D_MODEL), lambda b, i, *_: (b, i, 0))
    return pl.pallas_call(
        functools.partial(kernel_fn, tile=tile, n_pairs=n_pairs),
        grid_spec=pltpu.PrefetchScalarGridSpec(
            num_scalar_prefetch=1,
            grid=(BATCH, t_rows // tile),
            in_specs=[row,
                      pl.BlockSpec((None,) + pair_tok.shape[1:], lambda b, i, *_: (b, 0, 0)),
                      pl.BlockSpec(memory_space=pl.ANY)] + param_specs,
            out_specs=[row] * len(out_dtypes),
            scratch_shapes=[pltpu.VMEM((tile, D_MODEL), F32),
                            pltpu.VMEM((2, PAIR_CHUNK * PAIR_WIN, D_MODEL), BF16),
                            pltpu.SemaphoreType.DMA((2,))]),
        out_shape=[jax.ShapeDtypeStruct((BATCH, t_rows, D_MODEL), dt) for dt in out_dtypes],
        compiler_params=_cparams(2),
        name=name,
    )(spans, x, pair_tok, ys, *params)


def _res_mid(x, spans, pair_tok, ys, mod_lat, mod_ctx, next_lat, next_ctx, norm_gain_next):
    specs = [_lat_spec(5), _ctx_spec(5), pl.BlockSpec((1, D_MODEL), lambda b, i, *_: (0, 0)),
             _lat_spec(1), _ctx_spec(1), _lat_spec(0), _ctx_spec(0)]
    params = (mod_lat, mod_ctx, norm_gain_next, next_lat, next_ctx, next_lat, next_ctx)
    return _combine_call(_res_mid_kernel, "ffn_combine_residual_prenorm", COMB_TILE, T_ALL, spans,
                         x, pair_tok, ys, params, specs, (F32, BF16))


def _res_final(x, spans, pair_tok, ys, mod_lat, norm_final):
    specs = [_lat_spec(5), pl.BlockSpec((1, D_MODEL), lambda b, i, *_: (0, 0))]
    return _combine_call(_res_final_kernel, "ffn_combine_residual_final_norm", COMB_TILE, T_LAT,
                         spans, x, pair_tok, ys, (mod_lat, norm_final), specs, (F32,))[0]


def _rope_tables():
    half = HEAD_DIM // 2
    nf = half // 2
    t = np.arange(T_LAT)
    rows, cols = t // GRID_W, t % GRID_W
    freq = ROPE_BASE ** (-jnp.arange(nf, dtype=F32) / nf)
    ang_r = jnp.asarray(rows, F32)[:, None] * freq
    ang_c = jnp.asarray(cols, F32)[:, None] * freq
    zeros = jnp.zeros_like(ang_r)
    cos = jnp.concatenate([jnp.cos(ang_r), jnp.cos(ang_r), jnp.cos(ang_c), jnp.cos(ang_c)], axis=1)
    s1 = jnp.concatenate([-jnp.sin(ang_r), zeros, -jnp.sin(ang_c), zeros], axis=1)
    s2 = jnp.concatenate([zeros, jnp.sin(ang_r), zeros, jnp.sin(ang_c)], axis=1)
    return cos, s1, s2


def _router_split(rw):
    hi = rw.astype(BF16)
    lo = (rw - hi.astype(F32)).astype(BF16)
    return jnp.concatenate([hi, lo], axis=1)


def _route(logits, t0, t1, cap):
    aff = jax.nn.softmax(logits[:, t0:t1], axis=-1)
    g, idx = lax.top_k(aff.transpose(0, 2, 1), cap)
    return g, idx + t0


def _expert_ffn(h, logits2, w_gate, w_up, w_down, layer, with_ctx, tile):
    logits = logits2[..., :N_EXPERTS] + logits2[..., N_EXPERTS:]
    g, idx = _route(logits, 0, T_LAT, EC_CAPACITY * T_LAT // N_EXPERTS)
    if with_ctx:
        gc, idxc = _route(logits, T_LAT, T_ALL, EC_CAPACITY * T_CTX // N_EXPERTS)
        g = jnp.concatenate([g, gc], axis=-1)
        idx = jnp.concatenate([idx, idxc], axis=-1)
    cap = idx.shape[-1]
    t_rows = h.shape[1]
    n_pairs = N_EXPERTS * cap
    assert n_pairs % PAIR_WIN == 0
    flat = (idx + jnp.arange(BATCH)[:, None, None] * t_rows).transpose(1, 0, 2).reshape(N_EXPERTS, -1)
    gates = g.transpose(1, 0, 2).reshape(N_EXPERTS, BATCH * cap, 1)
    xs = h.reshape(BATCH * t_rows, D_MODEL)[flat]
    y = _moe(xs, gates, w_gate, w_up, w_down, layer)
    src = (jnp.arange(N_EXPERTS)[None, :, None] * (BATCH * cap)
           + jnp.arange(BATCH)[:, None, None] * cap + jnp.arange(cap)[None, None, :])
    pair_tok, src = lax.sort((idx.reshape(BATCH, n_pairs), src.reshape(BATCH, n_pairs)),
                             dimension=1, num_keys=1)
    ys = y.reshape(-1, D_MODEL)[src.reshape(-1)]
    bounds = jnp.arange(0, t_rows + tile, tile)
    below = jnp.sum(pair_tok[:, :, None] < bounds[None, None, :], axis=1)
    spans = jnp.stack([below[:, :-1], below[:, 1:]], axis=-1).reshape(-1).astype(jnp.int32)
    return spans, pair_tok.reshape(BATCH, n_pairs // PAIR_WIN, PAIR_WIN), ys


def kernel(x, c, ctx, c_ctx, ada_w, ada_b, norm_mix, norm_ffn, norm_final, ev_w_in, ev_w_out,
           ret_gamma_logit, ret_norm, gmlp_norm, gmlp_ws, gmlp_bs, od_w_in, od_w_out, na_rpb,
           gla_w_up, gla_b_up, gla_norm, router_w, moe_w_gate, moe_w_up, moe_w_down):
    assert x.shape == (BATCH, T_LAT, D_MODEL) and ctx.shape == (BATCH, T_CTX, D_MODEL)
    assert ada_w.shape[0] == 2 and ev_w_in.shape[0] == 1 and od_w_in.shape[0] == 1

    cond = jnp.concatenate([jax.nn.silu(c), jax.nn.silu(c_ctx)[None],
                            jnp.zeros((11, D_MODEL), F32)], axis=0).astype(BF16)
    mod_lat, mod_ctx = [], []
    for l in range(2):
        m = _mod_matmul(cond, ada_w, l) + ada_b[l]
        m = m[:BATCH + 1].reshape(BATCH + 1, 6, 1, D_MODEL)
        mod_lat.append(m[:BATCH])
        mod_ctx.append(m[BATCH])

    xa = jnp.concatenate([x, ctx], axis=1)
    cos, s1, s2 = _rope_tables()

    h = _modnorm(xa, norm_mix[0:1], mod_lat[0], mod_ctx[0], 1, 0)
    p = _matmul(h.reshape(BATCH * T_ALL, D_MODEL), ev_w_in[0].astype(BF16), 1024, 1024, BF16)
    p = p.reshape(BATCH, T_ALL, EVEN_COLS)
    log_g = jax.nn.log_sigmoid(ret_gamma_logit[0].astype(F32))
    ret = _retention(p, log_g, cos, s1, s2, ret_norm[0:1])
    gm = _gmlp(p, gmlp_norm[0:1], gmlp_ws[0].astype(BF16), gmlp_bs[0].T)
    xa, h2, logits = _out_proj(ret, gm, ev_w_out[0].astype(BF16), xa, mod_lat[0], mod_ctx[0],
                               norm_ffn[0:1], _router_split(router_w[0]), MIX_TILE,
                               T_ALL // MIX_TILE)
    spans, pair_tok, ys = _expert_ffn(h2, logits, moe_w_gate, moe_w_up, moe_w_down, 0, True,
                                      COMB_TILE)
    xa, h = _res_mid(xa, spans, pair_tok, ys, mod_lat[0], mod_ctx[0], mod_lat[1], mod_ctx[1],
                     norm_mix[1:2])

    w_in = jnp.pad(od_w_in[0], ((0, 0), (0, ODD_COLS_PAD - ODD_COLS))).astype(BF16)
    p = _matmul(h.reshape(BATCH * T_ALL, D_MODEL), w_in, 1024, 896, BF16)
    p = p.reshape(BATCH, T_ALL, ODD_COLS_PAD)
    o_na = _na(p, _na_bias_table(na_rpb[0]))
    w_up_full = jnp.zeros((2, LANE, GLA_QK_DIM), F32)
    for d in range(2):
        w_up_full = w_up_full.at[d, d * GLA_GATE_RANK:(d + 1) * GLA_GATE_RANK].set(gla_w_up[0, d])
    w_hi = w_up_full.astype(BF16)
    w_lo = (w_up_full - w_hi.astype(F32)).astype(BF16)
    gla = _gla(p, jnp.concatenate([w_hi, w_lo], axis=1), gla_b_up[0][:, None, :], gla_norm[0:1])
    xl, h2, logits = _out_proj(o_na, gla, od_w_out[0].astype(BF16), xa, mod_lat[1], mod_ctx[1],
                               norm_ffn[1:2], _router_split(router_w[1]), LAT_TILE,
                               T_LAT // LAT_TILE)
    spans, pair_tok, ys = _expert_ffn(h2, logits, moe_w_gate, moe_w_up, moe_w_down, 1, False,
                                      COMB_TILE)
    return _res_final(xl, spans, pair_tok, ys, mod_lat[1], norm_final[None])
```

```python
import functools

import numpy as np
import jax
import jax.numpy as jnp
from jax import lax
from jax.experimental import pallas as pl
from jax.experimental.pallas import tpu as pltpu

D_MODEL = 2048
BATCH = 4
T_LAT = 4096
T_CTX = 256
T_ALL = T_LAT + T_CTX
GRID_W = 64
GRID_ROWS = T_LAT // GRID_W
HEAD_DIM = 128
RET_HEADS = 8
RET_DIM = 1024
GMLP_DIM = 1024
GMLP_GROUPS = 8
GMLP_CHUNK = 128
NA_HEADS = 8
NA_DIM = 1024
WIN_ROWS = 8
WIN_COLS = 16
GLA_HEADS = 8
GLA_DK = 64
GLA_DV = 128
GLA_QK_DIM = 512
GLA_V_DIM = 1024
GLA_GATE_RANK = 16
GLA_TAU = 16.0
N_EXPERTS = 16
D_EXPERT = 2048
EC_CAPACITY = 2
ROPE_BASE = 10000.0
RMS_EPS = 1e-6
EVEN_COLS = 6144
ODD_COLS = 6176
ODD_COLS_PAD = 6272
LANE = 128

ROW_TILE = 256
MIX_TILE = 544
LAT_TILE = 512
RET_CHUNK = 256
GLA_SUB = 64
GLA_SUPER = 256
GLA_PAIRS = 2
NA_HEADS_PER_STEP = 2
RET_HEADS_PER_STEP = 2
NA_QROWS = 4
NA_QT = NA_QROWS * GRID_W
NA_KROWS = NA_QROWS + WIN_ROWS - 1
NA_KT = NA_KROWS * GRID_W
NEG_BIG = -1e30
MOE_TN = 512
PAIR_WIN = 256
PAIR_CHUNK = 3
COMB_TILE = 256
VMEM_LIMIT = 56 * 1024 * 1024

F32 = jnp.float32
BF16 = jnp.bfloat16


def _cparams(n_axes):
    return pltpu.CompilerParams(dimension_semantics=("arbitrary",) * n_axes,
                                vmem_limit_bytes=VMEM_LIMIT)


def _dot(a, b):
    return jnp.dot(a, b, preferred_element_type=F32)


def _dot_nt(a, b):
    return lax.dot_general(a, b, (((1,), (1,)), ((), ())), preferred_element_type=F32)


def _dot_tn(a, b):
    return lax.dot_general(a, b, (((0,), (0,)), ((), ())), preferred_element_type=F32)


def _silu(x):
    return x * jax.nn.sigmoid(x)


def _rms(x):
    return x * lax.rsqrt(jnp.mean(x * x, axis=-1, keepdims=True) + RMS_EPS)


def _split3(x):
    hi = x.astype(BF16)
    r1 = x - hi.astype(F32)
    mid = r1.astype(BF16)
    lo = (r1 - mid.astype(F32)).astype(BF16)
    return hi, mid, lo


def _mm_kernel(a_ref, w_ref, o_ref):
    o_ref[...] = _dot(a_ref[...], w_ref[...]).astype(o_ref.dtype)


def _matmul(a, w, tm, tn, out_dtype):
    m, k = a.shape
    n = w.shape[1]
    return pl.pallas_call(
        _mm_kernel,
        grid=(n // tn, m // tm),
        in_specs=[pl.BlockSpec((tm, k), lambda j, i: (i, 0)),
                  pl.BlockSpec((k, tn), lambda j, i: (0, j))],
        out_specs=pl.BlockSpec((tm, tn), lambda j, i: (i, j)),
        out_shape=jax.ShapeDtypeStruct((m, n), out_dtype),
        compiler_params=_cparams(2),
        name="matmul",
    )(a, w)


def _mod_mm_kernel(a_ref, w_ref, o_ref):
    o_ref[...] = _dot(a_ref[...], w_ref[...].astype(BF16))


def _mod_matmul(cond, ada_w, layer):
    m, k = cond.shape
    n = ada_w.shape[2]
    tn = 1024
    return pl.pallas_call(
        _mod_mm_kernel,
        grid=(n // tn,),
        in_specs=[pl.BlockSpec((m, k), lambda j: (0, 0)),
                  pl.BlockSpec((None, k, tn), lambda j: (layer, 0, j))],
        out_specs=pl.BlockSpec((m, tn), lambda j: (0, j)),
        out_shape=jax.ShapeDtypeStruct((m, n), F32),
        compiler_params=_cparams(1),
        name="mod_matmul",
    )(cond, ada_w)


def _lat_spec(which):
    return pl.BlockSpec((None, None, 1, D_MODEL), lambda b, i, *_: (b, which, 0, 0))


def _ctx_spec(which):
    return pl.BlockSpec((None, 1, D_MODEL), lambda b, i, *_: (which, 0, 0))


def _region_select(tile, lat_ref, ctx_ref, row0=0, n_rows=None):
    n_rows = tile if n_rows is None else n_rows
    rows = pl.program_id(1) * tile + row0 + lax.broadcasted_iota(jnp.int32, (n_rows, 1), 0)
    return jnp.where(rows >= T_LAT, ctx_ref[...], lat_ref[...])


def _modnorm_kernel(x_ref, gain_ref, scl_ref, scc_ref, shl_ref, shc_ref, o_ref, *, tile):
    sc = _region_select(tile, scl_ref, scc_ref)
    sh = _region_select(tile, shl_ref, shc_ref)
    o_ref[...] = (_rms(x_ref[...]) * gain_ref[...] * (1.0 + sc) + sh).astype(o_ref.dtype)


def _modnorm(x, gain, mod_lat, mod_ctx, sc_i, sh_i):
    tile = MIX_TILE
    row = pl.BlockSpec((None, tile, D_MODEL), lambda b, i: (b, i, 0))
    return pl.pallas_call(
        functools.partial(_modnorm_kernel, tile=tile),
        grid=(BATCH, T_ALL // tile),
        in_specs=[row, pl.BlockSpec((1, D_MODEL), lambda b, i: (0, 0)),
                  _lat_spec(sc_i), _ctx_spec(sc_i), _lat_spec(sh_i), _ctx_spec(sh_i)],
        out_specs=row,
        out_shape=jax.ShapeDtypeStruct((BATCH, T_ALL, D_MODEL), BF16),
        compiler_params=_cparams(2),
        name="modnorm",
    )(x, gain, mod_lat, mod_ctx, mod_lat, mod_ctx)


def _rotary(z, cos, s1, s2):
    return z * cos + pltpu.roll(z, 96, 1) * s1 + pltpu.roll(z, 32, 1) * s2


def _ret_kernel(lg_ref, q_ref, g_ref, k_ref, v_ref, cos_ref, s1_ref, s2_ref, gain_ref,
                o_ref, acc_ref, qs_ref, ks_ref):
    c = RET_CHUNK
    nh = RET_HEADS_PER_STEP
    scale = HEAD_DIM ** -0.5
    dist = (lax.broadcasted_iota(jnp.int32, (c, c), 0)
            - lax.broadcasted_iota(jnp.int32, (c, c), 1)).astype(F32)
    tcol = lax.broadcasted_iota(jnp.int32, (c, 1), 0).astype(F32)

    def head_consts(hp):
        h = pl.program_id(1) * nh + hp
        la_f, la_b = lg_ref[0, h], lg_ref[1, h]
        dmat = (jnp.where(dist >= 0.0, jnp.exp(la_f * jnp.maximum(dist, 0.0)), 0.0)
                + jnp.where(dist <= 0.0, jnp.exp(la_b * jnp.maximum(-dist, 0.0)), 0.0))
        fwd_dec = (jnp.exp(la_f * (tcol + 1.0)), jnp.exp(la_f * (c - 1.0 - tcol)),
                   jnp.exp(jnp.full((1, LANE), la_f * c, F32)))
        bwd_dec = (jnp.exp(la_b * (c - tcol)), jnp.exp(la_b * tcol),
                   jnp.exp(jnp.full((1, LANE), la_b * c, F32)))
        return dmat, fwd_dec, bwd_dec

    consts = [head_consts(hp) for hp in range(nh)]

    def fwd(i, sts, base, rot):
        start = pl.multiple_of(base + i * c, c)
        rows = pl.ds(start, c)
        if rot:
            cos, s1, s2 = cos_ref[rows, :], s1_ref[rows, :], s2_ref[rows, :]
        out = []
        for hp in range(nh):
            hl = pl.ds(hp * LANE, LANE)
            dmat, (qdec, kdec, cdec), _ = consts[hp]
            q = q_ref[rows, hl].astype(F32) * scale
            k = k_ref[rows, hl].astype(F32)
            v = v_ref[rows, hl]
            if rot:
                q = _rotary(q, cos, s1, s2)
                k = _rotary(k, cos, s1, s2)
            qs_ref[rows, hl] = q
            ks_ref[rows, hl] = k
            att = _dot_nt(q.astype(BF16), k.astype(BF16)) * dmat
            st = sts[hp]
            acc_ref[rows, hl] = (_dot(att.astype(BF16), v)
                                 + _dot_nt((q * qdec).astype(BF16), st.astype(BF16)))
            out.append(cdec * st + _dot_tn(v, (k * kdec).astype(BF16)))
        return tuple(out)

    def bwd(i, sts, base, n):
        start = pl.multiple_of(base + (n - 1 - i) * c, c)
        rows = pl.ds(start, c)
        out = []
        for hp in range(nh):
            hl = pl.ds(hp * LANE, LANE)
            _, _, (qdec, kdec, cdec) = consts[hp]
            q = qs_ref[rows, hl]
            k = ks_ref[rows, hl]
            v = v_ref[rows, hl]
            st = sts[hp]
            o = acc_ref[rows, hl] + _dot_nt((q * qdec).astype(BF16), st.astype(BF16))
            gate = g_ref[rows, hl].astype(F32)
            o_ref[rows, hl] = (_rms(o) * gain_ref[:, hl] * _silu(gate)).astype(o_ref.dtype)
            out.append(cdec * st + _dot_tn(v, (k * kdec).astype(BF16)))
        return tuple(out)

    zero = tuple(jnp.zeros((HEAD_DIM, HEAD_DIM), F32) for _ in range(nh))
    n_ctx, n_lat = T_CTX // c, T_LAT // c
    sts = lax.fori_loop(0, n_ctx, functools.partial(fwd, base=T_LAT, rot=False), zero)
    lax.fori_loop(0, n_lat, functools.partial(fwd, base=0, rot=True), sts)
    sts = lax.fori_loop(0, n_ctx, functools.partial(bwd, base=T_LAT, n=n_ctx), zero)
    lax.fori_loop(0, n_lat, functools.partial(bwd, base=0, n=n_lat), sts)


def _retention(p, log_g, cos, s1, s2, ret_norm):
    nh = RET_HEADS_PER_STEP
    hw = nh * LANE
    col = lambda off: pl.BlockSpec((None, T_ALL, hw),
                                   lambda b, h, lg: (b, 0, off * LANE // hw + h))
    tab = pl.BlockSpec((T_LAT, LANE), lambda b, h, lg: (0, 0))
    return pl.pallas_call(
        _ret_kernel,
        grid_spec=pltpu.PrefetchScalarGridSpec(
            num_scalar_prefetch=1,
            grid=(BATCH, RET_HEADS // nh),
            in_specs=[col(0), col(8), col(32), col(40), tab, tab, tab,
                      pl.BlockSpec((1, hw), lambda b, h, lg: (0, h))],
            out_specs=pl.BlockSpec((None, T_ALL, hw), lambda b, h, lg: (b, 0, h)),
            scratch_shapes=[pltpu.VMEM((T_ALL, hw), F32), pltpu.VMEM((T_ALL, hw), F32),
                            pltpu.VMEM((T_ALL, hw), F32)]),
        out_shape=jax.ShapeDtypeStruct((BATCH, T_ALL, RET_DIM), BF16),
        compiler_params=_cparams(2),
        name="retention",
    )(log_g, p, p, p, p, cos, s1, s2, ret_norm)


def _gmlp_kernel(u_ref, v_ref, gain_ref, ws_ref, bst_ref, o_ref):
    u = jax.nn.gelu(u_ref[...].astype(F32))
    v = _rms(jax.nn.gelu(v_ref[...].astype(F32))) * gain_ref[...]
    vb = v.astype(BF16)
    for n in range(ROW_TILE // GMLP_CHUNK):
        r0 = n * GMLP_CHUNK
        for g in range(GMLP_GROUPS):
            c0 = g * LANE
            mixed = (_dot(ws_ref[g], vb[r0:r0 + GMLP_CHUNK, c0:c0 + LANE])
                     + bst_ref[:, g:g + 1])
            o_ref[r0:r0 + GMLP_CHUNK, c0:c0 + LANE] = (
                u[r0:r0 + GMLP_CHUNK, c0:c0 + LANE] * mixed).astype(o_ref.dtype)


def _gmlp(p, gmlp_norm, ws_bf16, bs_t):
    return pl.pallas_call(
        _gmlp_kernel,
        grid=(BATCH, T_ALL // ROW_TILE),
        in_specs=[pl.BlockSpec((None, ROW_TILE, GMLP_DIM), lambda b, i: (b, i, 2)),
                  pl.BlockSpec((None, ROW_TILE, GMLP_DIM), lambda b, i: (b, i, 3)),
                  pl.BlockSpec((1, GMLP_DIM), lambda b, i: (0, 0)),
                  pl.BlockSpec((GMLP_GROUPS, GMLP_CHUNK, GMLP_CHUNK), lambda b, i: (0, 0, 0)),
                  pl.BlockSpec((GMLP_CHUNK, GMLP_GROUPS), lambda b, i: (0, 0))],
        out_specs=pl.BlockSpec((None, ROW_TILE, GMLP_DIM), lambda b, i: (b, i, 0)),
        out_shape=jax.ShapeDtypeStruct((BATCH, T_ALL, GMLP_DIM), BF16),
        compiler_params=_cparams(2),
        name="gmlp",
    )(p, p, gmlp_norm, ws_bf16, bs_t)


def _na_kernel(q_ref, k_ref, v_ref, bias_ref, o_ref):
    i = pl.program_id(2)
    w0 = jnp.clip(i * NA_QROWS - WIN_ROWS // 2, 0, GRID_ROWS - NA_KROWS)
    start = pl.multiple_of(w0 * GRID_W, GRID_W)
    scale = HEAD_DIM ** -0.5
    for hp in range(NA_HEADS_PER_STEP):
        hl = pl.ds(hp * LANE, LANE)
        q = q_ref[:, hl]
        s_win = _dot_nt(q, k_ref[pl.ds(start, NA_KT), hl]) * scale + bias_ref[hp]
        s_ctx = _dot_nt(q, k_ref[pl.ds(T_LAT, T_CTX), hl]) * scale
        m = jnp.maximum(jnp.max(s_win, axis=-1, keepdims=True),
                        jnp.max(s_ctx, axis=-1, keepdims=True))
        p_win = jnp.exp(s_win - m)
        p_ctx = jnp.exp(s_ctx - m)
        denom = jnp.sum(p_win, axis=-1, keepdims=True) + jnp.sum(p_ctx, axis=-1, keepdims=True)
        o = (_dot(p_win.astype(BF16), v_ref[pl.ds(start, NA_KT), hl])
             + _dot(p_ctx.astype(BF16), v_ref[pl.ds(T_LAT, T_CTX), hl]))
        o_ref[:, hl] = (o / denom).astype(o_ref.dtype)


def _na_geometry():
    n_blk = GRID_ROWS // NA_QROWS
    dr_idx = np.zeros((3, NA_QROWS, NA_KROWS), np.int32)
    row_ok = np.zeros((3, NA_QROWS, NA_KROWS), bool)
    for kind, i_rep in enumerate((0, 1, n_blk - 1)):
        r0 = i_rep * NA_QROWS
        w0 = int(np.clip(r0 - WIN_ROWS // 2, 0, GRID_ROWS - NA_KROWS))
        r = r0 + np.arange(NA_QROWS)[:, None]
        kr = w0 + np.arange(NA_KROWS)[None, :]
        rstart = np.clip(r - WIN_ROWS // 2, 0, GRID_ROWS - WIN_ROWS)
        row_ok[kind] = (kr >= rstart) & (kr < rstart + WIN_ROWS)
        dr_idx[kind] = np.clip(kr - r + WIN_ROWS - 1, 0, 2 * WIN_ROWS - 2)
    c = np.arange(GRID_W)[:, None]
    kc = np.arange(GRID_W)[None, :]
    cstart = np.clip(c - WIN_COLS // 2, 0, GRID_W - WIN_COLS)
    col_ok = (kc >= cstart) & (kc < cstart + WIN_COLS)
    valid = row_ok[:, :, None, :, None] & col_ok[None, None, :, None, :]
    return dr_idx, valid.reshape(3, NA_QT, NA_KT)


def _na_bias_table(rpb):
    n_dr, n_dc = 2 * WIN_ROWS - 1, 2 * WIN_COLS - 1
    w = jnp.concatenate([rpb[..., WIN_COLS - 1:], jnp.zeros((NA_HEADS, n_dr, LANE - n_dc), F32),
                         rpb[..., :WIN_COLS - 1]], axis=-1)
    toep = jnp.tile(w, (1, 1, GRID_W))[..., :GRID_W * (LANE - 1)]
    toep = toep.reshape(NA_HEADS, n_dr, GRID_W, LANE - 1)[..., :GRID_W]
    dr_idx, valid = _na_geometry()
    tab = toep[:, dr_idx.reshape(-1)].reshape(NA_HEADS, 3, NA_QROWS, NA_KROWS, GRID_W, GRID_W)
    tab = tab.transpose(1, 0, 2, 4, 3, 5).reshape(3, NA_HEADS, NA_QT, NA_KT)
    return jnp.where(jnp.asarray(valid)[:, None], tab, NEG_BIG)


def _na(p, bias):
    n_blk = T_LAT // NA_QT
    nh = NA_HEADS_PER_STEP
    hw = nh * LANE
    k0, v0 = 20 * LANE // hw, 28 * LANE // hw
    kind = lambda i: jnp.where(i == 0, 0, jnp.where(i == n_blk - 1, 2, 1))
    return pl.pallas_call(
        _na_kernel,
        grid=(BATCH, NA_HEADS // nh, n_blk),
        in_specs=[pl.BlockSpec((None, NA_QT, hw), lambda b, h, i: (b, i, h)),
                  pl.BlockSpec((None, T_ALL, hw), lambda b, h, i: (b, 0, k0 + h)),
                  pl.BlockSpec((None, T_ALL, hw), lambda b, h, i: (b, 0, v0 + h)),
                  pl.BlockSpec((None, nh, NA_QT, NA_KT), lambda b, h, i: (kind(i), h, 0, 0))],
        out_specs=pl.BlockSpec((None, NA_QT, hw), lambda b, h, i: (b, i, h)),
        out_shape=jax.ShapeDtypeStruct((BATCH, T_LAT, NA_DIM), BF16),
        compiler_params=_cparams(3),
        name="nbr_attention",
    )(p, p, p, bias)


def _gla_kernel(q_ref, k_ref, v_ref, g_ref, lr_ref, whi_ref, wlo_ref, bup_ref, gain_ref,
                o_ref, acc_ref):
    r, c = GLA_SUPER, GLA_SUB
    n_sub = r // c
    dv2 = 2 * GLA_DV
    qscale = GLA_DK ** -0.5
    row = lax.broadcasted_iota(jnp.int32, (r, r), 0)
    col = lax.broadcasted_iota(jnp.int32, (r, r), 1)
    sub_shift = c.bit_length() - 1
    same = (row >> sub_shift) == (col >> sub_shift)
    blk_ones = jnp.where(same, 1.0, 0.0).astype(BF16)
    second_head_lane = lax.broadcasted_iota(jnp.int32, (1, LANE), 1) >= GLA_DK
    st_mask = ((lax.broadcasted_iota(jnp.int32, (dv2, LANE), 0) >= GLA_DV)
               == (lax.broadcasted_iota(jnp.int32, (dv2, LANE), 1) >= GLA_DK))

    for d in range(2):
        tri = same & ((row >= col) if d == 0 else (col >= row))
        tri_b = jnp.where(tri, 1.0, 0.0).astype(BF16)

        def pair_step(hp, start, lr, st, d=d, tri=tri, tri_b=tri_b):
            kl = pl.ds(hp * LANE, LANE)
            vl = pl.ds(hp * dv2, dv2)
            z = _dot(lr, whi_ref[d, :, kl]) + _dot(lr, wlo_ref[d, :, kl]) + bup_ref[d, :, kl]
            la = (jnp.minimum(z, 0.0) - jnp.log(1.0 + jnp.exp(-jnp.abs(z)))) * (1.0 / GLA_TAU)
            hi, mid, lo = _split3(la)
            b = _dot(tri_b, hi) + _dot(tri_b, mid) + _dot(tri_b, lo)
            bt = _dot(blk_ones, hi) + _dot(blk_ones, mid) + _dot(blk_ones, lo)
            q = q_ref[pl.ds(start, r), kl].astype(F32) * qscale
            k = k_ref[pl.ds(start, r), kl].astype(F32)
            v = v_ref[pl.ds(start, r), vl]
            qi = q * jnp.exp(b)
            qib = qi.astype(BF16)
            ki = (k * jnp.exp(-b)).astype(BF16)
            ko = (k * jnp.exp(bt - b)).astype(BF16)
            dec = jnp.exp(bt)
            outs = []
            for j in range(2):
                qm = jnp.where(second_head_lane == (j == 1), qi, 0.0).astype(BF16)
                att = jnp.where(tri, _dot_nt(qm, ki), 0.0)
                outs.append(_dot(att.astype(BF16), v[:, j * GLA_DV:(j + 1) * GLA_DV]))
            o_intra = jnp.concatenate(outs, axis=1)
            parts = [None] * n_sub
            for s in (range(n_sub) if d == 0 else reversed(range(n_sub))):
                lo_r, hi_r = s * c, (s + 1) * c
                parts[s] = o_intra[lo_r:hi_r] + _dot_nt(qib[lo_r:hi_r], st.astype(BF16))
                st = (st * dec[lo_r:lo_r + 1, :]
                      + jnp.where(st_mask, _dot_tn(v[lo_r:hi_r], ko[lo_r:hi_r]), 0.0))
            o = jnp.concatenate(parts, axis=0)
            if d == 0:
                acc_ref[pl.ds(start, r), vl] = o
            else:
                o = acc_ref[pl.ds(start, r), vl] + o
                gate = g_ref[pl.ds(start, r), vl].astype(F32)
                on = (jnp.concatenate([_rms(o[:, :GLA_DV]), _rms(o[:, GLA_DV:])], axis=1)
                      * gain_ref[:, vl])
                o_ref[pl.ds(start, r), vl] = (on * _silu(gate)).astype(o_ref.dtype)
            return st

        def step(i, sts, base, n, d=d):
            idx = i if d == 0 else n - 1 - i
            start = pl.multiple_of(base + idx * r, r)
            lr = lr_ref[pl.ds(start, r), :]
            return tuple(pair_step(hp, start, lr, sts[hp]) for hp in range(GLA_PAIRS))

        sts = tuple(jnp.zeros((dv2, LANE), F32) for _ in range(GLA_PAIRS))
        sts = lax.fori_loop(0, T_CTX // r, functools.partial(step, base=T_LAT, n=T_CTX // r), sts)
        lax.fori_loop(0, T_LAT // r, functools.partial(step, base=0, n=T_LAT // r), sts)


def _gla(p, w_hi, w_lo, b_up, gla_norm):
    kw = GLA_PAIRS * LANE
    vw = GLA_PAIRS * 2 * GLA_DV
    q0, k0 = 8 * LANE // kw, 36 * LANE // kw
    v0, g0 = 40 * LANE // vw, 12 * LANE // vw
    return pl.pallas_call(
        _gla_kernel,
        grid=(BATCH, GLA_HEADS // (2 * GLA_PAIRS)),
        in_specs=[pl.BlockSpec((None, T_ALL, kw), lambda b, j: (b, 0, q0 + j)),
                  pl.BlockSpec((None, T_ALL, kw), lambda b, j: (b, 0, k0 + j)),
                  pl.BlockSpec((None, T_ALL, vw), lambda b, j: (b, 0, v0 + j)),
                  pl.BlockSpec((None, T_ALL, vw), lambda b, j: (b, 0, g0 + j)),
                  pl.BlockSpec((None, T_ALL, LANE), lambda b, j: (b, 0, 48)),
                  pl.BlockSpec((2, LANE, kw), lambda b, j: (0, 0, j)),
                  pl.BlockSpec((2, LANE, kw), lambda b, j: (0, 0, j)),
                  pl.BlockSpec((2, 1, kw), lambda b, j: (0, 0, j)),
                  pl.BlockSpec((1, vw), lambda b, j: (0, j))],
        out_specs=pl.BlockSpec((None, T_ALL, vw), lambda b, j: (b, 0, j)),
        out_shape=jax.ShapeDtypeStruct((BATCH, T_ALL, GLA_V_DIM), BF16),
        scratch_shapes=[pltpu.VMEM((T_ALL, vw), F32)],
        compiler_params=_cparams(2),
        name="gla",
    )(p, p, p, p, p, w_hi, w_lo, b_up, gla_norm)


def _out_kernel(a1_ref, a2_ref, w1_ref, w2_ref, x_ref, gl_ref, gc_ref, gain_ref,
                scl_ref, scc_ref, shl_ref, shc_ref, rw_ref, xo_ref, h_ref, lg_ref, *, tile):
    half = tile // 2
    for r0 in (0, half):
        rows = pl.ds(r0, half)
        sel = functools.partial(_region_select, tile, row0=r0, n_rows=half)
        y = _dot(a1_ref[rows, :], w1_ref[...]) + _dot(a2_ref[rows, :], w2_ref[...])
        xn = x_ref[rows, :] + sel(gl_ref, gc_ref) * y
        xo_ref[rows, :] = xn
        h = _rms(xn) * gain_ref[...] * (1.0 + sel(scl_ref, scc_ref)) + sel(shl_ref, shc_ref)
        h_hi = h.astype(BF16)
        h_ref[rows, :] = h_hi
        h_lo = (h - h_hi.astype(F32)).astype(BF16)
        lg_ref[rows, :] = _dot(h_hi, rw_ref[...]) + _dot(h_lo, rw_ref[...])


def _out_proj(a1, a2, w_out_bf16, x, mod_lat, mod_ctx, norm_gain, rw_split, tile, n_tiles):
    half = w_out_bf16.shape[0] // 2
    t_out = n_tiles * tile
    row = lambda width: pl.BlockSpec((None, tile, width), lambda b, i: (b, i, 0))
    return pl.pallas_call(
        functools.partial(_out_kernel, tile=tile),
        grid=(BATCH, n_tiles),
        in_specs=[row(half), row(half),
                  pl.BlockSpec((half, D_MODEL), lambda b, i: (0, 0)),
                  pl.BlockSpec((half, D_MODEL), lambda b, i: (1, 0)),
                  row(D_MODEL), _lat_spec(2), _ctx_spec(2),
                  pl.BlockSpec((1, D_MODEL), lambda b, i: (0, 0)),
                  _lat_spec(4), _ctx_spec(4), _lat_spec(3), _ctx_spec(3),
                  pl.BlockSpec((D_MODEL, 2 * N_EXPERTS), lambda b, i: (0, 0))],
        out_specs=[row(D_MODEL), row(D_MODEL), row(2 * N_EXPERTS)],
        out_shape=[jax.ShapeDtypeStruct((BATCH, t_out, D_MODEL), F32),
                   jax.ShapeDtypeStruct((BATCH, t_out, D_MODEL), BF16),
                   jax.ShapeDtypeStruct((BATCH, t_out, 2 * N_EXPERTS), F32)],
        compiler_params=_cparams(2),
        name="out_proj",
    )(a1, a2, w_out_bf16, w_out_bf16, x, mod_lat, mod_ctx, norm_gain,
      mod_lat, mod_ctx, mod_lat, mod_ctx, rw_split)


def _moe_up_kernel(x_ref, wg_ref, wu_ref, o_ref):
    x = x_ref[...]
    a = _dot(x, wg_ref[...].astype(BF16))
    u = _dot(x, wu_ref[...].astype(BF16))
    o_ref[...] = (_silu(a) * u).astype(o_ref.dtype)


def _moe_down_kernel(h_ref, wd_ref, g_ref, o_ref):
    o_ref[...] = (_dot(h_ref[...], wd_ref[...].astype(BF16)) * g_ref[...]).astype(o_ref.dtype)


def _moe(xs, gates, w_gate, w_up, w_down, layer):
    e, m, _ = xs.shape
    tn = MOE_TN
    wspec = pl.BlockSpec((None, None, D_MODEL, tn), lambda e, f: (layer, e, 0, f))
    hmid = pl.pallas_call(
        _moe_up_kernel,
        grid=(e, D_EXPERT // tn),
        in_specs=[pl.BlockSpec((None, m, D_MODEL), lambda e, f: (e, 0, 0)), wspec, wspec],
        out_specs=pl.BlockSpec((None, m, tn), lambda e, f: (e, 0, f)),
        out_shape=jax.ShapeDtypeStruct((e, m, D_EXPERT), BF16),
        compiler_params=_cparams(2),
        name="moe_up",
    )(xs, w_gate, w_up)
    return pl.pallas_call(
        _moe_down_kernel,
        grid=(e, D_MODEL // tn),
        in_specs=[pl.BlockSpec((None, m, D_EXPERT), lambda e, f: (e, 0, 0)),
                  pl.BlockSpec((None, None, D_EXPERT, tn), lambda e, f: (layer, e, 0, f)),
                  pl.BlockSpec((None, m, 1), lambda e, f: (e, 0, 0))],
        out_specs=pl.BlockSpec((None, m, tn), lambda e, f: (e, 0, f)),
        out_shape=jax.ShapeDtypeStruct((e, m, D_MODEL), BF16),
        compiler_params=_cparams(2),
        name="moe_down",
    )(hmid, w_down, gates)


def _combine_moe(span_ref, pair_tok_ref, ys_hbm, acc_ref, buf_ref, sem, *, tile, n_pairs):
    b, i = pl.program_id(0), pl.program_id(1)
    n_tiles = pl.num_programs(1)
    step = b * n_tiles + i
    n_steps = pl.num_programs(0) * n_tiles
    w = PAIR_WIN
    shift = w.bit_length() - 1
    last_start = n_pairs // w - PAIR_CHUNK

    def first_window(s):
        return jnp.minimum(span_ref[2 * s] >> shift, last_start)

    def chunk_copy(sample, win, slot):
        return pltpu.make_async_copy(
            ys_hbm.at[pl.ds(sample * n_pairs + win * w, PAIR_CHUNK * w), :],
            buf_ref.at[slot], sem.at[slot])

    def window_copy(win, slot):
        return pltpu.make_async_copy(ys_hbm.at[pl.ds(b * n_pairs + win * w, w), :],
                                     buf_ref.at[slot, pl.ds(0, w), :], sem.at[slot])

    slot = step & 1
    w0 = first_window(step)

    @pl.when(step == 0)
    def _():
        chunk_copy(b, w0, slot).start()

    @pl.when(step + 1 < n_steps)
    def _():
        next_sample = jnp.where(i + 1 < n_tiles, b, b + 1)
        chunk_copy(next_sample, first_window(step + 1), 1 - slot).start()

    chunk_copy(b, w0, slot).wait()
    tok = i * tile + lax.broadcasted_iota(jnp.int32, (tile, 1), 0)

    def select(win):
        return jnp.where(tok == pair_tok_ref[pl.ds(win, 1), :], 1.0, 0.0).astype(BF16)

    sel = jnp.concatenate([select(w0 + c) for c in range(PAIR_CHUNK)], axis=1)
    acc_ref[...] = _dot(sel, buf_ref[slot])

    hi = span_ref[2 * step + 1]
    end_win = jnp.where(hi > 0, ((hi - 1) >> shift) + 1, 0)

    def body(win, carry):
        cp = window_copy(win, slot)
        cp.start()
        cp.wait()
        acc_ref[...] += _dot(select(win), buf_ref[slot, pl.ds(0, w), :])
        return carry

    lax.fori_loop(w0 + PAIR_CHUNK, end_win, body, 0)
    return acc_ref[...]


def _res_mid_kernel(span_ref, x_ref, pair_tok_ref, ys_hbm, gl_ref, gc_ref, gain_ref,
                    scl_ref, scc_ref, shl_ref, shc_ref, xo_ref, h_ref, acc_ref, buf_ref, sem,
                    *, tile, n_pairs):
    moe = _combine_moe(span_ref, pair_tok_ref, ys_hbm, acc_ref, buf_ref, sem,
                       tile=tile, n_pairs=n_pairs)
    xn = x_ref[...] + _region_select(tile, gl_ref, gc_ref) * moe
    xo_ref[...] = xn
    h_ref[...] = (_rms(xn) * gain_ref[...] * (1.0 + _region_select(tile, scl_ref, scc_ref))
                  + _region_select(tile, shl_ref, shc_ref)).astype(h_ref.dtype)


def _res_final_kernel(span_ref, x_ref, pair_tok_ref, ys_hbm, g_ref, gain_ref, o_ref,
                      acc_ref, buf_ref, sem, *, tile, n_pairs):
    moe = _combine_moe(span_ref, pair_tok_ref, ys_hbm, acc_ref, buf_ref, sem,
                       tile=tile, n_pairs=n_pairs)
    o_ref[...] = _rms(x_ref[...] + g_ref[...] * moe) * gain_ref[...]


def _combine_call(kernel_fn, name, tile, t_rows, spans, x, pair_tok, ys, params, param_specs,
                  out_dtypes):
    n_pairs = pair_tok.shape[1] * PAIR_WIN
    row = pl.BlockSpec((None, tile, D_MODEL), lambda b, i, *_: (b, i, 0))
    return pl.pallas_call(
        functools.partial(kernel_fn, tile=tile, n_pairs=n_pairs),
        grid_spec=pltpu.PrefetchScalarGridSpec(
            num_scalar_prefetch=1,
            grid=(BATCH, t_rows // tile),
            in_specs=[row,
                      pl.BlockSpec((None,) + pair_tok.shape[1:], lambda b, i, *_: (b, 0, 0)),
                      pl.BlockSpec(memory_space=pl.ANY)] + param_specs,
            out_specs=[row] * len(out_dtypes),
            scratch_shapes=[pltpu.VMEM((tile, D_MODEL), F32),
                            pltpu.VMEM((2, PAIR_CHUNK * PAIR_WIN, D_MODEL), BF16),
                            pltpu.SemaphoreType.DMA((2,))]),
        out_shape=[jax.ShapeDtypeStruct((BATCH, t_rows, D_MODEL), dt) for dt in out_dtypes],
        compiler_params=_cparams(2),
        name=name,
    )(spans, x, pair_tok, ys, *params)


def _res_mid(x, spans, pair_tok, ys, mod_lat, mod_ctx, next_lat, next_ctx, norm_gain_next):
    specs = [_lat_spec(5), _ctx_spec(5), pl.BlockSpec((1, D_MODEL), lambda b, i, *_: (0, 0)),
             _lat_spec(1), _ctx_spec(1), _lat_spec(0), _ctx_spec(0)]
    params = (mod_lat, mod_ctx, norm_gain_next, next_lat, next_ctx, next_lat, next_ctx)
    return _combine_call(_res_mid_kernel, "ffn_combine_residual_prenorm", COMB_TILE, T_ALL, spans,
                         x, pair_tok, ys, params, specs, (F32, BF16))


def _res_final(x, spans, pair_tok, ys, mod_lat, norm_final):
    specs = [_lat_spec(5), pl.BlockSpec((1, D_MODEL), lambda b, i, *_: (0, 0))]
    return _combine_call(_res_final_kernel, "ffn_combine_residual_final_norm", COMB_TILE, T_LAT,
                         spans, x, pair_tok, ys, (mod_lat, norm_final), specs, (F32,))[0]


def _rope_tables():
    half = HEAD_DIM // 2
    nf = half // 2
    t = np.arange(T_LAT)
    rows, cols = t // GRID_W, t % GRID_W
    freq = ROPE_BASE ** (-jnp.arange(nf, dtype=F32) / nf)
    ang_r = jnp.asarray(rows, F32)[:, None] * freq
    ang_c = jnp.asarray(cols, F32)[:, None] * freq
    zeros = jnp.zeros_like(ang_r)
    cos = jnp.concatenate([jnp.cos(ang_r), jnp.cos(ang_r), jnp.cos(ang_c), jnp.cos(ang_c)], axis=1)
    s1 = jnp.concatenate([-jnp.sin(ang_r), zeros, -jnp.sin(ang_c), zeros], axis=1)
    s2 = jnp.concatenate([zeros, jnp.sin(ang_r), zeros, jnp.sin(ang_c)], axis=1)
    return cos, s1, s2


def _router_split(rw):
    hi = rw.astype(BF16)
    lo = (rw - hi.astype(F32)).astype(BF16)
    return jnp.concatenate([hi, lo], axis=1)


def _route(logits, t0, t1, cap):
    aff = jax.nn.softmax(logits[:, t0:t1], axis=-1)
    g, idx = lax.top_k(aff.transpose(0, 2, 1), cap)
    return g, idx + t0


def _expert_ffn(h, logits2, w_gate, w_up, w_down, layer, with_ctx, tile):
    logits = logits2[..., :N_EXPERTS] + logits2[..., N_EXPERTS:]
    g, idx = _route(logits, 0, T_LAT, EC_CAPACITY * T_LAT // N_EXPERTS)
    if with_ctx:
        gc, idxc = _route(logits, T_LAT, T_ALL, EC_CAPACITY * T_CTX // N_EXPERTS)
        g = jnp.concatenate([g, gc], axis=-1)
        idx = jnp.concatenate([idx, idxc], axis=-1)
    cap = idx.shape[-1]
    t_rows = h.shape[1]
    n_pairs = N_EXPERTS * cap
    assert n_pairs % PAIR_WIN == 0
    flat = (idx + jnp.arange(BATCH)[:, None, None] * t_rows).transpose(1, 0, 2).reshape(N_EXPERTS, -1)
    gates = g.transpose(1, 0, 2).reshape(N_EXPERTS, BATCH * cap, 1)
    xs = h.reshape(BATCH * t_rows, D_MODEL)[flat]
    y = _moe(xs, gates, w_gate, w_up, w_down, layer)
    src = (jnp.arange(N_EXPERTS)[None, :, None] * (BATCH * cap)
           + jnp.arange(BATCH)[:, None, None] * cap + jnp.arange(cap)[None, None, :])
    pair_tok, src = lax.sort((idx.reshape(BATCH, n_pairs), src.reshape(BATCH, n_pairs)),
                             dimension=1, num_keys=1)
    ys = y.reshape(-1, D_MODEL)[src.reshape(-1)]
    bounds = jnp.arange(0, t_rows + tile, tile)
    below = jnp.sum(pair_tok[:, :, None] < bounds[None, None, :], axis=1)
    spans = jnp.stack([below[:, :-1], below[:, 1:]], axis=-1).reshape(-1).astype(jnp.int32)
    return spans, pair_tok.reshape(BATCH, n_pairs // PAIR_WIN, PAIR_WIN), ys


def kernel(x, c, ctx, c_ctx, ada_w, ada_b, norm_mix, norm_ffn, norm_final, ev_w_in, ev_w_out,
           ret_gamma_logit, ret_norm, gmlp_norm, gmlp_ws, gmlp_bs, od_w_in, od_w_out, na_rpb,
           gla_w_up, gla_b_up, gla_norm, router_w, moe_w_gate, moe_w_up, moe_w_down):
    assert x.shape == (BATCH, T_LAT, D_MODEL) and ctx.shape == (BATCH, T_CTX, D_MODEL)
    assert ada_w.shape[0] == 2 and ev_w_in.shape[0] == 1 and od_w_in.shape[0] == 1

    cond = jnp.concatenate([jax.nn.silu(c), jax.nn.silu(c_ctx)[None],
                            jnp.zeros((11, D_MODEL), F32)], axis=0).astype(BF16)
    mod_lat, mod_ctx = [], []
    for l in range(2):
        m = _mod_matmul(cond, ada_w, l) + ada_b[l]
        m = m[:BATCH + 1].reshape(BATCH + 1, 6, 1, D_MODEL)
        mod_lat.append(m[:BATCH])
        mod_ctx.append(m[BATCH])

    xa = jnp.concatenate([x, ctx], axis=1)
    cos, s1, s2 = _rope_tables()

    h = _modnorm(xa, norm_mix[0:1], mod_lat[0], mod_ctx[0], 1, 0)
    p = _matmul(h.reshape(BATCH * T_ALL, D_MODEL), ev_w_in[0].astype(BF16), 1024, 1024, BF16)
    p = p.reshape(BATCH, T_ALL, EVEN_COLS)
    log_g = jax.nn.log_sigmoid(ret_gamma_logit[0].astype(F32))
    ret = _retention(p, log_g, cos, s1, s2, ret_norm[0:1])
    gm = _gmlp(p, gmlp_norm[0:1], gmlp_ws[0].astype(BF16), gmlp_bs[0].T)
    xa, h2, logits = _out_proj(ret, gm, ev_w_out[0].astype(BF16), xa, mod_lat[0], mod_ctx[0],
                               norm_ffn[0:1], _router_split(router_w[0]), MIX_TILE,
                               T_ALL // MIX_TILE)
    spans, pair_tok, ys = _expert_ffn(h2, logits, moe_w_gate, moe_w_up, moe_w_down, 0, True,
                                      COMB_TILE)
    xa, h = _res_mid(xa, spans, pair_tok, ys, mod_lat[0], mod_ctx[0], mod_lat[1], mod_ctx[1],
                     norm_mix[1:2])

    w_in = jnp.pad(od_w_in[0], ((0, 0), (0, ODD_COLS_PAD - ODD_COLS))).astype(BF16)
    p = _matmul(h.reshape(BATCH * T_ALL, D_MODEL), w_in, 1024, 896, BF16)
    p = p.reshape(BATCH, T_ALL, ODD_COLS_PAD)
    o_na = _na(p, _na_bias_table(na_rpb[0]))
    w_up_full = jnp.zeros((2, LANE, GLA_QK_DIM), F32)
    for d in range(2):
        w_up_full = w_up_full.at[d, d * GLA_GATE_RANK:(d + 1) * GLA_GATE_RANK].set(gla_w_up[0, d])
    w_hi = w_up_full.astype(BF16)
    w_lo = (w_up_full - w_hi.astype(F32)).astype(BF16)
    gla = _gla(p, w_hi, w_lo, gla_b_up[0][:, None, :], gla_norm[0:1])
    xl, h2, logits = _out_proj(o_na, gla, od_w_out[0].astype(BF16), xa, mod_lat[1], mod_ctx[1],
                               norm_ffn[1:2], _router_split(router_w[1]), LAT_TILE,
                               T_LAT // LAT_TILE)
    spans, pair_tok, ys = _expert_ffn(h2, logits, moe_w_gate, moe_w_up, moe_w_down, 1, False,
                                      COMB_TILE)
    return _res_final(xl, spans, pair_tok, ys, mod_lat[1], norm_final[None])
```

```python
import functools

import numpy as np
import jax
import jax.numpy as jnp
from jax import lax
from jax.experimental import pallas as pl
from jax.experimental.pallas import tpu as pltpu

D_MODEL = 2048
BATCH = 4
T_LAT = 4096
T_CTX = 256
T_ALL = T_LAT + T_CTX
GRID_W = 64
GRID_ROWS = T_LAT // GRID_W
HEAD_DIM = 128
RET_HEADS = 8
RET_DIM = 1024
GMLP_DIM = 1024
GMLP_GROUPS = 8
GMLP_CHUNK = 128
NA_HEADS = 8
NA_DIM = 1024
WIN_ROWS = 8
WIN_COLS = 16
GLA_HEADS = 8
GLA_DK = 64
GLA_DV = 128
GLA_QK_DIM = 512
GLA_V_DIM = 1024
GLA_GATE_RANK = 16
GLA_TAU = 16.0
N_EXPERTS = 16
D_EXPERT = 2048
EC_CAPACITY = 2
ROPE_BASE = 10000.0
RMS_EPS = 1e-6
EVEN_COLS = 6144
ODD_COLS = 6176
ODD_MAIN_COLS = 6144
LANE = 128

ROW_TILE = 256
MIX_TILE = 544
LAT_TILE = 512
RET_CHUNK = 256
GLA_SUB = 64
GLA_SUPER = 256
GLA_PAIRS = 2
NA_HEADS_PER_STEP = 2
RET_HEADS_PER_STEP = 2
NA_QROWS = 4
NA_QT = NA_QROWS * GRID_W
NA_KROWS = NA_QROWS + WIN_ROWS - 1
NA_KT = NA_KROWS * GRID_W
NEG_BIG = -1e30
MOE_TN = 512
PAIR_WIN = 256
PAIR_CHUNK = 3
COMB_TILE = 256
VMEM_LIMIT = 56 * 1024 * 1024

F32 = jnp.float32
BF16 = jnp.bfloat16


def _cparams(n_axes):
    return pltpu.CompilerParams(dimension_semantics=("arbitrary",) * n_axes,
                                vmem_limit_bytes=VMEM_LIMIT)


def _dot(a, b):
    return jnp.dot(a, b, preferred_element_type=F32)


def _dot_nt(a, b):
    return lax.dot_general(a, b, (((1,), (1,)), ((), ())), preferred_element_type=F32)


def _dot_tn(a, b):
    return lax.dot_general(a, b, (((0,), (0,)), ((), ())), preferred_element_type=F32)


def _silu(x):
    return x * jax.nn.sigmoid(x)


def _rms(x):
    return x * lax.rsqrt(jnp.mean(x * x, axis=-1, keepdims=True) + RMS_EPS)


def _split2(x):
    hi = x.astype(BF16)
    lo = (x - hi.astype(F32)).astype(BF16)
    return hi, lo


def _mm_kernel(a_ref, w_ref, o_ref):
    o_ref[...] = _dot(a_ref[...], w_ref[...]).astype(o_ref.dtype)


def _matmul(a, w, tm, tn, out_dtype):
    m, k = a.shape
    n = w.shape[1]
    return pl.pallas_call(
        _mm_kernel,
        grid=(n // tn, m // tm),
        in_specs=[pl.BlockSpec((tm, k), lambda j, i: (i, 0)),
                  pl.BlockSpec((k, tn), lambda j, i: (0, j))],
        out_specs=pl.BlockSpec((tm, tn), lambda j, i: (i, j)),
        out_shape=jax.ShapeDtypeStruct((m, n), out_dtype),
        compiler_params=_cparams(2),
        name="matmul",
    )(a, w)


def _mod_mm_kernel(a_ref, w_ref, o_ref):
    o_ref[...] = _dot(a_ref[...], w_ref[...].astype(BF16))


def _mod_matmul(cond, ada_w, layer):
    m, k = cond.shape
    n = ada_w.shape[2]
    tn = 1024
    return pl.pallas_call(
        _mod_mm_kernel,
        grid=(n // tn,),
        in_specs=[pl.BlockSpec((m, k), lambda j: (0, 0)),
                  pl.BlockSpec((None, k, tn), lambda j: (layer, 0, j))],
        out_specs=pl.BlockSpec((m, tn), lambda j: (0, j)),
        out_shape=jax.ShapeDtypeStruct((m, n), F32),
        compiler_params=_cparams(1),
        name="mod_matmul",
    )(cond, ada_w)


def _lat_spec(which):
    return pl.BlockSpec((None, None, 1, D_MODEL), lambda b, i, *_: (b, which, 0, 0))


def _ctx_spec(which):
    return pl.BlockSpec((None, 1, D_MODEL), lambda b, i, *_: (which, 0, 0))


def _region_select(tile, lat_ref, ctx_ref, row0=0, n_rows=None, has_ctx=True):
    if not has_ctx:
        return lat_ref[...]
    n_rows = tile if n_rows is None else n_rows
    rows = pl.program_id(1) * tile + row0 + lax.broadcasted_iota(jnp.int32, (n_rows, 1), 0)
    return jnp.where(rows >= T_LAT, ctx_ref[...], lat_ref[...])


def _modnorm_kernel(x_ref, gain_ref, scl_ref, scc_ref, shl_ref, shc_ref, o_ref, *, tile):
    sc = _region_select(tile, scl_ref, scc_ref)
    sh = _region_select(tile, shl_ref, shc_ref)
    o_ref[...] = (_rms(x_ref[...]) * gain_ref[...] * (1.0 + sc) + sh).astype(o_ref.dtype)


def _modnorm(x, gain, mod_lat, mod_ctx, sc_i, sh_i):
    tile = MIX_TILE
    row = pl.BlockSpec((None, tile, D_MODEL), lambda b, i: (b, i, 0))
    return pl.pallas_call(
        functools.partial(_modnorm_kernel, tile=tile),
        grid=(BATCH, T_ALL // tile),
        in_specs=[row, pl.BlockSpec((1, D_MODEL), lambda b, i: (0, 0)),
                  _lat_spec(sc_i), _ctx_spec(sc_i), _lat_spec(sh_i), _ctx_spec(sh_i)],
        out_specs=row,
        out_shape=jax.ShapeDtypeStruct((BATCH, T_ALL, D_MODEL), BF16),
        compiler_params=_cparams(2),
        name="modnorm",
    )(x, gain, mod_lat, mod_ctx, mod_lat, mod_ctx)


def _rotary(z, cos, s1, s2):
    return z * cos + pltpu.roll(z, 96, 1) * s1 + pltpu.roll(z, 32, 1) * s2


def _ret_kernel(lg_ref, q_ref, g_ref, k_ref, v_ref, cos_ref, s1_ref, s2_ref, gain_ref,
                o_ref, acc_ref, qs_ref, ks_ref):
    c = RET_CHUNK
    nh = RET_HEADS_PER_STEP
    scale = HEAD_DIM ** -0.5
    dist = (lax.broadcasted_iota(jnp.int32, (c, c), 0)
            - lax.broadcasted_iota(jnp.int32, (c, c), 1)).astype(F32)
    tcol = lax.broadcasted_iota(jnp.int32, (c, 1), 0).astype(F32)

    def head_consts(hp):
        h = pl.program_id(1) * nh + hp
        la_f, la_b = lg_ref[0, h], lg_ref[1, h]
        dmat = (jnp.where(dist >= 0.0, jnp.exp(la_f * jnp.maximum(dist, 0.0)), 0.0)
                + jnp.where(dist <= 0.0, jnp.exp(la_b * jnp.maximum(-dist, 0.0)), 0.0))
        fwd_dec = (jnp.exp(la_f * (tcol + 1.0)), jnp.exp(la_f * (c - 1.0 - tcol)),
                   jnp.exp(jnp.full((1, LANE), la_f * c, F32)))
        bwd_dec = (jnp.exp(la_b * (c - tcol)), jnp.exp(la_b * tcol),
                   jnp.exp(jnp.full((1, LANE), la_b * c, F32)))
        return dmat, fwd_dec, bwd_dec

    consts = [head_consts(hp) for hp in range(nh)]

    def fwd(i, sts, base, rot):
        start = pl.multiple_of(base + i * c, c)
        rows = pl.ds(start, c)
        if rot:
            cos, s1, s2 = cos_ref[rows, :], s1_ref[rows, :], s2_ref[rows, :]
        out = []
        for hp in range(nh):
            hl = pl.ds(hp * LANE, LANE)
            dmat, (qdec, kdec, cdec), _ = consts[hp]
            q = q_ref[rows, hl].astype(F32) * scale
            k = k_ref[rows, hl].astype(F32)
            v = v_ref[rows, hl]
            if rot:
                q = _rotary(q, cos, s1, s2)
                k = _rotary(k, cos, s1, s2)
            qs_ref[rows, hl] = q
            ks_ref[rows, hl] = k
            att = _dot_nt(q.astype(BF16), k.astype(BF16)) * dmat
            st = sts[hp]
            acc_ref[rows, hl] = (_dot(att.astype(BF16), v)
                                 + _dot_nt((q * qdec).astype(BF16), st.astype(BF16)))
            out.append(cdec * st + _dot_tn(v, (k * kdec).astype(BF16)))
        return tuple(out)

    def bwd(i, sts, base, n):
        start = pl.multiple_of(base + (n - 1 - i) * c, c)
        rows = pl.ds(start, c)
        out = []
        for hp in range(nh):
            hl = pl.ds(hp * LANE, LANE)
            _, _, (qdec, kdec, cdec) = consts[hp]
            q = qs_ref[rows, hl]
            k = ks_ref[rows, hl]
            v = v_ref[rows, hl]
            st = sts[hp]
            o = acc_ref[rows, hl] + _dot_nt((q * qdec).astype(BF16), st.astype(BF16))
            gate = g_ref[rows, hl].astype(F32)
            o_ref[rows, hl] = (_rms(o) * gain_ref[:, hl] * _silu(gate)).astype(o_ref.dtype)
            out.append(cdec * st + _dot_tn(v, (k * kdec).astype(BF16)))
        return tuple(out)

    zero = tuple(jnp.zeros((HEAD_DIM, HEAD_DIM), F32) for _ in range(nh))
    n_ctx, n_lat = T_CTX // c, T_LAT // c
    sts = lax.fori_loop(0, n_ctx, functools.partial(fwd, base=T_LAT, rot=False), zero)
    lax.fori_loop(0, n_lat, functools.partial(fwd, base=0, rot=True), sts)
    sts = lax.fori_loop(0, n_ctx, functools.partial(bwd, base=T_LAT, n=n_ctx), zero)
    lax.fori_loop(0, n_lat, functools.partial(bwd, base=0, n=n_lat), sts)


def _retention(p, log_g, cos, s1, s2, ret_norm):
    nh = RET_HEADS_PER_STEP
    hw = nh * LANE
    col = lambda off: pl.BlockSpec((None, T_ALL, hw),
                                   lambda b, h, lg: (b, 0, off * LANE // hw + h))
    tab = pl.BlockSpec((T_LAT, LANE), lambda b, h, lg: (0, 0))
    return pl.pallas_call(
        _ret_kernel,
        grid_spec=pltpu.PrefetchScalarGridSpec(
            num_scalar_prefetch=1,
            grid=(BATCH, RET_HEADS // nh),
            in_specs=[col(0), col(8), col(32), col(40), tab, tab, tab,
                      pl.BlockSpec((1, hw), lambda b, h, lg: (0, h))],
            out_specs=pl.BlockSpec((None, T_ALL, hw), lambda b, h, lg: (b, 0, h)),
            scratch_shapes=[pltpu.VMEM((T_ALL, hw), F32), pltpu.VMEM((T_ALL, hw), F32),
                            pltpu.VMEM((T_ALL, hw), F32)]),
        out_shape=jax.ShapeDtypeStruct((BATCH, T_ALL, RET_DIM), BF16),
        compiler_params=_cparams(2),
        name="retention",
    )(log_g, p, p, p, p, cos, s1, s2, ret_norm)


def _gmlp_kernel(u_ref, v_ref, gain_ref, ws_ref, bst_ref, o_ref):
    u = jax.nn.gelu(u_ref[...].astype(F32))
    v = _rms(jax.nn.gelu(v_ref[...].astype(F32))) * gain_ref[...]
    vb = v.astype(BF16)
    for n in range(ROW_TILE // GMLP_CHUNK):
        r0 = n * GMLP_CHUNK
        for g in range(GMLP_GROUPS):
            c0 = g * LANE
            mixed = (_dot(ws_ref[g], vb[r0:r0 + GMLP_CHUNK, c0:c0 + LANE])
                     + bst_ref[:, g:g + 1])
            o_ref[r0:r0 + GMLP_CHUNK, c0:c0 + LANE] = (
                u[r0:r0 + GMLP_CHUNK, c0:c0 + LANE] * mixed).astype(o_ref.dtype)


def _gmlp(p, gmlp_norm, ws_bf16, bs_t):
    return pl.pallas_call(
        _gmlp_kernel,
        grid=(BATCH, T_ALL // ROW_TILE),
        in_specs=[pl.BlockSpec((None, ROW_TILE, GMLP_DIM), lambda b, i: (b, i, 2)),
                  pl.BlockSpec((None, ROW_TILE, GMLP_DIM), lambda b, i: (b, i, 3)),
                  pl.BlockSpec((1, GMLP_DIM), lambda b, i: (0, 0)),
                  pl.BlockSpec((GMLP_GROUPS, GMLP_CHUNK, GMLP_CHUNK), lambda b, i: (0, 0, 0)),
                  pl.BlockSpec((GMLP_CHUNK, GMLP_GROUPS), lambda b, i: (0, 0))],
        out_specs=pl.BlockSpec((None, ROW_TILE, GMLP_DIM), lambda b, i: (b, i, 0)),
        out_shape=jax.ShapeDtypeStruct((BATCH, T_ALL, GMLP_DIM), BF16),
        compiler_params=_cparams(2),
        name="gmlp",
    )(p, p, gmlp_norm, ws_bf16, bs_t)


def _na_kernel(q_ref, k_ref, v_ref, bias_ref, o_ref):
    i = pl.program_id(2)
    w0 = jnp.clip(i * NA_QROWS - WIN_ROWS // 2, 0, GRID_ROWS - NA_KROWS)
    start = pl.multiple_of(w0 * GRID_W, GRID_W)
    scale = HEAD_DIM ** -0.5
    for hp in range(NA_HEADS_PER_STEP):
        hl = pl.ds(hp * LANE, LANE)
        q = q_ref[:, hl]
        s_win = _dot_nt(q, k_ref[pl.ds(start, NA_KT), hl]) * scale + bias_ref[hp]
        s_ctx = _dot_nt(q, k_ref[pl.ds(T_LAT, T_CTX), hl]) * scale
        m = jnp.maximum(jnp.max(s_win, axis=-1, keepdims=True),
                        jnp.max(s_ctx, axis=-1, keepdims=True))
        p_win = jnp.exp(s_win - m)
        p_ctx = jnp.exp(s_ctx - m)
        denom = jnp.sum(p_win, axis=-1, keepdims=True) + jnp.sum(p_ctx, axis=-1, keepdims=True)
        o = (_dot(p_win.astype(BF16), v_ref[pl.ds(start, NA_KT), hl])
             + _dot(p_ctx.astype(BF16), v_ref[pl.ds(T_LAT, T_CTX), hl]))
        o_ref[:, hl] = (o / denom).astype(o_ref.dtype)


def _na_geometry():
    n_blk = GRID_ROWS // NA_QROWS
    dr00 = []
    row_ok = np.zeros((3, NA_QROWS, NA_KROWS), bool)
    for kind, i_rep in enumerate((0, 1, n_blk - 1)):
        r0 = i_rep * NA_QROWS
        w0 = int(np.clip(r0 - WIN_ROWS // 2, 0, GRID_ROWS - NA_KROWS))
        r = r0 + np.arange(NA_QROWS)[:, None]
        kr = w0 + np.arange(NA_KROWS)[None, :]
        rstart = np.clip(r - WIN_ROWS // 2, 0, GRID_ROWS - WIN_ROWS)
        row_ok[kind] = (kr >= rstart) & (kr < rstart + WIN_ROWS)
        dr00.append(w0 - r0 + WIN_ROWS - 1)
    c = np.arange(GRID_W)[:, None]
    kc = np.arange(GRID_W)[None, :]
    cstart = np.clip(c - WIN_COLS // 2, 0, GRID_W - WIN_COLS)
    col_ok = (kc >= cstart) & (kc < cstart + WIN_COLS)
    valid = row_ok[:, :, None, :, None] & col_ok[None, None, :, None, :]
    return dr00, valid.reshape(3, NA_QT, NA_KT)


def _na_bias_table(rpb):
    n_dr, n_dc = 2 * WIN_ROWS - 1, 2 * WIN_COLS - 1
    w = jnp.concatenate([rpb[..., WIN_COLS - 1:], jnp.zeros((NA_HEADS, n_dr, LANE - n_dc), F32),
                         rpb[..., :WIN_COLS - 1]], axis=-1)
    toep = jnp.tile(w, (1, 1, GRID_W))[..., :GRID_W * (LANE - 1)]
    toep = toep.reshape(NA_HEADS, n_dr, GRID_W, LANE - 1)[..., :GRID_W]
    dr00, valid = _na_geometry()
    front = max(0, NA_QROWS - 1 - min(dr00))
    back = max(0, max(dr00) + NA_KROWS - n_dr)
    toep = jnp.pad(toep.transpose(0, 2, 1, 3), ((0, 0), (0, 0), (front, back), (0, 0)))
    kinds = []
    for d0 in dr00:
        rows = [toep[:, :, d0 - rl + front:d0 - rl + front + NA_KROWS] for rl in range(NA_QROWS)]
        kinds.append(jnp.stack(rows, axis=1))
    tab = jnp.stack(kinds, axis=0).reshape(3, NA_HEADS, NA_QT, NA_KT)
    return jnp.where(jnp.asarray(valid)[:, None], tab, NEG_BIG).astype(BF16)


def _na(p, bias):
    n_blk = T_LAT // NA_QT
    nh = NA_HEADS_PER_STEP
    hw = nh * LANE
    k0, v0 = 20 * LANE // hw, 28 * LANE // hw
    kind = lambda i: jnp.where(i == 0, 0, jnp.where(i == n_blk - 1, 2, 1))
    return pl.pallas_call(
        _na_kernel,
        grid=(BATCH, NA_HEADS // nh, n_blk),
        in_specs=[pl.BlockSpec((None, NA_QT, hw), lambda b, h, i: (b, i, h)),
                  pl.BlockSpec((None, T_ALL, hw), lambda b, h, i: (b, 0, k0 + h)),
                  pl.BlockSpec((None, T_ALL, hw), lambda b, h, i: (b, 0, v0 + h)),
                  pl.BlockSpec((None, nh, NA_QT, NA_KT), lambda b, h, i: (kind(i), h, 0, 0))],
        out_specs=pl.BlockSpec((None, NA_QT, hw), lambda b, h, i: (b, i, h)),
        out_shape=jax.ShapeDtypeStruct((BATCH, T_LAT, NA_DIM), BF16),
        compiler_params=_cparams(3),
        name="nbr_attention",
    )(p, p, p, bias)


def _gla_kernel(q_ref, k_ref, v_ref, g_ref, lr_ref, whi_ref, wlo_ref, bup_ref, gain_ref,
                o_ref, acc_ref):
    r, c = GLA_SUPER, GLA_SUB
    n_sub = r // c
    dv2 = 2 * GLA_DV
    qscale = GLA_DK ** -0.5
    row = lax.broadcasted_iota(jnp.int32, (r, r), 0)
    col = lax.broadcasted_iota(jnp.int32, (r, r), 1)
    sub_shift = c.bit_length() - 1
    same = (row >> sub_shift) == (col >> sub_shift)
    blk_ones = jnp.where(same, 1.0, 0.0).astype(BF16)
    second_head_lane = lax.broadcasted_iota(jnp.int32, (1, LANE), 1) >= GLA_DK
    st_mask = ((lax.broadcasted_iota(jnp.int32, (dv2, LANE), 0) >= GLA_DV)
               == (lax.broadcasted_iota(jnp.int32, (dv2, LANE), 1) >= GLA_DK))

    for d in range(2):
        tri = same & ((row >= col) if d == 0 else (col >= row))
        tri_b = jnp.where(tri, 1.0, 0.0).astype(BF16)

        def pair_step(hp, start, lr, st, d=d, tri=tri, tri_b=tri_b):
            kl = pl.ds(hp * LANE, LANE)
            vl = pl.ds(hp * dv2, dv2)
            z = _dot(lr, whi_ref[d, :, kl]) + _dot(lr, wlo_ref[d, :, kl]) + bup_ref[d, :, kl]
            la = (jnp.minimum(z, 0.0) - jnp.log(1.0 + jnp.exp(-jnp.abs(z)))) * (1.0 / GLA_TAU)
            hi, lo = _split2(la)
            b = _dot(tri_b, hi) + _dot(tri_b, lo)
            bt = _dot(blk_ones, hi) + _dot(blk_ones, lo)
            q = q_ref[pl.ds(start, r), kl].astype(F32) * qscale
            k = k_ref[pl.ds(start, r), kl].astype(F32)
            v = v_ref[pl.ds(start, r), vl]
            qi = q * jnp.exp(b)
            qib = qi.astype(BF16)
            ki = (k * jnp.exp(-b)).astype(BF16)
            ko = (k * jnp.exp(bt - b)).astype(BF16)
            dec = jnp.exp(bt)
            outs = []
            for j in range(2):
                qm = jnp.where(second_head_lane == (j == 1), qi, 0.0).astype(BF16)
                att = jnp.where(tri, _dot_nt(qm, ki), 0.0)
                outs.append(_dot(att.astype(BF16), v[:, j * GLA_DV:(j + 1) * GLA_DV]))
            o_intra = jnp.concatenate(outs, axis=1)
            parts = [None] * n_sub
            for s in (range(n_sub) if d == 0 else reversed(range(n_sub))):
                lo_r, hi_r = s * c, (s + 1) * c
                parts[s] = o_intra[lo_r:hi_r] + _dot_nt(qib[lo_r:hi_r], st.astype(BF16))
                st = (st * dec[lo_r:lo_r + 1, :]
                      + jnp.where(st_mask, _dot_tn(v[lo_r:hi_r], ko[lo_r:hi_r]), 0.0))
            o = jnp.concatenate(parts, axis=0)
            if d == 0:
                acc_ref[pl.ds(start, r), vl] = o
            else:
                o = acc_ref[pl.ds(start, r), vl] + o
                gate = g_ref[pl.ds(start, r), vl].astype(F32)
                on = (jnp.concatenate([_rms(o[:, :GLA_DV]), _rms(o[:, GLA_DV:])], axis=1)
                      * gain_ref[:, vl])
                o_ref[pl.ds(start, r), vl] = (on * _silu(gate)).astype(o_ref.dtype)
            return st

        def step(i, sts, base, n, d=d):
            idx = i if d == 0 else n - 1 - i
            start = pl.multiple_of(base + idx * r, r)
            lr = lr_ref[pl.ds(start, r), :]
            return tuple(pair_step(hp, start, lr, sts[hp]) for hp in range(GLA_PAIRS))

        sts = tuple(jnp.zeros((dv2, LANE), F32) for _ in range(GLA_PAIRS))
        sts = lax.fori_loop(0, T_CTX // r, functools.partial(step, base=T_LAT, n=T_CTX // r), sts)
        lax.fori_loop(0, T_LAT // r, functools.partial(step, base=0, n=T_LAT // r), sts)


def _gla(p, lr, w_hi, w_lo, b_up, gla_norm):
    kw = GLA_PAIRS * LANE
    vw = GLA_PAIRS * 2 * GLA_DV
    q0, k0 = 8 * LANE // kw, 36 * LANE // kw
    v0, g0 = 40 * LANE // vw, 12 * LANE // vw
    return pl.pallas_call(
        _gla_kernel,
        grid=(BATCH, GLA_HEADS // (2 * GLA_PAIRS)),
        in_specs=[pl.BlockSpec((None, T_ALL, kw), lambda b, j: (b, 0, q0 + j)),
                  pl.BlockSpec((None, T_ALL, kw), lambda b, j: (b, 0, k0 + j)),
                  pl.BlockSpec((None, T_ALL, vw), lambda b, j: (b, 0, v0 + j)),
                  pl.BlockSpec((None, T_ALL, vw), lambda b, j: (b, 0, g0 + j)),
                  pl.BlockSpec((None, T_ALL, LANE), lambda b, j: (b, 0, 0)),
                  pl.BlockSpec((2, LANE, kw), lambda b, j: (0, 0, j)),
                  pl.BlockSpec((2, LANE, kw), lambda b, j: (0, 0, j)),
                  pl.BlockSpec((2, 1, kw), lambda b, j: (0, 0, j)),
                  pl.BlockSpec((1, vw), lambda b, j: (0, j))],
        out_specs=pl.BlockSpec((None, T_ALL, vw), lambda b, j: (b, 0, j)),
        out_shape=jax.ShapeDtypeStruct((BATCH, T_ALL, GLA_V_DIM), BF16),
        scratch_shapes=[pltpu.VMEM((T_ALL, vw), F32)],
        compiler_params=_cparams(2),
        name="gla",
    )(p, p, p, p, lr, w_hi, w_lo, b_up, gla_norm)


def _out_kernel(a1_ref, a2_ref, w1_ref, w2_ref, x_ref, gl_ref, gc_ref, gain_ref,
                scl_ref, scc_ref, shl_ref, shc_ref, rw_ref, xo_ref, h_ref, lg_ref, *, tile,
                has_ctx):
    half = tile // 2
    for r0 in (0, half):
        rows = pl.ds(r0, half)
        sel = functools.partial(_region_select, tile, row0=r0, n_rows=half, has_ctx=has_ctx)
        y = _dot(a1_ref[rows, :], w1_ref[...]) + _dot(a2_ref[rows, :], w2_ref[...])
        xn = x_ref[rows, :] + sel(gl_ref, gc_ref) * y
        xo_ref[rows, :] = xn
        h = _rms(xn) * gain_ref[...] * (1.0 + sel(scl_ref, scc_ref)) + sel(shl_ref, shc_ref)
        h_hi = h.astype(BF16)
        h_ref[rows, :] = h_hi
        h_lo = (h - h_hi.astype(F32)).astype(BF16)
        lg_ref[rows, :] = _dot(h_hi, rw_ref[...]) + _dot(h_lo, rw_ref[...])


def _out_proj(a1, a2, w_out_bf16, x, mod_lat, mod_ctx, norm_gain, rw_split, tile, n_tiles):
    half = w_out_bf16.shape[0] // 2
    t_out = n_tiles * tile
    row = lambda width: pl.BlockSpec((None, tile, width), lambda b, i: (b, i, 0))
    return pl.pallas_call(
        functools.partial(_out_kernel, tile=tile, has_ctx=t_out > T_LAT),
        grid=(BATCH, n_tiles),
        in_specs=[row(half), row(half),
                  pl.BlockSpec((half, D_MODEL), lambda b, i: (0, 0)),
                  pl.BlockSpec((half, D_MODEL), lambda b, i: (1, 0)),
                  row(D_MODEL), _lat_spec(2), _ctx_spec(2),
                  pl.BlockSpec((1, D_MODEL), lambda b, i: (0, 0)),
                  _lat_spec(4), _ctx_spec(4), _lat_spec(3), _ctx_spec(3),
                  pl.BlockSpec((D_MODEL, 2 * N_EXPERTS), lambda b, i: (0, 0))],
        out_specs=[row(D_MODEL), row(D_MODEL), row(2 * N_EXPERTS)],
        out_shape=[jax.ShapeDtypeStruct((BATCH, t_out, D_MODEL), F32),
                   jax.ShapeDtypeStruct((BATCH, t_out, D_MODEL), BF16),
                   jax.ShapeDtypeStruct((BATCH, t_out, 2 * N_EXPERTS), F32)],
        compiler_params=_cparams(2),
        name="out_proj",
    )(a1, a2, w_out_bf16, w_out_bf16, x, mod_lat, mod_ctx, norm_gain,
      mod_lat, mod_ctx, mod_lat, mod_ctx, rw_split)


def _moe_up_kernel(x_ref, wg_ref, wu_ref, o_ref):
    x = x_ref[...]
    a = _dot(x, wg_ref[...].astype(BF16))
    u = _dot(x, wu_ref[...].astype(BF16))
    o_ref[...] = (_silu(a) * u).astype(o_ref.dtype)


def _moe_down_kernel(h_ref, wd_ref, g_ref, o_ref):
    o_ref[...] = (_dot(h_ref[...], wd_ref[...].astype(BF16)) * g_ref[...]).astype(o_ref.dtype)


def _moe(xs, gates, w_gate, w_up, w_down, layer):
    e, m, _ = xs.shape
    tn = MOE_TN
    wspec = pl.BlockSpec((None, None, D_MODEL, tn), lambda e, f: (layer, e, 0, f))
    hmid = pl.pallas_call(
        _moe_up_kernel,
        grid=(e, D_EXPERT // tn),
        in_specs=[pl.BlockSpec((None, m, D_MODEL), lambda e, f: (e, 0, 0)), wspec, wspec],
        out_specs=pl.BlockSpec((None, m, tn), lambda e, f: (e, 0, f)),
        out_shape=jax.ShapeDtypeStruct((e, m, D_EXPERT), BF16),
        compiler_params=_cparams(2),
        name="moe_up",
    )(xs, w_gate, w_up)
    return pl.pallas_call(
        _moe_down_kernel,
        grid=(e, D_MODEL // tn),
        in_specs=[pl.BlockSpec((None, m, D_EXPERT), lambda e, f: (e, 0, 0)),
                  pl.BlockSpec((None, None, D_EXPERT, tn), lambda e, f: (layer, e, 0, f)),
                  pl.BlockSpec((None, m, 1), lambda e, f: (e, 0, 0))],
        out_specs=pl.BlockSpec((None, m, tn), lambda e, f: (e, 0, f)),
        out_shape=jax.ShapeDtypeStruct((e, m, D_MODEL), BF16),
        compiler_params=_cparams(2),
        name="moe_down",
    )(hmid, w_down, gates)


def _combine_moe(span_ref, pair_tok_ref, ys_hbm, acc_ref, buf_ref, sem, *, tile, n_pairs):
    b, i = pl.program_id(0), pl.program_id(1)
    n_tiles = pl.num_programs(1)
    step = b * n_tiles + i
    n_steps = pl.num_programs(0) * n_tiles
    w = PAIR_WIN
    shift = w.bit_length() - 1
    last_start = n_pairs // w - PAIR_CHUNK

    def first_window(s):
        return jnp.minimum(span_ref[2 * s] >> shift, last_start)

    def chunk_copy(sample, win, slot):
        return pltpu.make_async_copy(
            ys_hbm.at[pl.ds(sample * n_pairs + win * w, PAIR_CHUNK * w), :],
            buf_ref.at[slot], sem.at[slot])

    def window_copy(win, slot):
        return pltpu.make_async_copy(ys_hbm.at[pl.ds(b * n_pairs + win * w, w), :],
                                     buf_ref.at[slot, pl.ds(0, w), :], sem.at[slot])

    slot = step & 1
    w0 = first_window(step)

    @pl.when(step == 0)
    def _():
        chunk_copy(b, w0, slot).start()

    @pl.when(step + 1 < n_steps)
    def _():
        next_sample = jnp.where(i + 1 < n_tiles, b, b + 1)
        chunk_copy(next_sample, first_window(step + 1), 1 - slot).start()

    chunk_copy(b, w0, slot).wait()
    tok = i * tile + lax.broadcasted_iota(jnp.int32, (tile, 1), 0)

    def select(win):
        return jnp.where(tok == pair_tok_ref[pl.ds(win, 1), :], 1.0, 0.0).astype(BF16)

    sel = jnp.concatenate([select(w0 + c) for c in range(PAIR_CHUNK)], axis=1)
    acc_ref[...] = _dot(sel, buf_ref[slot])

    hi = span_ref[2 * step + 1]
    end_win = jnp.where(hi > 0, ((hi - 1) >> shift) + 1, 0)

    def body(win, carry):
        cp = window_copy(win, slot)
        cp.start()
        cp.wait()
        acc_ref[...] += _dot(select(win), buf_ref[slot, pl.ds(0, w), :])
        return carry

    lax.fori_loop(w0 + PAIR_CHUNK, end_win, body, 0)
    return acc_ref[...]


def _res_mid_kernel(span_ref, x_ref, pair_tok_ref, ys_hbm, gl_ref, gc_ref, gain_ref,
                    scl_ref, scc_ref, shl_ref, shc_ref, xo_ref, h_ref, acc_ref, buf_ref, sem,
                    *, tile, n_pairs):
    moe = _combine_moe(span_ref, pair_tok_ref, ys_hbm, acc_ref, buf_ref, sem,
                       tile=tile, n_pairs=n_pairs)
    xn = x_ref[...] + _region_select(tile, gl_ref, gc_ref) * moe
    xo_ref[...] = xn
    h_ref[...] = (_rms(xn) * gain_ref[...] * (1.0 + _region_select(tile, scl_ref, scc_ref))
                  + _region_select(tile, shl_ref, shc_ref)).astype(h_ref.dtype)


def _res_final_kernel(span_ref, x_ref, pair_tok_ref, ys_hbm, g_ref, gain_ref, o_ref,
                      acc_ref, buf_ref, sem, *, tile, n_pairs):
    moe = _combine_moe(span_ref, pair_tok_ref, ys_hbm, acc_ref, buf_ref, sem,
                       tile=tile, n_pairs=n_pairs)
    o_ref[...] = _rms(x_ref[...] + g_ref[...] * moe) * gain_ref[...]


def _combine_call(kernel_fn, name, tile, t_rows, spans, x, pair_tok, ys, params, param_specs,
                  out_dtypes):
    n_pairs = pair_tok.shape[1] * PAIR_WIN
    row = pl.BlockSpec((None, tile, D_MODEL), lambda b, i, *_: (b, i, 0))
    return pl.pallas_call(
        functools.partial(kernel_fn, tile=tile, n_pairs=n_pairs),
        grid_spec=pltpu.PrefetchScalarGridSpec(
            num_scalar_prefetch=1,
            grid=(BATCH, t_rows // tile),
            in_specs=[row,
                      pl.BlockSpec((None,) + pair_tok.shape[1:], lambda b, i, *_: (b, 0, 0)),
                      pl.BlockSpec(memory_space=pl.ANY)] + param_specs,
            out_specs=[row] * len(out_dtypes),
            scratch_shapes=[pltpu.VMEM((tile, D_MODEL), F32),
                            pltpu.VMEM((2, PAIR_CHUNK * PAIR_WIN, D_MODEL), BF16),
                            pltpu.SemaphoreType.DMA((2,))]),
        out_shape=[jax.ShapeDtypeStruct((BATCH, t_rows, D_MODEL), dt) for dt in out_dtypes],
        compiler_params=_cparams(2),
        name=name,
    )(spans, x, pair_tok, ys, *params)


def _res_mid(x, spans, pair_tok, ys, mod_lat, mod_ctx, next_lat, next_ctx, norm_gain_next):
    specs = [_lat_spec(5), _ctx_spec(5), pl.BlockSpec((1, D_MODEL), lambda b, i, *_: (0, 0)),
             _lat_spec(1), _ctx_spec(1), _lat_spec(0), _ctx_spec(0)]
    params = (mod_lat, mod_ctx, norm_gain_next, next_lat, next_ctx, next_lat, next_ctx)
    return _combine_call(_res_mid_kernel, "ffn_combine_residual_prenorm", COMB_TILE, T_ALL, spans,
                         x, pair_tok, ys, params, specs, (F32, BF16))


def _res_final(x, spans, pair_tok, ys, mod_lat, norm_final):
    specs = [_lat_spec(5), pl.BlockSpec((1, D_MODEL), lambda b, i, *_: (0, 0))]
    return _combine_call(_res_final_kernel, "ffn_combine_residual_final_norm", COMB_TILE, T_LAT,
                         spans, x, pair_tok, ys, (mod_lat, norm_final), specs, (F32,))[0]


def _rope_tables():
    half = HEAD_DIM // 2
    nf = half // 2
    t = np.arange(T_LAT)
    rows, cols = t // GRID_W, t % GRID_W
    freq = ROPE_BASE ** (-jnp.arange(nf, dtype=F32) / nf)
    ang_r = jnp.asarray(rows, F32)[:, None] * freq
    ang_c = jnp.asarray(cols, F32)[:, None] * freq
    zeros = jnp.zeros_like(ang_r)
    cos = jnp.concatenate([jnp.cos(ang_r), jnp.cos(ang_r), jnp.cos(ang_c), jnp.cos(ang_c)], axis=1)
    s1 = jnp.concatenate([-jnp.sin(ang_r), zeros, -jnp.sin(ang_c), zeros], axis=1)
    s2 = jnp.concatenate([zeros, jnp.sin(ang_r), zeros, jnp.sin(ang_c)], axis=1)
    return cos, s1, s2


def _router_split(rw):
    hi = rw.astype(BF16)
    lo = (rw - hi.astype(F32)).astype(BF16)
    return jnp.concatenate([hi, lo], axis=1)


def _route(logits, t0, t1, cap):
    aff = jax.nn.softmax(logits[:, t0:t1], axis=-1)
    g, idx = lax.top_k(aff.transpose(0, 2, 1), cap)
    return g, idx + t0


def _expert_ffn(h, logits2, w_gate, w_up, w_down, layer, with_ctx, tile):
    logits = logits2[..., :N_EXPERTS] + logits2[..., N_EXPERTS:]
    g, idx = _route(logits, 0, T_LAT, EC_CAPACITY * T_LAT // N_EXPERTS)
    if with_ctx:
        gc, idxc = _route(logits, T_LAT, T_ALL, EC_CAPACITY * T_CTX // N_EXPERTS)
        g = jnp.concatenate([g, gc], axis=-1)
        idx = jnp.concatenate([idx, idxc], axis=-1)
    cap = idx.shape[-1]
    t_rows = h.shape[1]
    n_pairs = N_EXPERTS * cap
    assert n_pairs % PAIR_WIN == 0
    flat = (idx + jnp.arange(BATCH)[:, None, None] * t_rows).transpose(1, 0, 2).reshape(N_EXPERTS, -1)
    gates = g.transpose(1, 0, 2).reshape(N_EXPERTS, BATCH * cap, 1)
    xs = h.reshape(BATCH * t_rows, D_MODEL)[flat]
    y = _moe(xs, gates, w_gate, w_up, w_down, layer)
    src = (jnp.arange(N_EXPERTS)[None, :, None] * (BATCH * cap)
           + jnp.arange(BATCH)[:, None, None] * cap + jnp.arange(cap)[None, None, :])
    pair_tok, src = lax.sort((idx.reshape(BATCH, n_pairs), src.reshape(BATCH, n_pairs)),
                             dimension=1, num_keys=1)
    ys = y.reshape(-1, D_MODEL)[src.reshape(-1)]
    bounds = jnp.arange(0, t_rows + tile, tile)
    below = jnp.sum(pair_tok[:, :, None] < bounds[None, None, :], axis=1)
    spans = jnp.stack([below[:, :-1], below[:, 1:]], axis=-1).reshape(-1).astype(jnp.int32)
    return spans, pair_tok.reshape(BATCH, n_pairs // PAIR_WIN, PAIR_WIN), ys


def kernel(x, c, ctx, c_ctx, ada_w, ada_b, norm_mix, norm_ffn, norm_final, ev_w_in, ev_w_out,
           ret_gamma_logit, ret_norm, gmlp_norm, gmlp_ws, gmlp_bs, od_w_in, od_w_out, na_rpb,
           gla_w_up, gla_b_up, gla_norm, router_w, moe_w_gate, moe_w_up, moe_w_down):
    assert x.shape == (BATCH, T_LAT, D_MODEL) and ctx.shape == (BATCH, T_CTX, D_MODEL)
    assert ada_w.shape[0] == 2 and ev_w_in.shape[0] == 1 and od_w_in.shape[0] == 1

    cond = jnp.concatenate([jax.nn.silu(c), jax.nn.silu(c_ctx)[None],
                            jnp.zeros((11, D_MODEL), F32)], axis=0).astype(BF16)
    mod_lat, mod_ctx = [], []
    for l in range(2):
        m = _mod_matmul(cond, ada_w, l) + ada_b[l]
        m = m[:BATCH + 1].reshape(BATCH + 1, 6, 1, D_MODEL)
        mod_lat.append(m[:BATCH])
        mod_ctx.append(m[BATCH])

    xa = jnp.concatenate([x, ctx], axis=1)
    cos, s1, s2 = _rope_tables()

    h = _modnorm(xa, norm_mix[0:1], mod_lat[0], mod_ctx[0], 1, 0)
    p = _matmul(h.reshape(BATCH * T_ALL, D_MODEL), ev_w_in[0].astype(BF16), 2176, 1024, BF16)
    p = p.reshape(BATCH, T_ALL, EVEN_COLS)
    log_g = jax.nn.log_sigmoid(ret_gamma_logit[0].astype(F32))
    ret = _retention(p, log_g, cos, s1, s2, ret_norm[0:1])
    gm = _gmlp(p, gmlp_norm[0:1], gmlp_ws[0].astype(BF16), gmlp_bs[0].T)
    xa, h2, logits = _out_proj(ret, gm, ev_w_out[0].astype(BF16), xa, mod_lat[0], mod_ctx[0],
                               norm_ffn[0:1], _router_split(router_w[0]), MIX_TILE,
                               T_ALL // MIX_TILE)
    spans, pair_tok, ys = _expert_ffn(h2, logits, moe_w_gate, moe_w_up, moe_w_down, 0, True,
                                      COMB_TILE)
    xa, h = _res_mid(xa, spans, pair_tok, ys, mod_lat[0], mod_ctx[0], mod_lat[1], mod_ctx[1],
                     norm_mix[1:2])

    h2d = h.reshape(BATCH * T_ALL, D_MODEL)
    p = _matmul(h2d, od_w_in[0][:, :ODD_MAIN_COLS].astype(BF16), 1024, 1024, BF16)
    p = p.reshape(BATCH, T_ALL, ODD_MAIN_COLS)
    w_lr = jnp.pad(od_w_in[0][:, ODD_MAIN_COLS:], ((0, 0), (0, LANE - 2 * GLA_GATE_RANK)))
    lr = _matmul(h2d, w_lr.astype(BF16), 2176, LANE, BF16).reshape(BATCH, T_ALL, LANE)
    o_na = _na(p, _na_bias_table(na_rpb[0]))
    w_up_full = jnp.zeros((2, LANE, GLA_QK_DIM), F32)
    for d in range(2):
        w_up_full = w_up_full.at[d, d * GLA_GATE_RANK:(d + 1) * GLA_GATE_RANK].set(gla_w_up[0, d])
    w_hi = w_up_full.astype(BF16)
    w_lo = (w_up_full - w_hi.astype(F32)).astype(BF16)
    gla = _gla(p, lr, w_hi, w_lo, gla_b_up[0][:, None, :], gla_norm[0:1])
    xl, h2, logits = _out_proj(o_na, gla, od_w_out[0].astype(BF16), xa, mod_lat[1], mod_ctx[1],
                               norm_ffn[1:2], _router_split(router_w[1]), LAT_TILE,
                               T_LAT // LAT_TILE)
    spans, pair_tok, ys = _expert_ffn(h2, logits, moe_w_gate, moe_w_up, moe_w_down, 1, False,
                                      COMB_TILE)
    return _res_final(xl, spans, pair_tok, ys, mod_lat[1], norm_final[None])
```

```python
import functools

import numpy as np
import jax
import jax.numpy as jnp
from jax import lax
from jax.experimental import pallas as pl
from jax.experimental.pallas import tpu as pltpu

D_MODEL = 2048
BATCH = 4
T_LAT = 4096
T_CTX = 256
T_ALL = T_LAT + T_CTX
GRID_W = 64
GRID_ROWS = T_LAT // GRID_W
HEAD_DIM = 128
RET_HEADS = 8
RET_DIM = 1024
GMLP_DIM = 1024
GMLP_GROUPS = 8
GMLP_CHUNK = 128
NA_HEADS = 8
NA_DIM = 1024
WIN_ROWS = 8
WIN_COLS = 16
GLA_HEADS = 8
GLA_DK = 64
GLA_DV = 128
GLA_QK_DIM = 512
GLA_V_DIM = 1024
GLA_GATE_RANK = 16
GLA_TAU = 16.0
N_EXPERTS = 16
D_EXPERT = 2048
EC_CAPACITY = 2
ROPE_BASE = 10000.0
RMS_EPS = 1e-6
EVEN_COLS = 6144
ODD_COLS = 6176
ODD_MAIN_COLS = 6144
LANE = 128

ROW_TILE = 256
MIX_TILE = 544
LAT_TILE = 512
RET_CHUNK = 256
GLA_SUB = 64
GLA_SUPER = 256
GLA_PAIRS = 2
NA_HEADS_PER_STEP = 4
RET_HEADS_PER_STEP = 2
NA_QROWS = 4
NA_QT = NA_QROWS * GRID_W
NA_KROWS = NA_QROWS + WIN_ROWS - 1
NA_KT = NA_KROWS * GRID_W
NEG_BIG = -1e30
MOE_TN = 512
PAIR_WIN = 256
PAIR_CHUNK = 3
COMB_TILE = 256
VMEM_LIMIT = 56 * 1024 * 1024

F32 = jnp.float32
BF16 = jnp.bfloat16


def _cparams(n_axes):
    return pltpu.CompilerParams(dimension_semantics=("arbitrary",) * n_axes,
                                vmem_limit_bytes=VMEM_LIMIT)


def _dot(a, b):
    return jnp.dot(a, b, preferred_element_type=F32)


def _dot_nt(a, b):
    return lax.dot_general(a, b, (((1,), (1,)), ((), ())), preferred_element_type=F32)


def _dot_tn(a, b):
    return lax.dot_general(a, b, (((0,), (0,)), ((), ())), preferred_element_type=F32)


def _silu(x):
    return x * jax.nn.sigmoid(x)


def _rms(x):
    return x * lax.rsqrt(jnp.mean(x * x, axis=-1, keepdims=True) + RMS_EPS)


def _split2(x):
    hi = x.astype(BF16)
    lo = (x - hi.astype(F32)).astype(BF16)
    return hi, lo


def _mm_kernel(a_ref, w_ref, o_ref):
    o_ref[...] = _dot(a_ref[...], w_ref[...]).astype(o_ref.dtype)


def _matmul(a, w, tm, tn, out_dtype):
    m, k = a.shape
    n = w.shape[1]
    return pl.pallas_call(
        _mm_kernel,
        grid=(n // tn, m // tm),
        in_specs=[pl.BlockSpec((tm, k), lambda j, i: (i, 0)),
                  pl.BlockSpec((k, tn), lambda j, i: (0, j))],
        out_specs=pl.BlockSpec((tm, tn), lambda j, i: (i, j)),
        out_shape=jax.ShapeDtypeStruct((m, n), out_dtype),
        compiler_params=_cparams(2),
        name="matmul",
    )(a, w)


def _mod_mm_kernel(a_ref, w_ref, o_ref):
    o_ref[...] = _dot(a_ref[...], w_ref[...].astype(BF16))


def _mod_matmul(cond, ada_w, layer):
    m, k = cond.shape
    n = ada_w.shape[2]
    tn = 1024
    return pl.pallas_call(
        _mod_mm_kernel,
        grid=(n // tn,),
        in_specs=[pl.BlockSpec((m, k), lambda j: (0, 0)),
                  pl.BlockSpec((None, k, tn), lambda j: (layer, 0, j))],
        out_specs=pl.BlockSpec((m, tn), lambda j: (0, j)),
        out_shape=jax.ShapeDtypeStruct((m, n), F32),
        compiler_params=_cparams(1),
        name="mod_matmul",
    )(cond, ada_w)


def _lat_spec(which):
    return pl.BlockSpec((None, None, 1, D_MODEL), lambda b, i, *_: (b, which, 0, 0))


def _ctx_spec(which):
    return pl.BlockSpec((None, 1, D_MODEL), lambda b, i, *_: (which, 0, 0))


def _region_select(tile, lat_ref, ctx_ref, row0=0, n_rows=None, has_ctx=True):
    if not has_ctx:
        return lat_ref[...]
    n_rows = tile if n_rows is None else n_rows
    rows = pl.program_id(1) * tile + row0 + lax.broadcasted_iota(jnp.int32, (n_rows, 1), 0)
    return jnp.where(rows >= T_LAT, ctx_ref[...], lat_ref[...])


def _modnorm_kernel(x_ref, ctx_ref, gain_ref, scl_ref, scc_ref, shl_ref, shc_ref, xa_ref, o_ref,
                    *, tile):
    is_ctx = pl.program_id(1) >= T_LAT // tile
    x = jnp.where(is_ctx, ctx_ref[...], x_ref[...])
    xa_ref[...] = x
    sc = _region_select(tile, scl_ref, scc_ref)
    sh = _region_select(tile, shl_ref, shc_ref)
    o_ref[...] = (_rms(x) * gain_ref[...] * (1.0 + sc) + sh).astype(o_ref.dtype)


def _modnorm(x, ctx, gain, mod_lat, mod_ctx, sc_i, sh_i):
    tile = T_CTX
    n_lat = T_LAT // tile
    row = pl.BlockSpec((None, tile, D_MODEL), lambda b, i: (b, i, 0))
    return pl.pallas_call(
        functools.partial(_modnorm_kernel, tile=tile),
        grid=(BATCH, T_ALL // tile),
        in_specs=[pl.BlockSpec((None, tile, D_MODEL), lambda b, i: (b, jnp.minimum(i, n_lat - 1), 0)),
                  pl.BlockSpec((None, tile, D_MODEL), lambda b, i: (b, jnp.maximum(i - n_lat, 0), 0)),
                  pl.BlockSpec((1, D_MODEL), lambda b, i: (0, 0)),
                  _lat_spec(sc_i), _ctx_spec(sc_i), _lat_spec(sh_i), _ctx_spec(sh_i)],
        out_specs=[row, row],
        out_shape=[jax.ShapeDtypeStruct((BATCH, T_ALL, D_MODEL), F32),
                   jax.ShapeDtypeStruct((BATCH, T_ALL, D_MODEL), BF16)],
        compiler_params=_cparams(2),
        name="modnorm",
    )(x, ctx, gain, mod_lat, mod_ctx, mod_lat, mod_ctx)


def _rotary(z, cos, s1, s2):
    return z * cos + pltpu.roll(z, 96, 1) * s1 + pltpu.roll(z, 32, 1) * s2


def _ret_kernel(lg_ref, q_ref, g_ref, k_ref, v_ref, cos_ref, s1_ref, s2_ref, gain_ref,
                o_ref, acc_ref, qs_ref, ks_ref):
    c = RET_CHUNK
    nh = RET_HEADS_PER_STEP
    scale = HEAD_DIM ** -0.5
    dist = (lax.broadcasted_iota(jnp.int32, (c, c), 0)
            - lax.broadcasted_iota(jnp.int32, (c, c), 1)).astype(F32)
    tcol = lax.broadcasted_iota(jnp.int32, (c, 1), 0).astype(F32)

    def head_consts(hp):
        h = pl.program_id(1) * nh + hp
        la_f, la_b = lg_ref[0, h], lg_ref[1, h]
        dmat = (jnp.where(dist >= 0.0, jnp.exp(la_f * jnp.maximum(dist, 0.0)), 0.0)
                + jnp.where(dist <= 0.0, jnp.exp(la_b * jnp.maximum(-dist, 0.0)), 0.0))
        fwd_dec = (jnp.exp(la_f * (tcol + 1.0)), jnp.exp(la_f * (c - 1.0 - tcol)),
                   jnp.exp(jnp.full((1, LANE), la_f * c, F32)))
        bwd_dec = (jnp.exp(la_b * (c - tcol)), jnp.exp(la_b * tcol),
                   jnp.exp(jnp.full((1, LANE), la_b * c, F32)))
        return dmat, fwd_dec, bwd_dec

    consts = [head_consts(hp) for hp in range(nh)]

    def fwd(i, sts, base, rot):
        start = pl.multiple_of(base + i * c, c)
        rows = pl.ds(start, c)
        if rot:
            cos, s1, s2 = cos_ref[rows, :], s1_ref[rows, :], s2_ref[rows, :]
        out = []
        for hp in range(nh):
            hl = pl.ds(hp * LANE, LANE)
            dmat, (qdec, kdec, cdec), _ = consts[hp]
            q = q_ref[rows, hl].astype(F32) * scale
            k = k_ref[rows, hl].astype(F32)
            v = v_ref[rows, hl]
            if rot:
                q = _rotary(q, cos, s1, s2)
                k = _rotary(k, cos, s1, s2)
            qs_ref[rows, hl] = q
            ks_ref[rows, hl] = k
            att = _dot_nt(q.astype(BF16), k.astype(BF16)) * dmat
            st = sts[hp]
            acc_ref[rows, hl] = (_dot(att.astype(BF16), v)
                                 + _dot_nt((q * qdec).astype(BF16), st.astype(BF16)))
            out.append(cdec * st + _dot_tn(v, (k * kdec).astype(BF16)))
        return tuple(out)

    def bwd(i, sts, base, n):
        start = pl.multiple_of(base + (n - 1 - i) * c, c)
        rows = pl.ds(start, c)
        out = []
        for hp in range(nh):
            hl = pl.ds(hp * LANE, LANE)
            _, _, (qdec, kdec, cdec) = consts[hp]
            q = qs_ref[rows, hl]
            k = ks_ref[rows, hl]
            v = v_ref[rows, hl]
            st = sts[hp]
            o = acc_ref[rows, hl] + _dot_nt((q * qdec).astype(BF16), st.astype(BF16))
            gate = g_ref[rows, hl].astype(F32)
            o_ref[rows, hl] = (_rms(o) * gain_ref[:, hl] * _silu(gate)).astype(o_ref.dtype)
            out.append(cdec * st + _dot_tn(v, (k * kdec).astype(BF16)))
        return tuple(out)

    zero = tuple(jnp.zeros((HEAD_DIM, HEAD_DIM), F32) for _ in range(nh))
    n_ctx, n_lat = T_CTX // c, T_LAT // c
    sts = lax.fori_loop(0, n_ctx, functools.partial(fwd, base=T_LAT, rot=False), zero)
    lax.fori_loop(0, n_lat, functools.partial(fwd, base=0, rot=True), sts)
    sts = lax.fori_loop(0, n_ctx, functools.partial(bwd, base=T_LAT, n=n_ctx), zero)
    lax.fori_loop(0, n_lat, functools.partial(bwd, base=0, n=n_lat), sts)


def _retention(p, log_g, cos, s1, s2, ret_norm):
    nh = RET_HEADS_PER_STEP
    hw = nh * LANE
    col = lambda off: pl.BlockSpec((None, T_ALL, hw),
                                   lambda b, h, lg: (b, 0, off * LANE // hw + h))
    tab = pl.BlockSpec((T_LAT, LANE), lambda b, h, lg: (0, 0))
    return pl.pallas_call(
        _ret_kernel,
        grid_spec=pltpu.PrefetchScalarGridSpec(
            num_scalar_prefetch=1,
            grid=(BATCH, RET_HEADS // nh),
            in_specs=[col(0), col(8), col(32), col(40), tab, tab, tab,
                      pl.BlockSpec((1, hw), lambda b, h, lg: (0, h))],
            out_specs=pl.BlockSpec((None, T_ALL, hw), lambda b, h, lg: (b, 0, h)),
            scratch_shapes=[pltpu.VMEM((T_ALL, hw), F32), pltpu.VMEM((T_ALL, hw), F32),
                            pltpu.VMEM((T_ALL, hw), F32)]),
        out_shape=jax.ShapeDtypeStruct((BATCH, T_ALL, RET_DIM), BF16),
        compiler_params=_cparams(2),
        name="retention",
    )(log_g, p, p, p, p, cos, s1, s2, ret_norm)


def _gmlp_kernel(u_ref, v_ref, gain_ref, ws_ref, bst_ref, o_ref):
    u = jax.nn.gelu(u_ref[...].astype(F32))
    v = _rms(jax.nn.gelu(v_ref[...].astype(F32))) * gain_ref[...]
    vb = v.astype(BF16)
    for n in range(ROW_TILE // GMLP_CHUNK):
        r0 = n * GMLP_CHUNK
        for g in range(GMLP_GROUPS):
            c0 = g * LANE
            mixed = (_dot(ws_ref[g], vb[r0:r0 + GMLP_CHUNK, c0:c0 + LANE])
                     + bst_ref[:, g:g + 1])
            o_ref[r0:r0 + GMLP_CHUNK, c0:c0 + LANE] = (
                u[r0:r0 + GMLP_CHUNK, c0:c0 + LANE] * mixed).astype(o_ref.dtype)


def _gmlp(p, gmlp_norm, ws_bf16, bs_t):
    return pl.pallas_call(
        _gmlp_kernel,
        grid=(BATCH, T_ALL // ROW_TILE),
        in_specs=[pl.BlockSpec((None, ROW_TILE, GMLP_DIM), lambda b, i: (b, i, 2)),
                  pl.BlockSpec((None, ROW_TILE, GMLP_DIM), lambda b, i: (b, i, 3)),
                  pl.BlockSpec((1, GMLP_DIM), lambda b, i: (0, 0)),
                  pl.BlockSpec((GMLP_GROUPS, GMLP_CHUNK, GMLP_CHUNK), lambda b, i: (0, 0, 0)),
                  pl.BlockSpec((GMLP_CHUNK, GMLP_GROUPS), lambda b, i: (0, 0))],
        out_specs=pl.BlockSpec((None, ROW_TILE, GMLP_DIM), lambda b, i: (b, i, 0)),
        out_shape=jax.ShapeDtypeStruct((BATCH, T_ALL, GMLP_DIM), BF16),
        compiler_params=_cparams(2),
        name="gmlp",
    )(p, p, gmlp_norm, ws_bf16, bs_t)


def _na_kernel(q_ref, k_ref, v_ref, bias_ref, o_ref):
    i = pl.program_id(2)
    w0 = jnp.clip(i * NA_QROWS - WIN_ROWS // 2, 0, GRID_ROWS - NA_KROWS)
    start = pl.multiple_of(w0 * GRID_W, GRID_W)
    scale = HEAD_DIM ** -0.5
    for hp in range(NA_HEADS_PER_STEP):
        hl = pl.ds(hp * LANE, LANE)
        q = q_ref[:, hl]
        s_win = _dot_nt(q, k_ref[pl.ds(start, NA_KT), hl]) * scale + bias_ref[hp]
        s_ctx = _dot_nt(q, k_ref[pl.ds(T_LAT, T_CTX), hl]) * scale
        m = jnp.maximum(jnp.max(s_win, axis=-1, keepdims=True),
                        jnp.max(s_ctx, axis=-1, keepdims=True))
        p_win = jnp.exp(s_win - m)
        p_ctx = jnp.exp(s_ctx - m)
        denom = jnp.sum(p_win, axis=-1, keepdims=True) + jnp.sum(p_ctx, axis=-1, keepdims=True)
        o = (_dot(p_win.astype(BF16), v_ref[pl.ds(start, NA_KT), hl])
             + _dot(p_ctx.astype(BF16), v_ref[pl.ds(T_LAT, T_CTX), hl]))
        o_ref[:, hl] = (o / denom).astype(o_ref.dtype)


def _na_geometry():
    n_blk = GRID_ROWS // NA_QROWS
    dr00 = []
    row_ok = np.zeros((3, NA_QROWS, NA_KROWS), bool)
    for kind, i_rep in enumerate((0, 1, n_blk - 1)):
        r0 = i_rep * NA_QROWS
        w0 = int(np.clip(r0 - WIN_ROWS // 2, 0, GRID_ROWS - NA_KROWS))
        r = r0 + np.arange(NA_QROWS)[:, None]
        kr = w0 + np.arange(NA_KROWS)[None, :]
        rstart = np.clip(r - WIN_ROWS // 2, 0, GRID_ROWS - WIN_ROWS)
        row_ok[kind] = (kr >= rstart) & (kr < rstart + WIN_ROWS)
        dr00.append(w0 - r0 + WIN_ROWS - 1)
    c = np.arange(GRID_W)[:, None]
    kc = np.arange(GRID_W)[None, :]
    cstart = np.clip(c - WIN_COLS // 2, 0, GRID_W - WIN_COLS)
    col_ok = (kc >= cstart) & (kc < cstart + WIN_COLS)
    valid = row_ok[:, :, None, :, None] & col_ok[None, None, :, None, :]
    return dr00, valid.reshape(3, NA_QT, NA_KT)


def _na_bias_table(rpb):
    n_dr, n_dc = 2 * WIN_ROWS - 1, 2 * WIN_COLS - 1
    w = jnp.concatenate([rpb[..., WIN_COLS - 1:], jnp.zeros((NA_HEADS, n_dr, LANE - n_dc), F32),
                         rpb[..., :WIN_COLS - 1]], axis=-1)
    toep = jnp.tile(w, (1, 1, GRID_W))[..., :GRID_W * (LANE - 1)]
    toep = toep.reshape(NA_HEADS, n_dr, GRID_W, LANE - 1)[..., :GRID_W]
    dr00, valid = _na_geometry()
    front = max(0, NA_QROWS - 1 - min(dr00))
    back = max(0, max(dr00) + NA_KROWS - n_dr)
    toep = jnp.pad(toep.transpose(0, 2, 1, 3), ((0, 0), (0, 0), (front, back), (0, 0)))
    toep = toep.astype(BF16).reshape(NA_HEADS, GRID_W, -1)
    kinds = []
    for d0 in dr00:
        starts = [(d0 - rl + front) * GRID_W for rl in range(NA_QROWS)]
        kinds.append(jnp.stack([toep[:, :, s:s + NA_KT] for s in starts], axis=1))
    tab = jnp.stack(kinds, axis=0).reshape(3, NA_HEADS, NA_QT, NA_KT)
    return jnp.where(jnp.asarray(valid)[:, None], tab, jnp.asarray(NEG_BIG, BF16))


def _na(p, bias):
    n_blk = T_LAT // NA_QT
    nh = NA_HEADS_PER_STEP
    hw = nh * LANE
    k0, v0 = 20 * LANE // hw, 28 * LANE // hw
    kind = lambda i: jnp.where(i == 0, 0, jnp.where(i == n_blk - 1, 2, 1))
    return pl.pallas_call(
        _na_kernel,
        grid=(BATCH, NA_HEADS // nh, n_blk),
        in_specs=[pl.BlockSpec((None, NA_QT, hw), lambda b, h, i: (b, i, h)),
                  pl.BlockSpec((None, T_ALL, hw), lambda b, h, i: (b, 0, k0 + h)),
                  pl.BlockSpec((None, T_ALL, hw), lambda b, h, i: (b, 0, v0 + h)),
                  pl.BlockSpec((None, nh, NA_QT, NA_KT), lambda b, h, i: (kind(i), h, 0, 0))],
        out_specs=pl.BlockSpec((None, NA_QT, hw), lambda b, h, i: (b, i, h)),
        out_shape=jax.ShapeDtypeStruct((BATCH, T_LAT, NA_DIM), BF16),
        compiler_params=_cparams(3),
        name="nbr_attention",
    )(p, p, p, bias)


def _gla_kernel(q_ref, k_ref, v_ref, g_ref, lr_ref, wcat_ref, bup_ref, gain_ref,
                o_ref, acc_ref):
    r, c = GLA_SUPER, GLA_SUB
    n_sub = r // c
    dv2 = 2 * GLA_DV
    qscale = GLA_DK ** -0.5
    row = lax.broadcasted_iota(jnp.int32, (r, r), 0)
    col = lax.broadcasted_iota(jnp.int32, (r, r), 1)
    sub_shift = c.bit_length() - 1
    same = (row >> sub_shift) == (col >> sub_shift)
    blk_ones = jnp.where(same, 1.0, 0.0).astype(BF16)
    second_head_lane = lax.broadcasted_iota(jnp.int32, (1, LANE), 1) >= GLA_DK
    st_mask = ((lax.broadcasted_iota(jnp.int32, (dv2, LANE), 0) >= GLA_DV)
               == (lax.broadcasted_iota(jnp.int32, (dv2, LANE), 1) >= GLA_DK))

    for d in range(2):
        tri = same & ((row >= col) if d == 0 else (col >= row))
        tri_b = jnp.where(tri, 1.0, 0.0).astype(BF16)

        def pair_step(hp, start, lr, st, d=d, tri=tri, tri_b=tri_b):
            kl = pl.ds(hp * LANE, LANE)
            vl = pl.ds(hp * dv2, dv2)
            zz = _dot(lr, wcat_ref[d, :, pl.ds(hp * 2 * LANE, 2 * LANE)])
            z = zz[:, :LANE] + zz[:, LANE:] + bup_ref[d, :, kl]
            la = (jnp.minimum(z, 0.0) - jnp.log(1.0 + jnp.exp(-jnp.abs(z)))) * (1.0 / GLA_TAU)
            hl = jnp.concatenate(_split2(la), axis=1)
            bb = _dot(tri_b, hl)
            b = bb[:, :LANE] + bb[:, LANE:]
            bb = _dot(blk_ones, hl)
            bt = bb[:, :LANE] + bb[:, LANE:]
            q = q_ref[pl.ds(start, r), kl].astype(F32) * qscale
            k = k_ref[pl.ds(start, r), kl].astype(F32)
            v = v_ref[pl.ds(start, r), vl]
            qi = q * jnp.exp(b)
            qib = qi.astype(BF16)
            ki = (k * jnp.exp(-b)).astype(BF16)
            ko = (k * jnp.exp(bt - b)).astype(BF16)
            dec = jnp.exp(bt)
            outs = []
            for j in range(2):
                qm = jnp.where(second_head_lane == (j == 1), qi, 0.0).astype(BF16)
                att = jnp.where(tri, _dot_nt(qm, ki), 0.0)
                outs.append(_dot(att.astype(BF16), v[:, j * GLA_DV:(j + 1) * GLA_DV]))
            o_intra = jnp.concatenate(outs, axis=1)
            parts = [None] * n_sub
            for s in (range(n_sub) if d == 0 else reversed(range(n_sub))):
                lo_r, hi_r = s * c, (s + 1) * c
                parts[s] = o_intra[lo_r:hi_r] + _dot_nt(qib[lo_r:hi_r], st.astype(BF16))
                st = (st * dec[lo_r:lo_r + 1, :]
                      + jnp.where(st_mask, _dot_tn(v[lo_r:hi_r], ko[lo_r:hi_r]), 0.0))
            o = jnp.concatenate(parts, axis=0)
            if d == 0:
                acc_ref[pl.ds(start, r), vl] = o
            else:
                o = acc_ref[pl.ds(start, r), vl] + o
                gate = g_ref[pl.ds(start, r), vl].astype(F32)
                on = (jnp.concatenate([_rms(o[:, :GLA_DV]), _rms(o[:, GLA_DV:])], axis=1)
                      * gain_ref[:, vl])
                o_ref[pl.ds(start, r), vl] = (on * _silu(gate)).astype(o_ref.dtype)
            return st

        def step(i, sts, base, n, d=d):
            idx = i if d == 0 else n - 1 - i
            start = pl.multiple_of(base + idx * r, r)
            lr = lr_ref[pl.ds(start, r), :]
            return tuple(pair_step(hp, start, lr, sts[hp]) for hp in range(GLA_PAIRS))

        sts = tuple(jnp.zeros((dv2, LANE), F32) for _ in range(GLA_PAIRS))
        sts = lax.fori_loop(0, T_CTX // r, functools.partial(step, base=T_LAT, n=T_CTX // r), sts)
        lax.fori_loop(0, T_LAT // r, functools.partial(step, base=0, n=T_LAT // r), sts)


def _gla(p, lr, w_cat, b_up, gla_norm):
    kw = GLA_PAIRS * LANE
    vw = GLA_PAIRS * 2 * GLA_DV
    q0, k0 = 8 * LANE // kw, 36 * LANE // kw
    v0, g0 = 40 * LANE // vw, 12 * LANE // vw
    return pl.pallas_call(
        _gla_kernel,
        grid=(BATCH, GLA_HEADS // (2 * GLA_PAIRS)),
        in_specs=[pl.BlockSpec((None, T_ALL, kw), lambda b, j: (b, 0, q0 + j)),
                  pl.BlockSpec((None, T_ALL, kw), lambda b, j: (b, 0, k0 + j)),
                  pl.BlockSpec((None, T_ALL, vw), lambda b, j: (b, 0, v0 + j)),
                  pl.BlockSpec((None, T_ALL, vw), lambda b, j: (b, 0, g0 + j)),
                  pl.BlockSpec((None, T_ALL, LANE), lambda b, j: (b, 0, 0)),
                  pl.BlockSpec((2, LANE, 2 * kw), lambda b, j: (0, 0, j)),
                  pl.BlockSpec((2, 1, kw), lambda b, j: (0, 0, j)),
                  pl.BlockSpec((1, vw), lambda b, j: (0, j))],
        out_specs=pl.BlockSpec((None, T_ALL, vw), lambda b, j: (b, 0, j)),
        out_shape=jax.ShapeDtypeStruct((BATCH, T_ALL, GLA_V_DIM), BF16),
        scratch_shapes=[pltpu.VMEM((T_ALL, vw), F32)],
        compiler_params=_cparams(2),
        name="gla",
    )(p, p, p, p, lr, w_cat, b_up, gla_norm)


def _out_kernel(a1_ref, a2_ref, w1_ref, w2_ref, x_ref, gl_ref, gc_ref, gain_ref,
                scl_ref, scc_ref, shl_ref, shc_ref, rw_ref, xo_ref, h_ref, lg_ref, *, tile,
                has_ctx):
    half = tile // 2
    for r0 in (0, half):
        rows = pl.ds(r0, half)
        sel = functools.partial(_region_select, tile, row0=r0, n_rows=half, has_ctx=has_ctx)
        y = _dot(a1_ref[rows, :], w1_ref[...]) + _dot(a2_ref[rows, :], w2_ref[...])
        xn = x_ref[rows, :] + sel(gl_ref, gc_ref) * y
        xo_ref[rows, :] = xn
        h = _rms(xn) * gain_ref[...] * (1.0 + sel(scl_ref, scc_ref)) + sel(shl_ref, shc_ref)
        h_hi = h.astype(BF16)
        h_ref[rows, :] = h_hi
        h_lo = (h - h_hi.astype(F32)).astype(BF16)
        lg_ref[rows, :] = _dot(h_hi, rw_ref[...]) + _dot(h_lo, rw_ref[...])


def _out_proj(a1, a2, w_out_bf16, x, mod_lat, mod_ctx, norm_gain, rw_split, tile, n_tiles):
    half = w_out_bf16.shape[0] // 2
    t_out = n_tiles * tile
    row = lambda width: pl.BlockSpec((None, tile, width), lambda b, i: (b, i, 0))
    return pl.pallas_call(
        functools.partial(_out_kernel, tile=tile, has_ctx=t_out > T_LAT),
        grid=(BATCH, n_tiles),
        in_specs=[row(half), row(half),
                  pl.BlockSpec((half, D_MODEL), lambda b, i: (0, 0)),
                  pl.BlockSpec((half, D_MODEL), lambda b, i: (1, 0)),
                  row(D_MODEL), _lat_spec(2), _ctx_spec(2),
                  pl.BlockSpec((1, D_MODEL), lambda b, i: (0, 0)),
                  _lat_spec(4), _ctx_spec(4), _lat_spec(3), _ctx_spec(3),
                  pl.BlockSpec((D_MODEL, 2 * N_EXPERTS), lambda b, i: (0, 0))],
        out_specs=[row(D_MODEL), row(D_MODEL), row(2 * N_EXPERTS)],
        out_shape=[jax.ShapeDtypeStruct((BATCH, t_out, D_MODEL), F32),
                   jax.ShapeDtypeStruct((BATCH, t_out, D_MODEL), BF16),
                   jax.ShapeDtypeStruct((BATCH, t_out, 2 * N_EXPERTS), F32)],
        compiler_params=_cparams(2),
        name="out_proj",
    )(a1, a2, w_out_bf16, w_out_bf16, x, mod_lat, mod_ctx, norm_gain,
      mod_lat, mod_ctx, mod_lat, mod_ctx, rw_split)


def _moe_up_kernel(x_ref, wg_ref, wu_ref, o_ref):
    x = x_ref[...]
    a = _dot(x, wg_ref[...].astype(BF16))
    u = _dot(x, wu_ref[...].astype(BF16))
    o_ref[...] = (_silu(a) * u).astype(o_ref.dtype)


def _moe_down_kernel(h_ref, wd_ref, g_ref, o_ref):
    o_ref[...] = (_dot(h_ref[...], wd_ref[...].astype(BF16)) * g_ref[...]).astype(o_ref.dtype)


def _moe(xs, gates, w_gate, w_up, w_down, layer):
    e, m, _ = xs.shape
    tn = MOE_TN
    wspec = pl.BlockSpec((None, None, D_MODEL, tn), lambda e, f: (layer, e, 0, f))
    hmid = pl.pallas_call(
        _moe_up_kernel,
        grid=(e, D_EXPERT // tn),
        in_specs=[pl.BlockSpec((None, m, D_MODEL), lambda e, f: (e, 0, 0)), wspec, wspec],
        out_specs=pl.BlockSpec((None, m, tn), lambda e, f: (e, 0, f)),
        out_shape=jax.ShapeDtypeStruct((e, m, D_EXPERT), BF16),
        compiler_params=_cparams(2),
        name="moe_up",
    )(xs, w_gate, w_up)
    return pl.pallas_call(
        _moe_down_kernel,
        grid=(e, D_MODEL // tn),
        in_specs=[pl.BlockSpec((None, m, D_EXPERT), lambda e, f: (e, 0, 0)),
                  pl.BlockSpec((None, None, D_EXPERT, tn), lambda e, f: (layer, e, 0, f)),
                  pl.BlockSpec((None, m, 1), lambda e, f: (e, 0, 0))],
        out_specs=pl.BlockSpec((None, m, tn), lambda e, f: (e, 0, f)),
        out_shape=jax.ShapeDtypeStruct((e, m, D_MODEL), BF16),
        compiler_params=_cparams(2),
        name="moe_down",
    )(hmid, w_down, gates)


def _combine_moe(span_ref, pair_tok_ref, ys_hbm, acc_ref, buf_ref, sem, *, tile, n_pairs):
    b, i = pl.program_id(0), pl.program_id(1)
    n_tiles = pl.num_programs(1)
    step = b * n_tiles + i
    n_steps = pl.num_programs(0) * n_tiles
    w = PAIR_WIN
    shift = w.bit_length() - 1
    last_start = n_pairs // w - PAIR_CHUNK

    def first_window(s):
        return jnp.minimum(span_ref[2 * s] >> shift, last_start)

    def chunk_copy(sample, win, slot):
        return pltpu.make_async_copy(
            ys_hbm.at[pl.ds(sample * n_pairs + win * w, PAIR_CHUNK * w), :],
            buf_ref.at[slot], sem.at[slot])

    def window_copy(win, slot):
        return pltpu.make_async_copy(ys_hbm.at[pl.ds(b * n_pairs + win * w, w), :],
                                     buf_ref.at[slot, pl.ds(0, w), :], sem.at[slot])

    slot = step & 1
    w0 = first_window(step)

    @pl.when(step == 0)
    def _():
        chunk_copy(b, w0, slot).start()

    @pl.when(step + 1 < n_steps)
    def _():
        next_sample = jnp.where(i + 1 < n_tiles, b, b + 1)
        chunk_copy(next_sample, first_window(step + 1), 1 - slot).start()

    chunk_copy(b, w0, slot).wait()
    tok = i * tile + lax.broadcasted_iota(jnp.int32, (tile, 1), 0)

    def select(win):
        return jnp.where(tok == pair_tok_ref[pl.ds(win, 1), :], 1.0, 0.0).astype(BF16)

    sel = jnp.concatenate([select(w0 + c) for c in range(PAIR_CHUNK)], axis=1)
    acc_ref[...] = _dot(sel, buf_ref[slot])

    hi = span_ref[2 * step + 1]
    end_win = jnp.where(hi > 0, ((hi - 1) >> shift) + 1, 0)

    def body(win, carry):
        cp = window_copy(win, slot)
        cp.start()
        cp.wait()
        acc_ref[...] += _dot(select(win), buf_ref[slot, pl.ds(0, w), :])
        return carry

    lax.fori_loop(w0 + PAIR_CHUNK, end_win, body, 0)
    return acc_ref[...]


def _res_mid_kernel(span_ref, x_ref, pair_tok_ref, ys_hbm, gl_ref, gc_ref, gain_ref,
                    scl_ref, scc_ref, shl_ref, shc_ref, xo_ref, h_ref, acc_ref, buf_ref, sem,
                    *, tile, n_pairs):
    moe = _combine_moe(span_ref, pair_tok_ref, ys_hbm, acc_ref, buf_ref, sem,
                       tile=tile, n_pairs=n_pairs)
    xn = x_ref[...] + _region_select(tile, gl_ref, gc_ref) * moe
    xo_ref[...] = xn
    h_ref[...] = (_rms(xn) * gain_ref[...] * (1.0 + _region_select(tile, scl_ref, scc_ref))
                  + _region_select(tile, shl_ref, shc_ref)).astype(h_ref.dtype)


def _res_final_kernel(span_ref, x_ref, pair_tok_ref, ys_hbm, g_ref, gain_ref, o_ref,
                      acc_ref, buf_ref, sem, *, tile, n_pairs):
    moe = _combine_moe(span_ref, pair_tok_ref, ys_hbm, acc_ref, buf_ref, sem,
                       tile=tile, n_pairs=n_pairs)
    o_ref[...] = _rms(x_ref[...] + g_ref[...] * moe) * gain_ref[...]


def _combine_call(kernel_fn, name, tile, t_rows, spans, x, pair_tok, ys, params, param_specs,
                  out_dtypes):
    n_pairs = pair_tok.shape[1] * PAIR_WIN
    row = pl.BlockSpec((None, tile, D_MODEL), lambda b, i, *_: (b, i, 0))
    return pl.pallas_call(
        functools.partial(kernel_fn, tile=tile, n_pairs=n_pairs),
        grid_spec=pltpu.PrefetchScalarGridSpec(
            num_scalar_prefetch=1,
            grid=(BATCH, t_rows // tile),
            in_specs=[row,
                      pl.BlockSpec((None,) + pair_tok.shape[1:], lambda b, i, *_: (b, 0, 0)),
                      pl.BlockSpec(memory_space=pl.ANY)] + param_specs,
            out_specs=[row] * len(out_dtypes),
            scratch_shapes=[pltpu.VMEM((tile, D_MODEL), F32),
                            pltpu.VMEM((2, PAIR_CHUNK * PAIR_WIN, D_MODEL), BF16),
                            pltpu.SemaphoreType.DMA((2,))]),
        out_shape=[jax.ShapeDtypeStruct((BATCH, t_rows, D_MODEL), dt) for dt in out_dtypes],
        compiler_params=_cparams(2),
        name=name,
    )(spans, x, pair_tok, ys, *params)


def _res_mid(x, spans, pair_tok, ys, mod_lat, mod_ctx, next_lat, next_ctx, norm_gain_next):
    specs = [_lat_spec(5), _ctx_spec(5), pl.BlockSpec((1, D_MODEL), lambda b, i, *_: (0, 0)),
             _lat_spec(1), _ctx_spec(1), _lat_spec(0), _ctx_spec(0)]
    params = (mod_lat, mod_ctx, norm_gain_next, next_lat, next_ctx, next_lat, next_ctx)
    return _combine_call(_res_mid_kernel, "ffn_combine_residual_prenorm", COMB_TILE, T_ALL, spans,
                         x, pair_tok, ys, params, specs, (F32, BF16))


def _res_final(x, spans, pair_tok, ys, mod_lat, norm_final):
    specs = [_lat_spec(5), pl.BlockSpec((1, D_MODEL), lambda b, i, *_: (0, 0))]
    return _combine_call(_res_final_kernel, "ffn_combine_residual_final_norm", COMB_TILE, T_LAT,
                         spans, x, pair_tok, ys, (mod_lat, norm_final), specs, (F32,))[0]


def _rope_tables():
    half = HEAD_DIM // 2
    nf = half // 2
    t = np.arange(T_LAT)
    rows, cols = t // GRID_W, t % GRID_W
    freq = ROPE_BASE ** (-jnp.arange(nf, dtype=F32) / nf)
    ang_r = jnp.asarray(rows, F32)[:, None] * freq
    ang_c = jnp.asarray(cols, F32)[:, None] * freq
    zeros = jnp.zeros_like(ang_r)
    cos = jnp.concatenate([jnp.cos(ang_r), jnp.cos(ang_r), jnp.cos(ang_c), jnp.cos(ang_c)], axis=1)
    s1 = jnp.concatenate([-jnp.sin(ang_r), zeros, -jnp.sin(ang_c), zeros], axis=1)
    s2 = jnp.concatenate([zeros, jnp.sin(ang_r), zeros, jnp.sin(ang_c)], axis=1)
    return cos, s1, s2


def _router_split(rw):
    hi = rw.astype(BF16)
    lo = (rw - hi.astype(F32)).astype(BF16)
    return jnp.concatenate([hi, lo], axis=1)


def _route(logits, t0, t1, cap):
    aff = jax.nn.softmax(logits[:, t0:t1], axis=-1)
    g, idx = lax.top_k(aff.transpose(0, 2, 1), cap)
    return g, idx + t0


def _expert_ffn(h, logits2, w_gate, w_up, w_down, layer, with_ctx, tile):
    logits = logits2[..., :N_EXPERTS] + logits2[..., N_EXPERTS:]
    g, idx = _route(logits, 0, T_LAT, EC_CAPACITY * T_LAT // N_EXPERTS)
    if with_ctx:
        gc, idxc = _route(logits, T_LAT, T_ALL, EC_CAPACITY * T_CTX // N_EXPERTS)
        g = jnp.concatenate([g, gc], axis=-1)
        idx = jnp.concatenate([idx, idxc], axis=-1)
    cap = idx.shape[-1]
    t_rows = h.shape[1]
    n_pairs = N_EXPERTS * cap
    assert n_pairs % PAIR_WIN == 0
    flat = (idx + jnp.arange(BATCH)[:, None, None] * t_rows).transpose(1, 0, 2).reshape(N_EXPERTS, -1)
    gates = g.transpose(1, 0, 2).reshape(N_EXPERTS, BATCH * cap, 1)
    xs = h.reshape(BATCH * t_rows, D_MODEL)[flat]
    y = _moe(xs, gates, w_gate, w_up, w_down, layer)
    src = (jnp.arange(N_EXPERTS)[None, :, None] * (BATCH * cap)
           + jnp.arange(BATCH)[:, None, None] * cap + jnp.arange(cap)[None, None, :])
    pair_tok, src = lax.sort((idx.reshape(BATCH, n_pairs), src.reshape(BATCH, n_pairs)),
                             dimension=1, num_keys=1)
    ys = y.reshape(-1, D_MODEL)[src.reshape(-1)]
    bounds = jnp.arange(0, t_rows + tile, tile)
    below = jnp.sum(pair_tok[:, :, None] < bounds[None, None, :], axis=1)
    spans = jnp.stack([below[:, :-1], below[:, 1:]], axis=-1).reshape(-1).astype(jnp.int32)
    return spans, pair_tok.reshape(BATCH, n_pairs // PAIR_WIN, PAIR_WIN), ys


def kernel(x, c, ctx, c_ctx, ada_w, ada_b, norm_mix, norm_ffn, norm_final, ev_w_in, ev_w_out,
           ret_gamma_logit, ret_norm, gmlp_norm, gmlp_ws, gmlp_bs, od_w_in, od_w_out, na_rpb,
           gla_w_up, gla_b_up, gla_norm, router_w, moe_w_gate, moe_w_up, moe_w_down):
    assert x.shape == (BATCH, T_LAT, D_MODEL) and ctx.shape == (BATCH, T_CTX, D_MODEL)
    assert ada_w.shape[0] == 2 and ev_w_in.shape[0] == 1 and od_w_in.shape[0] == 1

    cond = jnp.concatenate([jax.nn.silu(c), jax.nn.silu(c_ctx)[None],
                            jnp.zeros((11, D_MODEL), F32)], axis=0).astype(BF16)
    mod_lat, mod_ctx = [], []
    for l in range(2):
        m = _mod_matmul(cond, ada_w, l) + ada_b[l]
        m = m[:BATCH + 1].reshape(BATCH + 1, 6, 1, D_MODEL)
        mod_lat.append(m[:BATCH])
        mod_ctx.append(m[BATCH])

    cos, s1, s2 = _rope_tables()

    xa, h = _modnorm(x, ctx, norm_mix[0:1], mod_lat[0], mod_ctx[0], 1, 0)
    p = _matmul(h.reshape(BATCH * T_ALL, D_MODEL), ev_w_in[0].astype(BF16), 2176, 1024, BF16)
    p = p.reshape(BATCH, T_ALL, EVEN_COLS)
    log_g = jax.nn.log_sigmoid(ret_gamma_logit[0].astype(F32))
    ret = _retention(p, log_g, cos, s1, s2, ret_norm[0:1])
    gm = _gmlp(p, gmlp_norm[0:1], gmlp_ws[0].astype(BF16), gmlp_bs[0].T)
    xa, h2, logits = _out_proj(ret, gm, ev_w_out[0].astype(BF16), xa, mod_lat[0], mod_ctx[0],
                               norm_ffn[0:1], _router_split(router_w[0]), MIX_TILE,
                               T_ALL // MIX_TILE)
    spans, pair_tok, ys = _expert_ffn(h2, logits, moe_w_gate, moe_w_up, moe_w_down, 0, True,
                                      COMB_TILE)
    xa, h = _res_mid(xa, spans, pair_tok, ys, mod_lat[0], mod_ctx[0], mod_lat[1], mod_ctx[1],
                     norm_mix[1:2])

    h2d = h.reshape(BATCH * T_ALL, D_MODEL)
    p = _matmul(h2d, od_w_in[0][:, :ODD_MAIN_COLS].astype(BF16), 1024, 1024, BF16)
    p = p.reshape(BATCH, T_ALL, ODD_MAIN_COLS)
    w_lr = jnp.pad(od_w_in[0][:, ODD_MAIN_COLS:], ((0, 0), (0, LANE - 2 * GLA_GATE_RANK)))
    lr = _matmul(h2d, w_lr.astype(BF16), 2176, LANE, BF16).reshape(BATCH, T_ALL, LANE)
    o_na = _na(p, _na_bias_table(na_rpb[0]))
    w_up_full = jnp.zeros((2, LANE, GLA_QK_DIM), F32)
    for d in range(2):
        w_up_full = w_up_full.at[d, d * GLA_GATE_RANK:(d + 1) * GLA_GATE_RANK].set(gla_w_up[0, d])
    w_hi = w_up_full.astype(BF16)
    w_lo = (w_up_full - w_hi.astype(F32)).astype(BF16)
    n_pair = GLA_QK_DIM // LANE
    w_cat = jnp.stack([w_hi.reshape(2, LANE, n_pair, LANE), w_lo.reshape(2, LANE, n_pair, LANE)],
                      axis=3).reshape(2, LANE, 2 * GLA_QK_DIM)
    gla = _gla(p, lr, w_cat, gla_b_up[0][:, None, :], gla_norm[0:1])
    xl, h2, logits = _out_proj(o_na, gla, od_w_out[0].astype(BF16), xa, mod_lat[1], mod_ctx[1],
                               norm_ffn[1:2], _router_split(router_w[1]), LAT_TILE,
                               T_LAT // LAT_TILE)
    spans, pair_tok, ys = _expert_ffn(h2, logits, moe_w_gate, moe_w_up, moe_w_down, 1, False,
                                      COMB_TILE)
    return _res_final(xl, spans, pair_tok, ys, mod_lat[1], norm_final[None])
```

```python
import functools

import numpy as np
import jax
import jax.numpy as jnp
from jax import lax
from jax.experimental import pallas as pl
from jax.experimental.pallas import tpu as pltpu

D_MODEL = 2048
BATCH = 4
T_LAT = 4096
T_CTX = 256
T_ALL = T_LAT + T_CTX
GRID_W = 64
GRID_ROWS = T_LAT // GRID_W
HEAD_DIM = 128
RET_HEADS = 8
RET_DIM = 1024
GMLP_DIM = 1024
GMLP_GROUPS = 8
GMLP_CHUNK = 128
NA_HEADS = 8
NA_DIM = 1024
WIN_ROWS = 8
WIN_COLS = 16
GLA_HEADS = 8
GLA_DK = 64
GLA_DV = 128
GLA_QK_DIM = 512
GLA_V_DIM = 1024
GLA_GATE_RANK = 16
GLA_TAU = 16.0
N_EXPERTS = 16
D_EXPERT = 2048
EC_CAPACITY = 2
ROPE_BASE = 10000.0
RMS_EPS = 1e-6
EVEN_COLS = 6144
ODD_COLS = 6176
ODD_MAIN_COLS = 6144
LANE = 128

ROW_TILE = 256
MIX_TILE = 544
LAT_TILE = 512
RET_CHUNK = 256
GLA_SUB = 64
GLA_SUPER = 256
GLA_PAIRS = 2
NA_HEADS_PER_STEP = 4
RET_HEADS_PER_STEP = 2
RET_UNROLL = 4
GLA_UNROLL = 2
NA_QROWS = 4
NA_QT = NA_QROWS * GRID_W
NA_KROWS = NA_QROWS + WIN_ROWS - 1
NA_KT = NA_KROWS * GRID_W
NEG_BIG = -1e30
MOE_TN = 512
MOE_DOWN_TN = 1024
PAIR_WIN = 256
PAIR_CHUNK = 3
COMB_TILE = 256
VMEM_LIMIT = 56 * 1024 * 1024

F32 = jnp.float32
BF16 = jnp.bfloat16


def _cparams(n_axes):
    return pltpu.CompilerParams(dimension_semantics=("arbitrary",) * n_axes,
                                vmem_limit_bytes=VMEM_LIMIT)


def _dot(a, b):
    return jnp.dot(a, b, preferred_element_type=F32)


def _dot_nt(a, b):
    return lax.dot_general(a, b, (((1,), (1,)), ((), ())), preferred_element_type=F32)


def _dot_tn(a, b):
    return lax.dot_general(a, b, (((0,), (0,)), ((), ())), preferred_element_type=F32)


def _silu(x):
    return x * jax.nn.sigmoid(x)


def _rms(x):
    return x * lax.rsqrt(jnp.mean(x * x, axis=-1, keepdims=True) + RMS_EPS)


def _split2(x):
    hi = x.astype(BF16)
    lo = (x - hi.astype(F32)).astype(BF16)
    return hi, lo


def _mm_kernel(a_ref, w_ref, o_ref):
    o_ref[...] = _dot(a_ref[...], w_ref[...]).astype(o_ref.dtype)


def _matmul(a, w, tm, tn, out_dtype):
    m, k = a.shape
    n = w.shape[1]
    return pl.pallas_call(
        _mm_kernel,
        grid=(n // tn, m // tm),
        in_specs=[pl.BlockSpec((tm, k), lambda j, i: (i, 0)),
                  pl.BlockSpec((k, tn), lambda j, i: (0, j))],
        out_specs=pl.BlockSpec((tm, tn), lambda j, i: (i, j)),
        out_shape=jax.ShapeDtypeStruct((m, n), out_dtype),
        compiler_params=_cparams(2),
        name="matmul",
    )(a, w)


def _proj_kernel(a_ref, w_ref, o_ref):
    o_ref[...] = _dot(a_ref[...], w_ref[...].astype(BF16)).astype(o_ref.dtype)


def _proj_matmul(a, w, layer, n_cols, tm, tn):
    m, k = a.shape
    return pl.pallas_call(
        _proj_kernel,
        grid=(n_cols // tn, m // tm),
        in_specs=[pl.BlockSpec((tm, k), lambda j, i: (i, 0)),
                  pl.BlockSpec((None, k, tn), lambda j, i: (layer, 0, j))],
        out_specs=pl.BlockSpec((tm, tn), lambda j, i: (i, j)),
        out_shape=jax.ShapeDtypeStruct((m, n_cols), BF16),
        compiler_params=_cparams(2),
        name="proj_matmul",
    )(a, w)


def _mod_mm_kernel(a_ref, w_ref, o_ref):
    o_ref[...] = _dot(a_ref[...], w_ref[...].astype(BF16))


def _mod_matmul(cond, ada_w, layer):
    m, k = cond.shape
    n = ada_w.shape[2]
    tn = 1024
    return pl.pallas_call(
        _mod_mm_kernel,
        grid=(n // tn,),
        in_specs=[pl.BlockSpec((m, k), lambda j: (0, 0)),
                  pl.BlockSpec((None, k, tn), lambda j: (layer, 0, j))],
        out_specs=pl.BlockSpec((m, tn), lambda j: (0, j)),
        out_shape=jax.ShapeDtypeStruct((m, n), F32),
        compiler_params=_cparams(1),
        name="mod_matmul",
    )(cond, ada_w)


def _lat_spec(which):
    return pl.BlockSpec((None, None, 1, D_MODEL), lambda b, i, *_: (b, which, 0, 0))


def _ctx_spec(which):
    return pl.BlockSpec((None, 1, D_MODEL), lambda b, i, *_: (which, 0, 0))


def _region_select(tile, lat_ref, ctx_ref, row0=0, n_rows=None, has_ctx=True):
    if not has_ctx:
        return lat_ref[...]
    n_rows = tile if n_rows is None else n_rows
    rows = pl.program_id(1) * tile + row0 + lax.broadcasted_iota(jnp.int32, (n_rows, 1), 0)
    return jnp.where(rows >= T_LAT, ctx_ref[...], lat_ref[...])


def _modnorm_kernel(x_ref, ctx_ref, gain_ref, scl_ref, scc_ref, shl_ref, shc_ref, xa_ref, o_ref,
                    *, tile):
    is_ctx = pl.program_id(1) >= T_LAT // tile
    x = jnp.where(is_ctx, ctx_ref[...], x_ref[...])
    xa_ref[...] = x
    sc = _region_select(tile, scl_ref, scc_ref)
    sh = _region_select(tile, shl_ref, shc_ref)
    o_ref[...] = (_rms(x) * gain_ref[...] * (1.0 + sc) + sh).astype(o_ref.dtype)


def _modnorm(x, ctx, gain, mod_lat, mod_ctx, sc_i, sh_i):
    tile = T_CTX
    n_lat = T_LAT // tile
    row = pl.BlockSpec((None, tile, D_MODEL), lambda b, i: (b, i, 0))
    return pl.pallas_call(
        functools.partial(_modnorm_kernel, tile=tile),
        grid=(BATCH, T_ALL // tile),
        in_specs=[pl.BlockSpec((None, tile, D_MODEL), lambda b, i: (b, jnp.minimum(i, n_lat - 1), 0)),
                  pl.BlockSpec((None, tile, D_MODEL), lambda b, i: (b, jnp.maximum(i - n_lat, 0), 0)),
                  pl.BlockSpec((1, D_MODEL), lambda b, i: (0, 0)),
                  _lat_spec(sc_i), _ctx_spec(sc_i), _lat_spec(sh_i), _ctx_spec(sh_i)],
        out_specs=[row, row],
        out_shape=[jax.ShapeDtypeStruct((BATCH, T_ALL, D_MODEL), F32),
                   jax.ShapeDtypeStruct((BATCH, T_ALL, D_MODEL), BF16)],
        compiler_params=_cparams(2),
        name="modnorm",
    )(x, ctx, gain, mod_lat, mod_ctx, mod_lat, mod_ctx)


def _rotary(z, cos, s1, s2):
    return z * cos + pltpu.roll(z, 96, 1) * s1 + pltpu.roll(z, 32, 1) * s2


def _ret_kernel(lg_ref, q_ref, g_ref, k_ref, v_ref, cos_ref, s1_ref, s2_ref, gain_ref,
                o_ref, acc_ref, qs_ref, ks_ref):
    c = RET_CHUNK
    nh = RET_HEADS_PER_STEP
    scale = HEAD_DIM ** -0.5
    dist = (lax.broadcasted_iota(jnp.int32, (c, c), 0)
            - lax.broadcasted_iota(jnp.int32, (c, c), 1)).astype(F32)
    tcol = lax.broadcasted_iota(jnp.int32, (c, 1), 0).astype(F32)

    def head_consts(hp):
        h = pl.program_id(1) * nh + hp
        la_f, la_b = lg_ref[0, h], lg_ref[1, h]
        dmat = (jnp.where(dist >= 0.0, jnp.exp(la_f * jnp.maximum(dist, 0.0)), 0.0)
                + jnp.where(dist <= 0.0, jnp.exp(la_b * jnp.maximum(-dist, 0.0)), 0.0))
        fwd_dec = (jnp.exp(la_f * (tcol + 1.0)), jnp.exp(la_f * (c - 1.0 - tcol)),
                   jnp.exp(jnp.full((1, LANE), la_f * c, F32)))
        bwd_dec = (jnp.exp(la_b * (c - tcol)), jnp.exp(la_b * tcol),
                   jnp.exp(jnp.full((1, LANE), la_b * c, F32)))
        return dmat, fwd_dec, bwd_dec

    consts = [head_consts(hp) for hp in range(nh)]

    def fwd(i, sts, base, rot):
        start = pl.multiple_of(base + i * c, c)
        rows = pl.ds(start, c)
        if rot:
            cos, s1, s2 = cos_ref[rows, :], s1_ref[rows, :], s2_ref[rows, :]
        out = []
        for hp in range(nh):
            hl = pl.ds(hp * LANE, LANE)
            dmat, (qdec, kdec, cdec), _ = consts[hp]
            q = q_ref[rows, hl].astype(F32) * scale
            k = k_ref[rows, hl].astype(F32)
            v = v_ref[rows, hl]
            if rot:
                q = _rotary(q, cos, s1, s2)
                k = _rotary(k, cos, s1, s2)
            qs_ref[rows, hl] = q
            ks_ref[rows, hl] = k
            att = _dot_nt(q.astype(BF16), k.astype(BF16)) * dmat
            st = sts[hp]
            acc_ref[rows, hl] = (_dot(att.astype(BF16), v)
                                 + _dot_nt((q * qdec).astype(BF16), st.astype(BF16)))
            out.append(cdec * st + _dot_tn(v, (k * kdec).astype(BF16)))
        return tuple(out)

    def bwd(i, sts, base, n):
        start = pl.multiple_of(base + (n - 1 - i) * c, c)
        rows = pl.ds(start, c)
        out = []
        for hp in range(nh):
            hl = pl.ds(hp * LANE, LANE)
            _, _, (qdec, kdec, cdec) = consts[hp]
            q = qs_ref[rows, hl]
            k = ks_ref[rows, hl]
            v = v_ref[rows, hl]
            st = sts[hp]
            o = acc_ref[rows, hl] + _dot_nt((q * qdec).astype(BF16), st.astype(BF16))
            gate = g_ref[rows, hl].astype(F32)
            o_ref[rows, hl] = (_rms(o) * gain_ref[:, hl] * _silu(gate)).astype(o_ref.dtype)
            out.append(cdec * st + _dot_tn(v, (k * kdec).astype(BF16)))
        return tuple(out)

    zero = tuple(jnp.zeros((HEAD_DIM, HEAD_DIM), F32) for _ in range(nh))
    n_ctx, n_lat = T_CTX // c, T_LAT // c
    sts = lax.fori_loop(0, n_ctx, functools.partial(fwd, base=T_LAT, rot=False), zero)
    lax.fori_loop(0, n_lat, functools.partial(fwd, base=0, rot=True), sts, unroll=RET_UNROLL)
    sts = lax.fori_loop(0, n_ctx, functools.partial(bwd, base=T_LAT, n=n_ctx), zero)
    lax.fori_loop(0, n_lat, functools.partial(bwd, base=0, n=n_lat), sts, unroll=RET_UNROLL)


def _retention(p, log_g, cos, s1, s2, ret_norm):
    nh = RET_HEADS_PER_STEP
    hw = nh * LANE
    col = lambda off: pl.BlockSpec((None, T_ALL, hw),
                                   lambda b, h, lg: (b, 0, off * LANE // hw + h))
    tab = pl.BlockSpec((T_LAT, LANE), lambda b, h, lg: (0, 0))
    return pl.pallas_call(
        _ret_kernel,
        grid_spec=pltpu.PrefetchScalarGridSpec(
            num_scalar_prefetch=1,
            grid=(BATCH, RET_HEADS // nh),
            in_specs=[col(0), col(8), col(32), col(40), tab, tab, tab,
                      pl.BlockSpec((1, hw), lambda b, h, lg: (0, h))],
            out_specs=pl.BlockSpec((None, T_ALL, hw), lambda b, h, lg: (b, 0, h)),
            scratch_shapes=[pltpu.VMEM((T_ALL, hw), F32), pltpu.VMEM((T_ALL, hw), F32),
                            pltpu.VMEM((T_ALL, hw), F32)]),
        out_shape=jax.ShapeDtypeStruct((BATCH, T_ALL, RET_DIM), BF16),
        compiler_params=_cparams(2),
        name="retention",
    )(log_g, p, p, p, p, cos, s1, s2, ret_norm)


def _gmlp_kernel(u_ref, v_ref, gain_ref, ws_ref, bst_ref, o_ref):
    u = jax.nn.gelu(u_ref[...].astype(F32))
    v = _rms(jax.nn.gelu(v_ref[...].astype(F32))) * gain_ref[...]
    vb = v.astype(BF16)
    for n in range(ROW_TILE // GMLP_CHUNK):
        r0 = n * GMLP_CHUNK
        for g in range(GMLP_GROUPS):
            c0 = g * LANE
            mixed = (_dot(ws_ref[g], vb[r0:r0 + GMLP_CHUNK, c0:c0 + LANE])
                     + bst_ref[:, g:g + 1])
            o_ref[r0:r0 + GMLP_CHUNK, c0:c0 + LANE] = (
                u[r0:r0 + GMLP_CHUNK, c0:c0 + LANE] * mixed).astype(o_ref.dtype)


def _gmlp(p, gmlp_norm, ws_bf16, bs_t):
    return pl.pallas_call(
        _gmlp_kernel,
        grid=(BATCH, T_ALL // ROW_TILE),
        in_specs=[pl.BlockSpec((None, ROW_TILE, GMLP_DIM), lambda b, i: (b, i, 2)),
                  pl.BlockSpec((None, ROW_TILE, GMLP_DIM), lambda b, i: (b, i, 3)),
                  pl.BlockSpec((1, GMLP_DIM), lambda b, i: (0, 0)),
                  pl.BlockSpec((GMLP_GROUPS, GMLP_CHUNK, GMLP_CHUNK), lambda b, i: (0, 0, 0)),
                  pl.BlockSpec((GMLP_CHUNK, GMLP_GROUPS), lambda b, i: (0, 0))],
        out_specs=pl.BlockSpec((None, ROW_TILE, GMLP_DIM), lambda b, i: (b, i, 0)),
        out_shape=jax.ShapeDtypeStruct((BATCH, T_ALL, GMLP_DIM), BF16),
        compiler_params=_cparams(2),
        name="gmlp",
    )(p, p, gmlp_norm, ws_bf16, bs_t)


def _na_kernel(q_ref, k_ref, v_ref, bias_ref, o_ref):
    i = pl.program_id(2)
    w0 = jnp.clip(i * NA_QROWS - WIN_ROWS // 2, 0, GRID_ROWS - NA_KROWS)
    start = pl.multiple_of(w0 * GRID_W, GRID_W)
    scale = HEAD_DIM ** -0.5
    for hp in range(NA_HEADS_PER_STEP):
        hl = pl.ds(hp * LANE, LANE)
        q = q_ref[:, hl]
        s_win = _dot_nt(q, k_ref[pl.ds(start, NA_KT), hl]) * scale + bias_ref[hp]
        s_ctx = _dot_nt(q, k_ref[pl.ds(T_LAT, T_CTX), hl]) * scale
        m = jnp.maximum(jnp.max(s_win, axis=-1, keepdims=True),
                        jnp.max(s_ctx, axis=-1, keepdims=True))
        p_win = jnp.exp(s_win - m)
        p_ctx = jnp.exp(s_ctx - m)
        denom = jnp.sum(p_win, axis=-1, keepdims=True) + jnp.sum(p_ctx, axis=-1, keepdims=True)
        o = (_dot(p_win.astype(BF16), v_ref[pl.ds(start, NA_KT), hl])
             + _dot(p_ctx.astype(BF16), v_ref[pl.ds(T_LAT, T_CTX), hl]))
        o_ref[:, hl] = (o / denom).astype(o_ref.dtype)


def _na_geometry():
    n_blk = GRID_ROWS // NA_QROWS
    dr00 = []
    row_ok = np.zeros((3, NA_QROWS, NA_KROWS), bool)
    for kind, i_rep in enumerate((0, 1, n_blk - 1)):
        r0 = i_rep * NA_QROWS
        w0 = int(np.clip(r0 - WIN_ROWS // 2, 0, GRID_ROWS - NA_KROWS))
        r = r0 + np.arange(NA_QROWS)[:, None]
        kr = w0 + np.arange(NA_KROWS)[None, :]
        rstart = np.clip(r - WIN_ROWS // 2, 0, GRID_ROWS - WIN_ROWS)
        row_ok[kind] = (kr >= rstart) & (kr < rstart + WIN_ROWS)
        dr00.append(w0 - r0 + WIN_ROWS - 1)
    c = np.arange(GRID_W)[:, None]
    kc = np.arange(GRID_W)[None, :]
    cstart = np.clip(c - WIN_COLS // 2, 0, GRID_W - WIN_COLS)
    col_ok = (kc >= cstart) & (kc < cstart + WIN_COLS)
    valid = row_ok[:, :, None, :, None] & col_ok[None, None, :, None, :]
    return dr00, valid.reshape(3, NA_QT, NA_KT)


def _na_bias_table(rpb):
    n_dr, n_dc = 2 * WIN_ROWS - 1, 2 * WIN_COLS - 1
    w = jnp.concatenate([rpb[..., WIN_COLS - 1:], jnp.zeros((NA_HEADS, n_dr, LANE - n_dc), F32),
                         rpb[..., :WIN_COLS - 1]], axis=-1)
    toep = jnp.tile(w, (1, 1, GRID_W))[..., :GRID_W * (LANE - 1)]
    toep = toep.reshape(NA_HEADS, n_dr, GRID_W, LANE - 1)[..., :GRID_W]
    dr00, valid = _na_geometry()
    front = max(0, NA_QROWS - 1 - min(dr00))
    back = max(0, max(dr00) + NA_KROWS - n_dr)
    toep = jnp.pad(toep.transpose(0, 2, 1, 3), ((0, 0), (0, 0), (front, back), (0, 0)))
    toep = toep.astype(BF16).reshape(NA_HEADS, GRID_W, -1)
    kinds = []
    for d0 in dr00:
        starts = [(d0 - rl + front) * GRID_W for rl in range(NA_QROWS)]
        kinds.append(jnp.stack([toep[:, :, s:s + NA_KT] for s in starts], axis=1))
    tab = jnp.stack(kinds, axis=0).reshape(3, NA_HEADS, NA_QT, NA_KT)
    return jnp.where(jnp.asarray(valid)[:, None], tab, jnp.asarray(NEG_BIG, BF16))


def _na(p, bias):
    n_blk = T_LAT // NA_QT
    nh = NA_HEADS_PER_STEP
    hw = nh * LANE
    k0, v0 = 20 * LANE // hw, 28 * LANE // hw
    kind = lambda i: jnp.where(i == 0, 0, jnp.where(i == n_blk - 1, 2, 1))
    return pl.pallas_call(
        _na_kernel,
        grid=(BATCH, NA_HEADS // nh, n_blk),
        in_specs=[pl.BlockSpec((None, NA_QT, hw), lambda b, h, i: (b, i, h)),
                  pl.BlockSpec((None, T_ALL, hw), lambda b, h, i: (b, 0, k0 + h)),
                  pl.BlockSpec((None, T_ALL, hw), lambda b, h, i: (b, 0, v0 + h)),
                  pl.BlockSpec((None, nh, NA_QT, NA_KT), lambda b, h, i: (kind(i), h, 0, 0))],
        out_specs=pl.BlockSpec((None, NA_QT, hw), lambda b, h, i: (b, i, h)),
        out_shape=jax.ShapeDtypeStruct((BATCH, T_LAT, NA_DIM), BF16),
        compiler_params=_cparams(3),
        name="nbr_attention",
    )(p, p, p, bias)


def _gla_kernel(q_ref, k_ref, v_ref, g_ref, lr_ref, wcat_ref, bup_ref, gain_ref,
                o_ref, acc_ref):
    r, c = GLA_SUPER, GLA_SUB
    n_sub = r // c
    dv2 = 2 * GLA_DV
    qscale = GLA_DK ** -0.5
    row = lax.broadcasted_iota(jnp.int32, (r, r), 0)
    col = lax.broadcasted_iota(jnp.int32, (r, r), 1)
    sub_shift = c.bit_length() - 1
    same = (row >> sub_shift) == (col >> sub_shift)
    blk_ones = jnp.where(same, 1.0, 0.0).astype(BF16)
    second_head_lane = lax.broadcasted_iota(jnp.int32, (1, LANE), 1) >= GLA_DK
    st_mask = ((lax.broadcasted_iota(jnp.int32, (dv2, LANE), 0) >= GLA_DV)
               == (lax.broadcasted_iota(jnp.int32, (dv2, LANE), 1) >= GLA_DK))

    for d in range(2):
        tri = same & ((row >= col) if d == 0 else (col >= row))
        tri_b = jnp.where(tri, 1.0, 0.0).astype(BF16)

        def pair_step(hp, start, lr, st, d=d, tri=tri, tri_b=tri_b):
            kl = pl.ds(hp * LANE, LANE)
            vl = pl.ds(hp * dv2, dv2)
            zz = _dot(lr, wcat_ref[d, :, pl.ds(hp * 2 * LANE, 2 * LANE)])
            z = zz[:, :LANE] + zz[:, LANE:] + bup_ref[d, :, kl]
            la = (jnp.minimum(z, 0.0) - jnp.log(1.0 + jnp.exp(-jnp.abs(z)))) * (1.0 / GLA_TAU)
            hl = jnp.concatenate(_split2(la), axis=1)
            bb = _dot(tri_b, hl)
            b = bb[:, :LANE] + bb[:, LANE:]
            bb = _dot(blk_ones, hl)
            bt = bb[:, :LANE] + bb[:, LANE:]
            q = q_ref[pl.ds(start, r), kl].astype(F32) * qscale
            k = k_ref[pl.ds(start, r), kl].astype(F32)
            v = v_ref[pl.ds(start, r), vl]
            qi = q * jnp.exp(b)
            qib = qi.astype(BF16)
            ki = (k * jnp.exp(-b)).astype(BF16)
            ko = (k * jnp.exp(bt - b)).astype(BF16)
            dec = jnp.exp(bt)
            outs = []
            for j in range(2):
                qm = jnp.where(second_head_lane == (j == 1), qi, 0.0).astype(BF16)
                att = jnp.where(tri, _dot_nt(qm, ki), 0.0)
                outs.append(_dot(att.astype(BF16), v[:, j * GLA_DV:(j + 1) * GLA_DV]))
            o_intra = jnp.concatenate(outs, axis=1)
            parts = [None] * n_sub
            for s in (range(n_sub) if d == 0 else reversed(range(n_sub))):
                lo_r, hi_r = s * c, (s + 1) * c
                parts[s] = o_intra[lo_r:hi_r] + _dot_nt(qib[lo_r:hi_r], st.astype(BF16))
                st = (st * dec[lo_r:lo_r + 1, :]
                      + jnp.where(st_mask, _dot_tn(v[lo_r:hi_r], ko[lo_r:hi_r]), 0.0))
            o = jnp.concatenate(parts, axis=0)
            if d == 0:
                acc_ref[pl.ds(start, r), vl] = o
            else:
                o = acc_ref[pl.ds(start, r), vl] + o
                gate = g_ref[pl.ds(start, r), vl].astype(F32)
                on = (jnp.concatenate([_rms(o[:, :GLA_DV]), _rms(o[:, GLA_DV:])], axis=1)
                      * gain_ref[:, vl])
                o_ref[pl.ds(start, r), vl] = (on * _silu(gate)).astype(o_ref.dtype)
            return st

        def step(i, sts, base, n, d=d):
            idx = i if d == 0 else n - 1 - i
            start = pl.multiple_of(base + idx * r, r)
            lr = lr_ref[pl.ds(start, r), :]
            return tuple(pair_step(hp, start, lr, sts[hp]) for hp in range(GLA_PAIRS))

        sts = tuple(jnp.zeros((dv2, LANE), F32) for _ in range(GLA_PAIRS))
        sts = lax.fori_loop(0, T_CTX // r, functools.partial(step, base=T_LAT, n=T_CTX // r), sts)
        lax.fori_loop(0, T_LAT // r, functools.partial(step, base=0, n=T_LAT // r), sts,
                      unroll=GLA_UNROLL)


def _gla(p, lr, w_cat, b_up, gla_norm):
    kw = GLA_PAIRS * LANE
    vw = GLA_PAIRS * 2 * GLA_DV
    q0, k0 = 8 * LANE // kw, 36 * LANE // kw
    v0, g0 = 40 * LANE // vw, 12 * LANE // vw
    return pl.pallas_call(
        _gla_kernel,
        grid=(BATCH, GLA_HEADS // (2 * GLA_PAIRS)),
        in_specs=[pl.BlockSpec((None, T_ALL, kw), lambda b, j: (b, 0, q0 + j)),
                  pl.BlockSpec((None, T_ALL, kw), lambda b, j: (b, 0, k0 + j)),
                  pl.BlockSpec((None, T_ALL, vw), lambda b, j: (b, 0, v0 + j)),
                  pl.BlockSpec((None, T_ALL, vw), lambda b, j: (b, 0, g0 + j)),
                  pl.BlockSpec((None, T_ALL, LANE), lambda b, j: (b, 0, 0)),
                  pl.BlockSpec((2, LANE, 2 * kw), lambda b, j: (0, 0, j)),
                  pl.BlockSpec((2, 1, kw), lambda b, j: (0, 0, j)),
                  pl.BlockSpec((1, vw), lambda b, j: (0, j))],
        out_specs=pl.BlockSpec((None, T_ALL, vw), lambda b, j: (b, 0, j)),
        out_shape=jax.ShapeDtypeStruct((BATCH, T_ALL, GLA_V_DIM), BF16),
        scratch_shapes=[pltpu.VMEM((T_ALL, vw), F32)],
        compiler_params=_cparams(2),
        name="gla",
    )(p, p, p, p, lr, w_cat, b_up, gla_norm)


def _out_kernel(a1_ref, a2_ref, w1_ref, w2_ref, x_ref, gl_ref, gc_ref, gain_ref,
                scl_ref, scc_ref, shl_ref, shc_ref, rw_ref, xo_ref, h_ref, lg_ref, *, tile,
                has_ctx):
    half = tile // 2
    for r0 in (0, half):
        rows = pl.ds(r0, half)
        sel = functools.partial(_region_select, tile, row0=r0, n_rows=half, has_ctx=has_ctx)
        y = _dot(a1_ref[rows, :], w1_ref[...]) + _dot(a2_ref[rows, :], w2_ref[...])
        xn = x_ref[rows, :] + sel(gl_ref, gc_ref) * y
        xo_ref[rows, :] = xn
        h = _rms(xn) * gain_ref[...] * (1.0 + sel(scl_ref, scc_ref)) + sel(shl_ref, shc_ref)
        h_hi = h.astype(BF16)
        h_ref[rows, :] = h_hi
        h_lo = (h - h_hi.astype(F32)).astype(BF16)
        lg_ref[rows, :] = _dot(h_hi, rw_ref[...]) + _dot(h_lo, rw_ref[...])


def _out_proj(a1, a2, w_out_bf16, x, mod_lat, mod_ctx, norm_gain, rw_split, tile, n_tiles):
    half = w_out_bf16.shape[0] // 2
    t_out = n_tiles * tile
    row = lambda width: pl.BlockSpec((None, tile, width), lambda b, i: (b, i, 0))
    return pl.pallas_call(
        functools.partial(_out_kernel, tile=tile, has_ctx=t_out > T_LAT),
        grid=(BATCH, n_tiles),
        in_specs=[row(half), row(half),
                  pl.BlockSpec((half, D_MODEL), lambda b, i: (0, 0)),
                  pl.BlockSpec((half, D_MODEL), lambda b, i: (1, 0)),
                  row(D_MODEL), _lat_spec(2), _ctx_spec(2),
                  pl.BlockSpec((1, D_MODEL), lambda b, i: (0, 0)),
                  _lat_spec(4), _ctx_spec(4), _lat_spec(3), _ctx_spec(3),
                  pl.BlockSpec((D_MODEL, 2 * N_EXPERTS), lambda b, i: (0, 0))],
        out_specs=[row(D_MODEL), row(D_MODEL), row(2 * N_EXPERTS)],
        out_shape=[jax.ShapeDtypeStruct((BATCH, t_out, D_MODEL), F32),
                   jax.ShapeDtypeStruct((BATCH, t_out, D_MODEL), BF16),
                   jax.ShapeDtypeStruct((BATCH, t_out, 2 * N_EXPERTS), F32)],
        compiler_params=_cparams(2),
        name="out_proj",
    )(a1, a2, w_out_bf16, w_out_bf16, x, mod_lat, mod_ctx, norm_gain,
      mod_lat, mod_ctx, mod_lat, mod_ctx, rw_split)


def _moe_up_kernel(x_ref, wg_ref, wu_ref, o_ref):
    x = x_ref[...]
    a = _dot(x, wg_ref[...].astype(BF16))
    u = _dot(x, wu_ref[...].astype(BF16))
    o_ref[...] = (_silu(a) * u).astype(o_ref.dtype)


def _moe_down_kernel(h_ref, wd_ref, g_ref, o_ref):
    o_ref[...] = (_dot(h_ref[...], wd_ref[...].astype(BF16)) * g_ref[...]).astype(o_ref.dtype)


def _moe(xs, gates, w_gate, w_up, w_down, layer):
    e, m, _ = xs.shape
    tn = MOE_TN
    wspec = pl.BlockSpec((None, None, D_MODEL, tn), lambda e, f: (layer, e, 0, f))
    hmid = pl.pallas_call(
        _moe_up_kernel,
        grid=(e, D_EXPERT // tn),
        in_specs=[pl.BlockSpec((None, m, D_MODEL), lambda e, f: (e, 0, 0)), wspec, wspec],
        out_specs=pl.BlockSpec((None, m, tn), lambda e, f: (e, 0, f)),
        out_shape=jax.ShapeDtypeStruct((e, m, D_EXPERT), BF16),
        compiler_params=_cparams(2),
        name="moe_up",
    )(xs, w_gate, w_up)
    tn = MOE_DOWN_TN
    return pl.pallas_call(
        _moe_down_kernel,
        grid=(e, D_MODEL // tn),
        in_specs=[pl.BlockSpec((None, m, D_EXPERT), lambda e, f: (e, 0, 0)),
                  pl.BlockSpec((None, None, D_EXPERT, tn), lambda e, f: (layer, e, 0, f)),
                  pl.BlockSpec((None, m, 1), lambda e, f: (e, 0, 0))],
        out_specs=pl.BlockSpec((None, m, tn), lambda e, f: (e, 0, f)),
        out_shape=jax.ShapeDtypeStruct((e, m, D_MODEL), BF16),
        compiler_params=_cparams(2),
        name="moe_down",
    )(hmid, w_down, gates)


def _combine_moe(span_ref, pair_tok_ref, ys_hbm, acc_ref, buf_ref, sem, *, tile, n_pairs):
    b, i = pl.program_id(0), pl.program_id(1)
    n_tiles = pl.num_programs(1)
    step = b * n_tiles + i
    n_steps = pl.num_programs(0) * n_tiles
    w = PAIR_WIN
    shift = w.bit_length() - 1
    last_start = n_pairs // w - PAIR_CHUNK

    def first_window(s):
        return jnp.minimum(span_ref[2 * s] >> shift, last_start)

    def chunk_copy(sample, win, slot):
        return pltpu.make_async_copy(
            ys_hbm.at[pl.ds(sample * n_pairs + win * w, PAIR_CHUNK * w), :],
            buf_ref.at[slot], sem.at[slot])

    def window_copy(win, slot):
        return pltpu.make_async_copy(ys_hbm.at[pl.ds(b * n_pairs + win * w, w), :],
                                     buf_ref.at[slot, pl.ds(0, w), :], sem.at[slot])

    slot = step & 1
    w0 = first_window(step)

    @pl.when(step == 0)
    def _():
        chunk_copy(b, w0, slot).start()

    @pl.when(step + 1 < n_steps)
    def _():
        next_sample = jnp.where(i + 1 < n_tiles, b, b + 1)
        chunk_copy(next_sample, first_window(step + 1), 1 - slot).start()

    chunk_copy(b, w0, slot).wait()
    tok = i * tile + lax.broadcasted_iota(jnp.int32, (tile, 1), 0)

    def select(win):
        return jnp.where(tok == pair_tok_ref[pl.ds(win, 1), :], 1.0, 0.0).astype(BF16)

    sel = jnp.concatenate([select(w0 + c) for c in range(PAIR_CHUNK)], axis=1)
    acc_ref[...] = _dot(sel, buf_ref[slot])

    hi = span_ref[2 * step + 1]
    end_win = jnp.where(hi > 0, ((hi - 1) >> shift) + 1, 0)

    def body(win, carry):
        cp = window_copy(win, slot)
        cp.start()
        cp.wait()
        acc_ref[...] += _dot(select(win), buf_ref[slot, pl.ds(0, w), :])
        return carry

    lax.fori_loop(w0 + PAIR_CHUNK, end_win, body, 0)
    return acc_ref[...]


def _res_mid_kernel(span_ref, x_ref, pair_tok_ref, ys_hbm, gl_ref, gc_ref, gain_ref,
                    scl_ref, scc_ref, shl_ref, shc_ref, xo_ref, h_ref, acc_ref, buf_ref, sem,
                    *, tile, n_pairs):
    moe = _combine_moe(span_ref, pair_tok_ref, ys_hbm, acc_ref, buf_ref, sem,
                       tile=tile, n_pairs=n_pairs)
    xn = x_ref[...] + _region_select(tile, gl_ref, gc_ref) * moe
    xo_ref[...] = xn
    h_ref[...] = (_rms(xn) * gain_ref[...] * (1.0 + _region_select(tile, scl_ref, scc_ref))
                  + _region_select(tile, shl_ref, shc_ref)).astype(h_ref.dtype)


def _res_final_kernel(span_ref, x_ref, pair_tok_ref, ys_hbm, g_ref, gain_ref, o_ref,
                      acc_ref, buf_ref, sem, *, tile, n_pairs):
    moe = _combine_moe(span_ref, pair_tok_ref, ys_hbm, acc_ref, buf_ref, sem,
                       tile=tile, n_pairs=n_pairs)
    o_ref[...] = _rms(x_ref[...] + g_ref[...] * moe) * gain_ref[...]


def _combine_call(kernel_fn, name, tile, t_rows, spans, x, pair_tok, ys, params, param_specs,
                  out_dtypes):
    n_pairs = pair_tok.shape[1] * PAIR_WIN
    row = pl.BlockSpec((None, tile, D_MODEL), lambda b, i, *_: (b, i, 0))
    return pl.pallas_call(
        functools.partial(kernel_fn, tile=tile, n_pairs=n_pairs),
        grid_spec=pltpu.PrefetchScalarGridSpec(
            num_scalar_prefetch=1,
            grid=(BATCH, t_rows // tile),
            in_specs=[row,
                      pl.BlockSpec((None,) + pair_tok.shape[1:], lambda b, i, *_: (b, 0, 0)),
                      pl.BlockSpec(memory_space=pl.ANY)] + param_specs,
            out_specs=[row] * len(out_dtypes),
            scratch_shapes=[pltpu.VMEM((tile, D_MODEL), F32),
                            pltpu.VMEM((2, PAIR_CHUNK * PAIR_WIN, D_MODEL), BF16),
                            pltpu.SemaphoreType.DMA((2,))]),
        out_shape=[jax.ShapeDtypeStruct((BATCH, t_rows, D_MODEL), dt) for dt in out_dtypes],
        compiler_params=_cparams(2),
        name=name,
    )(spans, x, pair_tok, ys, *params)


def _res_mid(x, spans, pair_tok, ys, mod_lat, mod_ctx, next_lat, next_ctx, norm_gain_next):
    specs = [_lat_spec(5), _ctx_spec(5), pl.BlockSpec((1, D_MODEL), lambda b, i, *_: (0, 0)),
             _lat_spec(1), _ctx_spec(1), _lat_spec(0), _ctx_spec(0)]
    params = (mod_lat, mod_ctx, norm_gain_next, next_lat, next_ctx, next_lat, next_ctx)
    return _combine_call(_res_mid_kernel, "ffn_combine_residual_prenorm", COMB_TILE, T_ALL, spans,
                         x, pair_tok, ys, params, specs, (F32, BF16))


def _res_final(x, spans, pair_tok, ys, mod_lat, norm_final):
    specs = [_lat_spec(5), pl.BlockSpec((1, D_MODEL), lambda b, i, *_: (0, 0))]
    return _combine_call(_res_final_kernel, "ffn_combine_residual_final_norm", COMB_TILE, T_LAT,
                         spans, x, pair_tok, ys, (mod_lat, norm_final), specs, (F32,))[0]


def _rope_tables():
    half = HEAD_DIM // 2
    nf = half // 2
    t = np.arange(T_LAT)
    rows, cols = t // GRID_W, t % GRID_W
    freq = (np.float32(ROPE_BASE) ** (-np.arange(nf, dtype=np.float32) / np.float32(nf))).astype(np.float32)
    ang_r = rows.astype(np.float32)[:, None] * freq
    ang_c = cols.astype(np.float32)[:, None] * freq
    zeros = np.zeros_like(ang_r)
    cos = np.concatenate([np.cos(ang_r), np.cos(ang_r), np.cos(ang_c), np.cos(ang_c)], axis=1)
    s1 = np.concatenate([-np.sin(ang_r), zeros, -np.sin(ang_c), zeros], axis=1)
    s2 = np.concatenate([zeros, np.sin(ang_r), zeros, np.sin(ang_c)], axis=1)
    return tuple(jnp.asarray(a, F32) for a in (cos, s1, s2))


def _router_split(rw):
    hi = rw.astype(BF16)
    lo = (rw - hi.astype(F32)).astype(BF16)
    return jnp.concatenate([hi, lo], axis=1)


def _route(logits, t0, t1, cap):
    aff = jax.nn.softmax(logits[:, t0:t1], axis=-1)
    g, idx = lax.top_k(aff.transpose(0, 2, 1), cap)
    return g, idx + t0


def _expert_ffn(h, logits2, w_gate, w_up, w_down, layer, with_ctx, tile):
    logits = logits2[..., :N_EXPERTS] + logits2[..., N_EXPERTS:]
    g, idx = _route(logits, 0, T_LAT, EC_CAPACITY * T_LAT // N_EXPERTS)
    if with_ctx:
        gc, idxc = _route(logits, T_LAT, T_ALL, EC_CAPACITY * T_CTX // N_EXPERTS)
        g = jnp.concatenate([g, gc], axis=-1)
        idx = jnp.concatenate([idx, idxc], axis=-1)
    cap = idx.shape[-1]
    t_rows = h.shape[1]
    n_pairs = N_EXPERTS * cap
    assert n_pairs % PAIR_WIN == 0
    flat = (idx + jnp.arange(BATCH)[:, None, None] * t_rows).transpose(1, 0, 2).reshape(N_EXPERTS, -1)
    gates = g.transpose(1, 0, 2).reshape(N_EXPERTS, BATCH * cap, 1)
    xs = h.reshape(BATCH * t_rows, D_MODEL)[flat]
    y = _moe(xs, gates, w_gate, w_up, w_down, layer)
    src = (jnp.arange(N_EXPERTS)[None, :, None] * (BATCH * cap)
           + jnp.arange(BATCH)[:, None, None] * cap + jnp.arange(cap)[None, None, :])
    pair_tok, src = lax.sort((idx.reshape(BATCH, n_pairs), src.reshape(BATCH, n_pairs)),
                             dimension=1, num_keys=1)
    ys = y.reshape(-1, D_MODEL)[src.reshape(-1)]
    bounds = jnp.arange(0, t_rows + tile, tile)
    below = jnp.sum(pair_tok[:, :, None] < bounds[None, None, :], axis=1)
    spans = jnp.stack([below[:, :-1], below[:, 1:]], axis=-1).reshape(-1).astype(jnp.int32)
    return spans, pair_tok.reshape(BATCH, n_pairs // PAIR_WIN, PAIR_WIN), ys


def kernel(x, c, ctx, c_ctx, ada_w, ada_b, norm_mix, norm_ffn, norm_final, ev_w_in, ev_w_out,
           ret_gamma_logit, ret_norm, gmlp_norm, gmlp_ws, gmlp_bs, od_w_in, od_w_out, na_rpb,
           gla_w_up, gla_b_up, gla_norm, router_w, moe_w_gate, moe_w_up, moe_w_down):
    assert x.shape == (BATCH, T_LAT, D_MODEL) and ctx.shape == (BATCH, T_CTX, D_MODEL)
    assert ada_w.shape[0] == 2 and ev_w_in.shape[0] == 1 and od_w_in.shape[0] == 1

    cond = jnp.concatenate([jax.nn.silu(c), jax.nn.silu(c_ctx)[None],
                            jnp.zeros((11, D_MODEL), F32)], axis=0).astype(BF16)
    mod_lat, mod_ctx = [], []
    for l in range(2):
        m = _mod_matmul(cond, ada_w, l) + ada_b[l]
        m = m[:BATCH + 1].reshape(BATCH + 1, 6, 1, D_MODEL)
        mod_lat.append(m[:BATCH])
        mod_ctx.append(m[BATCH])

    cos, s1, s2 = _rope_tables()

    xa, h = _modnorm(x, ctx, norm_mix[0:1], mod_lat[0], mod_ctx[0], 1, 0)
    p = _proj_matmul(h.reshape(BATCH * T_ALL, D_MODEL), ev_w_in, 0, EVEN_COLS, 2176, 1024)
    p = p.reshape(BATCH, T_ALL, EVEN_COLS)
    log_g = jax.nn.log_sigmoid(ret_gamma_logit[0].astype(F32))
    ret = _retention(p, log_g, cos, s1, s2, ret_norm[0:1])
    gm = _gmlp(p, gmlp_norm[0:1], gmlp_ws[0].astype(BF16), gmlp_bs[0].T)
    xa, h2, logits = _out_proj(ret, gm, ev_w_out[0].astype(BF16), xa, mod_lat[0], mod_ctx[0],
                               norm_ffn[0:1], _router_split(router_w[0]), MIX_TILE,
                               T_ALL // MIX_TILE)
    spans, pair_tok, ys = _expert_ffn(h2, logits, moe_w_gate, moe_w_up, moe_w_down, 0, True,
                                      COMB_TILE)
    xa, h = _res_mid(xa, spans, pair_tok, ys, mod_lat[0], mod_ctx[0], mod_lat[1], mod_ctx[1],
                     norm_mix[1:2])

    h2d = h.reshape(BATCH * T_ALL, D_MODEL)
    p = _proj_matmul(h2d, od_w_in, 0, ODD_MAIN_COLS, 2176, 1024)
    p = p.reshape(BATCH, T_ALL, ODD_MAIN_COLS)
    w_lr = jnp.pad(od_w_in[0][:, ODD_MAIN_COLS:], ((0, 0), (0, LANE - 2 * GLA_GATE_RANK)))
    lr = _matmul(h2d, w_lr.astype(BF16), 2176, LANE, BF16).reshape(BATCH, T_ALL, LANE)
    o_na = _na(p, _na_bias_table(na_rpb[0]))
    w_up_full = jnp.zeros((2, LANE, GLA_QK_DIM), F32)
    for d in range(2):
        w_up_full = w_up_full.at[d, d * GLA_GATE_RANK:(d + 1) * GLA_GATE_RANK].set(gla_w_up[0, d])
    w_hi = w_up_full.astype(BF16)
    w_lo = (w_up_full - w_hi.astype(F32)).astype(BF16)
    n_pair = GLA_QK_DIM // LANE
    w_cat = jnp.stack([w_hi.reshape(2, LANE, n_pair, LANE), w_lo.reshape(2, LANE, n_pair, LANE)],
                      axis=3).reshape(2, LANE, 2 * GLA_QK_DIM)
    gla = _gla(p, lr, w_cat, gla_b_up[0][:, None, :], gla_norm[0:1])
    xl, h2, logits = _out_proj(o_na, gla, od_w_out[0].astype(BF16), xa, mod_lat[1], mod_ctx[1],
                               norm_ffn[1:2], _router_split(router_w[1]), LAT_TILE,
                               T_LAT // LAT_TILE)
    spans, pair_tok, ys = _expert_ffn(h2, logits, moe_w_gate, moe_w_up, moe_w_down, 1, False,
                                      COMB_TILE)
    return _res_final(xl, spans, pair_tok, ys, mod_lat[1], norm_final[None])
```

```python
import functools

import numpy as np
import jax
import jax.numpy as jnp
from jax import lax
from jax.experimental import pallas as pl
from jax.experimental.pallas import tpu as pltpu

D_MODEL = 2048
BATCH = 4
T_LAT = 4096
T_CTX = 256
T_ALL = T_LAT + T_CTX
GRID_W = 64
GRID_ROWS = T_LAT // GRID_W
HEAD_DIM = 128
RET_HEADS = 8
RET_DIM = 1024
GMLP_DIM = 1024
GMLP_GROUPS = 8
GMLP_CHUNK = 128
NA_HEADS = 8
NA_DIM = 1024
WIN_ROWS = 8
WIN_COLS = 16
GLA_HEADS = 8
GLA_DK = 64
GLA_DV = 128
GLA_QK_DIM = 512
GLA_V_DIM = 1024
GLA_GATE_RANK = 16
GLA_TAU = 16.0
N_EXPERTS = 16
D_EXPERT = 2048
EC_CAPACITY = 2
ROPE_BASE = 10000.0
RMS_EPS = 1e-6
EVEN_COLS = 6144
ODD_COLS = 6176
ODD_MAIN_COLS = 6144
LANE = 128

ROW_TILE = 256
MIX_TILE = 544
LAT_TILE = 512
RET_CHUNK = 256
GLA_SUB = 64
GLA_SUPER = 256
GLA_PAIRS = 2
NA_HEADS_PER_STEP = 4
RET_HEADS_PER_STEP = 2
RET_UNROLL = 4
GLA_UNROLL = 2
NA_QROWS = 4
NA_QT = NA_QROWS * GRID_W
NA_KROWS = NA_QROWS + WIN_ROWS - 1
NA_KT = NA_KROWS * GRID_W
NEG_BIG = -1e30
MOE_TN = 512
MOE_DOWN_TN = 1024
GATE_ROWS = 16
PAIR_WIN = 256
PAIR_CHUNK = 3
COMB_TILE = 256
VMEM_LIMIT = 56 * 1024 * 1024

F32 = jnp.float32
BF16 = jnp.bfloat16


def _cparams(n_axes):
    return pltpu.CompilerParams(dimension_semantics=("arbitrary",) * n_axes,
                                vmem_limit_bytes=VMEM_LIMIT)


def _dot(a, b):
    return jnp.dot(a, b, preferred_element_type=F32)


def _dot_nt(a, b):
    return lax.dot_general(a, b, (((1,), (1,)), ((), ())), preferred_element_type=F32)


def _dot_tn(a, b):
    return lax.dot_general(a, b, (((0,), (0,)), ((), ())), preferred_element_type=F32)


def _silu(x):
    return x * jax.nn.sigmoid(x)


def _rms(x):
    return x * lax.rsqrt(jnp.mean(x * x, axis=-1, keepdims=True) + RMS_EPS)


def _split2(x):
    hi = x.astype(BF16)
    lo = (x - hi.astype(F32)).astype(BF16)
    return hi, lo


def _proj_kernel(a_ref, w_ref, o_ref):
    o_ref[...] = _dot(a_ref[...], w_ref[...].astype(BF16)).astype(o_ref.dtype)


def _proj_matmul(a, w, layer, n_cols, tm, tn):
    m, k = a.shape
    return pl.pallas_call(
        _proj_kernel,
        grid=(n_cols // tn, m // tm),
        in_specs=[pl.BlockSpec((tm, k), lambda j, i: (i, 0)),
                  pl.BlockSpec((None, k, tn), lambda j, i: (layer, 0, j))],
        out_specs=pl.BlockSpec((tm, tn), lambda j, i: (i, j)),
        out_shape=jax.ShapeDtypeStruct((m, n_cols), BF16),
        compiler_params=_cparams(2),
        name="proj_matmul",
    )(a, w)


def _proj_t_kernel(a_ref, wt_ref, o_ref):
    o_ref[...] = _dot_nt(a_ref[...], wt_ref[...].astype(BF16)).astype(o_ref.dtype)


def _proj_matmul_t(a, wt, n_cols, tm, tn):
    m, k = a.shape
    return pl.pallas_call(
        _proj_t_kernel,
        grid=(n_cols // tn, m // tm),
        in_specs=[pl.BlockSpec((tm, k), lambda j, i: (i, 0)),
                  pl.BlockSpec((tn, k), lambda j, i: (j, 0))],
        out_specs=pl.BlockSpec((tm, tn), lambda j, i: (i, j)),
        out_shape=jax.ShapeDtypeStruct((m, n_cols), BF16),
        compiler_params=_cparams(2),
        name="proj_matmul_t",
    )(a, wt)


def _mod_mm_kernel(a_ref, w_ref, o_ref):
    o_ref[...] = _dot(a_ref[...], w_ref[...].astype(BF16))


def _mod_matmul(cond, ada_w, layer):
    m, k = cond.shape
    n = ada_w.shape[2]
    tn = 1024
    return pl.pallas_call(
        _mod_mm_kernel,
        grid=(n // tn,),
        in_specs=[pl.BlockSpec((m, k), lambda j: (0, 0)),
                  pl.BlockSpec((None, k, tn), lambda j: (layer, 0, j))],
        out_specs=pl.BlockSpec((m, tn), lambda j: (0, j)),
        out_shape=jax.ShapeDtypeStruct((m, n), F32),
        compiler_params=_cparams(1),
        name="mod_matmul",
    )(cond, ada_w)


def _lat_spec(which):
    return pl.BlockSpec((None, None, 1, D_MODEL), lambda b, i, *_: (b, which, 0, 0))


def _ctx_spec(which):
    return pl.BlockSpec((None, 1, D_MODEL), lambda b, i, *_: (which, 0, 0))


def _region_select(tile, lat_ref, ctx_ref, row0=0, n_rows=None, has_ctx=True):
    if not has_ctx:
        return lat_ref[...]
    n_rows = tile if n_rows is None else n_rows
    rows = pl.program_id(1) * tile + row0 + lax.broadcasted_iota(jnp.int32, (n_rows, 1), 0)
    return jnp.where(rows >= T_LAT, ctx_ref[...], lat_ref[...])


def _modnorm_kernel(x_ref, ctx_ref, gain_ref, scl_ref, scc_ref, shl_ref, shc_ref, xa_ref, o_ref,
                    *, tile):
    is_ctx = pl.program_id(1) >= T_LAT // tile
    x = jnp.where(is_ctx, ctx_ref[...], x_ref[...])
    xa_ref[...] = x
    sc = _region_select(tile, scl_ref, scc_ref)
    sh = _region_select(tile, shl_ref, shc_ref)
    o_ref[...] = (_rms(x) * gain_ref[...] * (1.0 + sc) + sh).astype(o_ref.dtype)


def _modnorm(x, ctx, gain, mod_lat, mod_ctx, sc_i, sh_i):
    tile = T_CTX
    n_lat = T_LAT // tile
    row = pl.BlockSpec((None, tile, D_MODEL), lambda b, i: (b, i, 0))
    return pl.pallas_call(
        functools.partial(_modnorm_kernel, tile=tile),
        grid=(BATCH, T_ALL // tile),
        in_specs=[pl.BlockSpec((None, tile, D_MODEL), lambda b, i: (b, jnp.minimum(i, n_lat - 1), 0)),
                  pl.BlockSpec((None, tile, D_MODEL), lambda b, i: (b, jnp.maximum(i - n_lat, 0), 0)),
                  pl.BlockSpec((1, D_MODEL), lambda b, i: (0, 0)),
                  _lat_spec(sc_i), _ctx_spec(sc_i), _lat_spec(sh_i), _ctx_spec(sh_i)],
        out_specs=[row, row],
        out_shape=[jax.ShapeDtypeStruct((BATCH, T_ALL, D_MODEL), F32),
                   jax.ShapeDtypeStruct((BATCH, T_ALL, D_MODEL), BF16)],
        compiler_params=_cparams(2),
        name="modnorm",
    )(x, ctx, gain, mod_lat, mod_ctx, mod_lat, mod_ctx)


def _rotary(z, cos, s1, s2):
    return z * cos + pltpu.roll(z, 96, 1) * s1 + pltpu.roll(z, 32, 1) * s2


def _ret_kernel(lg_ref, q_ref, g_ref, k_ref, v_ref, cos_ref, s1_ref, s2_ref, gain_ref,
                o_ref, acc_ref, qs_ref, ks_ref):
    c = RET_CHUNK
    nh = RET_HEADS_PER_STEP
    scale = HEAD_DIM ** -0.5
    dist = (lax.broadcasted_iota(jnp.int32, (c, c), 0)
            - lax.broadcasted_iota(jnp.int32, (c, c), 1)).astype(F32)
    tcol = lax.broadcasted_iota(jnp.int32, (c, 1), 0).astype(F32)

    def head_consts(hp):
        h = pl.program_id(1) * nh + hp
        la_f, la_b = lg_ref[0, h], lg_ref[1, h]
        dmat = (jnp.where(dist >= 0.0, jnp.exp(la_f * jnp.maximum(dist, 0.0)), 0.0)
                + jnp.where(dist <= 0.0, jnp.exp(la_b * jnp.maximum(-dist, 0.0)), 0.0))
        fwd_dec = (jnp.exp(la_f * (tcol + 1.0)), jnp.exp(la_f * (c - 1.0 - tcol)),
                   jnp.exp(jnp.full((1, LANE), la_f * c, F32)))
        bwd_dec = (jnp.exp(la_b * (c - tcol)), jnp.exp(la_b * tcol),
                   jnp.exp(jnp.full((1, LANE), la_b * c, F32)))
        return dmat, fwd_dec, bwd_dec

    consts = [head_consts(hp) for hp in range(nh)]

    def fwd(i, sts, base, rot):
        start = pl.multiple_of(base + i * c, c)
        rows = pl.ds(start, c)
        if rot:
            cos, s1, s2 = cos_ref[rows, :], s1_ref[rows, :], s2_ref[rows, :]
        out = []
        for hp in range(nh):
            hl = pl.ds(hp * LANE, LANE)
            dmat, (qdec, kdec, cdec), _ = consts[hp]
            q = q_ref[rows, hl].astype(F32) * scale
            k = k_ref[rows, hl].astype(F32)
            v = v_ref[rows, hl]
            if rot:
                q = _rotary(q, cos, s1, s2)
                k = _rotary(k, cos, s1, s2)
            qs_ref[rows, hl] = q
            ks_ref[rows, hl] = k
            att = _dot_nt(q.astype(BF16), k.astype(BF16)) * dmat
            st = sts[hp]
            acc_ref[rows, hl] = (_dot(att.astype(BF16), v)
                                 + _dot_nt((q * qdec).astype(BF16), st.astype(BF16)))
            out.append(cdec * st + _dot_tn(v, (k * kdec).astype(BF16)))
        return tuple(out)

    def bwd(i, sts, base, n):
        start = pl.multiple_of(base + (n - 1 - i) * c, c)
        rows = pl.ds(start, c)
        out = []
        for hp in range(nh):
            hl = pl.ds(hp * LANE, LANE)
            _, _, (qdec, kdec, cdec) = consts[hp]
            q = qs_ref[rows, hl]
            k = ks_ref[rows, hl]
            v = v_ref[rows, hl]
            st = sts[hp]
            o = acc_ref[rows, hl] + _dot_nt((q * qdec).astype(BF16), st.astype(BF16))
            gate = g_ref[rows, hl].astype(F32)
            o_ref[rows, hl] = (_rms(o) * gain_ref[:, hl] * _silu(gate)).astype(o_ref.dtype)
            out.append(cdec * st + _dot_tn(v, (k * kdec).astype(BF16)))
        return tuple(out)

    zero = tuple(jnp.zeros((HEAD_DIM, HEAD_DIM), F32) for _ in range(nh))
    n_ctx, n_lat = T_CTX // c, T_LAT // c
    sts = lax.fori_loop(0, n_ctx, functools.partial(fwd, base=T_LAT, rot=False), zero)
    lax.fori_loop(0, n_lat, functools.partial(fwd, base=0, rot=True), sts, unroll=RET_UNROLL)
    sts = lax.fori_loop(0, n_ctx, functools.partial(bwd, base=T_LAT, n=n_ctx), zero)
    lax.fori_loop(0, n_lat, functools.partial(bwd, base=0, n=n_lat), sts, unroll=RET_UNROLL)


def _retention(p, log_g, cos, s1, s2, ret_norm):
    nh = RET_HEADS_PER_STEP
    hw = nh * LANE
    col = lambda off: pl.BlockSpec((None, T_ALL, hw),
                                   lambda b, h, lg: (b, 0, off * LANE // hw + h))
    tab = pl.BlockSpec((T_LAT, LANE), lambda b, h, lg: (0, 0))
    return pl.pallas_call(
        _ret_kernel,
        grid_spec=pltpu.PrefetchScalarGridSpec(
            num_scalar_prefetch=1,
            grid=(BATCH, RET_HEADS // nh),
            in_specs=[col(0), col(8), col(32), col(40), tab, tab, tab,
                      pl.BlockSpec((1, hw), lambda b, h, lg: (0, h))],
            out_specs=pl.BlockSpec((None, T_ALL, hw), lambda b, h, lg: (b, 0, h)),
            scratch_shapes=[pltpu.VMEM((T_ALL, hw), F32), pltpu.VMEM((T_ALL, hw), F32),
                            pltpu.VMEM((T_ALL, hw), F32)]),
        out_shape=jax.ShapeDtypeStruct((BATCH, T_ALL, RET_DIM), BF16),
        compiler_params=_cparams(2),
        name="retention",
    )(log_g, p, p, p, p, cos, s1, s2, ret_norm)


def _gmlp_kernel(u_ref, v_ref, gain_ref, ws_ref, bst_ref, o_ref):
    u = jax.nn.gelu(u_ref[...].astype(F32))
    v = _rms(jax.nn.gelu(v_ref[...].astype(F32))) * gain_ref[...]
    vb = v.astype(BF16)
    for n in range(ROW_TILE // GMLP_CHUNK):
        r0 = n * GMLP_CHUNK
        for g in range(GMLP_GROUPS):
            c0 = g * LANE
            mixed = (_dot(ws_ref[g], vb[r0:r0 + GMLP_CHUNK, c0:c0 + LANE])
                     + bst_ref[:, g:g + 1])
            o_ref[r0:r0 + GMLP_CHUNK, c0:c0 + LANE] = (
                u[r0:r0 + GMLP_CHUNK, c0:c0 + LANE] * mixed).astype(o_ref.dtype)


def _gmlp(p, gmlp_norm, ws_bf16, bs_t):
    return pl.pallas_call(
        _gmlp_kernel,
        grid=(BATCH, T_ALL // ROW_TILE),
        in_specs=[pl.BlockSpec((None, ROW_TILE, GMLP_DIM), lambda b, i: (b, i, 2)),
                  pl.BlockSpec((None, ROW_TILE, GMLP_DIM), lambda b, i: (b, i, 3)),
                  pl.BlockSpec((1, GMLP_DIM), lambda b, i: (0, 0)),
                  pl.BlockSpec((GMLP_GROUPS, GMLP_CHUNK, GMLP_CHUNK), lambda b, i: (0, 0, 0)),
                  pl.BlockSpec((GMLP_CHUNK, GMLP_GROUPS), lambda b, i: (0, 0))],
        out_specs=pl.BlockSpec((None, ROW_TILE, GMLP_DIM), lambda b, i: (b, i, 0)),
        out_shape=jax.ShapeDtypeStruct((BATCH, T_ALL, GMLP_DIM), BF16),
        compiler_params=_cparams(2),
        name="gmlp",
    )(p, p, gmlp_norm, ws_bf16, bs_t)


def _na_kernel(q_ref, k_ref, v_ref, bias_ref, o_ref):
    i = pl.program_id(2)
    w0 = jnp.clip(i * NA_QROWS - WIN_ROWS // 2, 0, GRID_ROWS - NA_KROWS)
    start = pl.multiple_of(w0 * GRID_W, GRID_W)
    scale = HEAD_DIM ** -0.5
    for hp in range(NA_HEADS_PER_STEP):
        hl = pl.ds(hp * LANE, LANE)
        q = q_ref[:, hl]
        s_win = _dot_nt(q, k_ref[pl.ds(start, NA_KT), hl]) * scale + bias_ref[hp]
        s_ctx = _dot_nt(q, k_ref[pl.ds(T_LAT, T_CTX), hl]) * scale
        m = jnp.maximum(jnp.max(s_win, axis=-1, keepdims=True),
                        jnp.max(s_ctx, axis=-1, keepdims=True))
        p_win = jnp.exp(s_win - m)
        p_ctx = jnp.exp(s_ctx - m)
        denom = jnp.sum(p_win, axis=-1, keepdims=True) + jnp.sum(p_ctx, axis=-1, keepdims=True)
        o = (_dot(p_win.astype(BF16), v_ref[pl.ds(start, NA_KT), hl])
             + _dot(p_ctx.astype(BF16), v_ref[pl.ds(T_LAT, T_CTX), hl]))
        o_ref[:, hl] = (o / denom).astype(o_ref.dtype)


def _na_geometry():
    n_blk = GRID_ROWS // NA_QROWS
    dr00 = []
    row_ok = np.zeros((3, NA_QROWS, NA_KROWS), bool)
    for kind, i_rep in enumerate((0, 1, n_blk - 1)):
        r0 = i_rep * NA_QROWS
        w0 = int(np.clip(r0 - WIN_ROWS // 2, 0, GRID_ROWS - NA_KROWS))
        r = r0 + np.arange(NA_QROWS)[:, None]
        kr = w0 + np.arange(NA_KROWS)[None, :]
        rstart = np.clip(r - WIN_ROWS // 2, 0, GRID_ROWS - WIN_ROWS)
        row_ok[kind] = (kr >= rstart) & (kr < rstart + WIN_ROWS)
        dr00.append(w0 - r0 + WIN_ROWS - 1)
    c = np.arange(GRID_W)[:, None]
    kc = np.arange(GRID_W)[None, :]
    cstart = np.clip(c - WIN_COLS // 2, 0, GRID_W - WIN_COLS)
    col_ok = (kc >= cstart) & (kc < cstart + WIN_COLS)
    valid = row_ok[:, :, None, :, None] & col_ok[None, None, :, None, :]
    return dr00, valid.reshape(3, NA_QT, NA_KT)


def _na_bias_table(rpb):
    n_dr, n_dc = 2 * WIN_ROWS - 1, 2 * WIN_COLS - 1
    w = jnp.concatenate([rpb[..., WIN_COLS - 1:], jnp.zeros((NA_HEADS, n_dr, LANE - n_dc), F32),
                         rpb[..., :WIN_COLS - 1]], axis=-1)
    toep = jnp.tile(w, (1, 1, GRID_W))[..., :GRID_W * (LANE - 1)]
    toep = toep.reshape(NA_HEADS, n_dr, GRID_W, LANE - 1)[..., :GRID_W]
    dr00, valid = _na_geometry()
    front = max(0, NA_QROWS - 1 - min(dr00))
    back = max(0, max(dr00) + NA_KROWS - n_dr)
    toep = jnp.pad(toep.transpose(0, 2, 1, 3), ((0, 0), (0, 0), (front, back), (0, 0)))
    toep = toep.astype(BF16).reshape(NA_HEADS, GRID_W, -1)
    kinds = []
    for d0 in dr00:
        starts = [(d0 - rl + front) * GRID_W for rl in range(NA_QROWS)]
        kinds.append(jnp.stack([toep[:, :, s:s + NA_KT] for s in starts], axis=1))
    tab = jnp.stack(kinds, axis=0).reshape(3, NA_HEADS, NA_QT, NA_KT)
    return jnp.where(jnp.asarray(valid)[:, None], tab, jnp.asarray(NEG_BIG, BF16))


def _na(p, bias):
    n_blk = T_LAT // NA_QT
    nh = NA_HEADS_PER_STEP
    hw = nh * LANE
    k0, v0 = 20 * LANE // hw, 28 * LANE // hw
    kind = lambda i: jnp.where(i == 0, 0, jnp.where(i == n_blk - 1, 2, 1))
    return pl.pallas_call(
        _na_kernel,
        grid=(BATCH, NA_HEADS // nh, n_blk),
        in_specs=[pl.BlockSpec((None, NA_QT, hw), lambda b, h, i: (b, i, h)),
                  pl.BlockSpec((None, T_ALL, hw), lambda b, h, i: (b, 0, k0 + h)),
                  pl.BlockSpec((None, T_ALL, hw), lambda b, h, i: (b, 0, v0 + h)),
                  pl.BlockSpec((None, nh, NA_QT, NA_KT), lambda b, h, i: (kind(i), h, 0, 0))],
        out_specs=pl.BlockSpec((None, NA_QT, hw), lambda b, h, i: (b, i, h)),
        out_shape=jax.ShapeDtypeStruct((BATCH, T_LAT, NA_DIM), BF16),
        compiler_params=_cparams(3),
        name="nbr_attention",
    )(p, p, p, bias)


def _gla_kernel(q_ref, k_ref, v_ref, g_ref, lr_ref, wcat_ref, bup_ref, gain_ref,
                o_ref, acc_ref):
    r, c = GLA_SUPER, GLA_SUB
    n_sub = r // c
    dv2 = 2 * GLA_DV
    qscale = GLA_DK ** -0.5
    row = lax.broadcasted_iota(jnp.int32, (r, r), 0)
    col = lax.broadcasted_iota(jnp.int32, (r, r), 1)
    sub_shift = c.bit_length() - 1
    same = (row >> sub_shift) == (col >> sub_shift)
    blk_ones = jnp.where(same, 1.0, 0.0).astype(BF16)
    second_head_lane = lax.broadcasted_iota(jnp.int32, (1, LANE), 1) >= GLA_DK
    st_mask = ((lax.broadcasted_iota(jnp.int32, (dv2, LANE), 0) >= GLA_DV)
               == (lax.broadcasted_iota(jnp.int32, (dv2, LANE), 1) >= GLA_DK))

    for d in range(2):
        tri = same & ((row >= col) if d == 0 else (col >= row))
        tri_b = jnp.where(tri, 1.0, 0.0).astype(BF16)

        def pair_step(hp, start, lr, st, d=d, tri=tri, tri_b=tri_b):
            kl = pl.ds(hp * LANE, LANE)
            vl = pl.ds(hp * dv2, dv2)
            zz = _dot(lr, wcat_ref[d, :, pl.ds(hp * 2 * LANE, 2 * LANE)])
            z = zz[:, :LANE] + zz[:, LANE:] + bup_ref[d, :, kl]
            la = (jnp.minimum(z, 0.0) - jnp.log(1.0 + jnp.exp(-jnp.abs(z)))) * (1.0 / GLA_TAU)
            hl = jnp.concatenate(_split2(la), axis=1)
            bb = _dot(tri_b, hl)
            b = bb[:, :LANE] + bb[:, LANE:]
            bb = _dot(blk_ones, hl)
            bt = bb[:, :LANE] + bb[:, LANE:]
            q = q_ref[pl.ds(start, r), kl].astype(F32) * qscale
            k = k_ref[pl.ds(start, r), kl].astype(F32)
            v = v_ref[pl.ds(start, r), vl]
            qi = q * jnp.exp(b)
            qib = qi.astype(BF16)
            ki = (k * jnp.exp(-b)).astype(BF16)
            ko = (k * jnp.exp(bt - b)).astype(BF16)
            dec = jnp.exp(bt)
            outs = []
            for j in range(2):
                qm = jnp.where(second_head_lane == (j == 1), qi, 0.0).astype(BF16)
                att = jnp.where(tri, _dot_nt(qm, ki), 0.0)
                outs.append(_dot(att.astype(BF16), v[:, j * GLA_DV:(j + 1) * GLA_DV]))
            o_intra = jnp.concatenate(outs, axis=1)
            parts = [None] * n_sub
            for s in (range(n_sub) if d == 0 else reversed(range(n_sub))):
                lo_r, hi_r = s * c, (s + 1) * c
                parts[s] = o_intra[lo_r:hi_r] + _dot_nt(qib[lo_r:hi_r], st.astype(BF16))
                st = (st * dec[lo_r:lo_r + 1, :]
                      + jnp.where(st_mask, _dot_tn(v[lo_r:hi_r], ko[lo_r:hi_r]), 0.0))
            o = jnp.concatenate(parts, axis=0)
            if d == 0:
                acc_ref[pl.ds(start, r), vl] = o
            else:
                o = acc_ref[pl.ds(start, r), vl] + o
                gate = g_ref[pl.ds(start, r), vl].astype(F32)
                on = (jnp.concatenate([_rms(o[:, :GLA_DV]), _rms(o[:, GLA_DV:])], axis=1)
                      * gain_ref[:, vl])
                o_ref[pl.ds(start, r), vl] = (on * _silu(gate)).astype(o_ref.dtype)
            return st

        def step(i, sts, base, n, d=d):
            idx = i if d == 0 else n - 1 - i
            start = pl.multiple_of(base + idx * r, r)
            lr = lr_ref[pl.ds(start, r), :]
            return tuple(pair_step(hp, start, lr, sts[hp]) for hp in range(GLA_PAIRS))

        sts = tuple(jnp.zeros((dv2, LANE), F32) for _ in range(GLA_PAIRS))
        sts = lax.fori_loop(0, T_CTX // r, functools.partial(step, base=T_LAT, n=T_CTX // r), sts)
        lax.fori_loop(0, T_LAT // r, functools.partial(step, base=0, n=T_LAT // r), sts,
                      unroll=GLA_UNROLL)


def _gla(p, lr, w_cat, b_up, gla_norm):
    kw = GLA_PAIRS * LANE
    vw = GLA_PAIRS * 2 * GLA_DV
    q0, k0 = 8 * LANE // kw, 36 * LANE // kw
    v0, g0 = 40 * LANE // vw, 12 * LANE // vw
    return pl.pallas_call(
        _gla_kernel,
        grid=(BATCH, GLA_HEADS // (2 * GLA_PAIRS)),
        in_specs=[pl.BlockSpec((None, T_ALL, kw), lambda b, j: (b, 0, q0 + j)),
                  pl.BlockSpec((None, T_ALL, kw), lambda b, j: (b, 0, k0 + j)),
                  pl.BlockSpec((None, T_ALL, vw), lambda b, j: (b, 0, v0 + j)),
                  pl.BlockSpec((None, T_ALL, vw), lambda b, j: (b, 0, g0 + j)),
                  pl.BlockSpec((None, T_ALL, LANE), lambda b, j: (b, 0, 0)),
                  pl.BlockSpec((2, LANE, 2 * kw), lambda b, j: (0, 0, j)),
                  pl.BlockSpec((2, 1, kw), lambda b, j: (0, 0, j)),
                  pl.BlockSpec((1, vw), lambda b, j: (0, j))],
        out_specs=pl.BlockSpec((None, T_ALL, vw), lambda b, j: (b, 0, j)),
        out_shape=jax.ShapeDtypeStruct((BATCH, T_ALL, GLA_V_DIM), BF16),
        scratch_shapes=[pltpu.VMEM((T_ALL, vw), F32)],
        compiler_params=_cparams(2),
        name="gla",
    )(p, p, p, p, lr, w_cat, b_up, gla_norm)


def _out_kernel(a1_ref, a2_ref, w1_ref, w2_ref, x_ref, gl_ref, gc_ref, gain_ref,
                scl_ref, scc_ref, shl_ref, shc_ref, rw_ref, xo_ref, h_ref, lg_ref, *, tile,
                has_ctx):
    half = tile // 2
    for r0 in (0, half):
        rows = pl.ds(r0, half)
        sel = functools.partial(_region_select, tile, row0=r0, n_rows=half, has_ctx=has_ctx)
        y = _dot(a1_ref[rows, :], w1_ref[...]) + _dot(a2_ref[rows, :], w2_ref[...])
        xn = x_ref[rows, :] + sel(gl_ref, gc_ref) * y
        xo_ref[rows, :] = xn
        h = _rms(xn) * gain_ref[...] * (1.0 + sel(scl_ref, scc_ref)) + sel(shl_ref, shc_ref)
        h_hi = h.astype(BF16)
        h_ref[rows, :] = h_hi
        h_lo = (h - h_hi.astype(F32)).astype(BF16)
        lg_ref[rows, :] = _dot(h_hi, rw_ref[...]) + _dot(h_lo, rw_ref[...])


def _out_proj(a1, a2, w_out_bf16, x, mod_lat, mod_ctx, norm_gain, rw_split, tile, n_tiles):
    half = w_out_bf16.shape[0] // 2
    t_out = n_tiles * tile
    row = lambda width: pl.BlockSpec((None, tile, width), lambda b, i: (b, i, 0))
    return pl.pallas_call(
        functools.partial(_out_kernel, tile=tile, has_ctx=t_out > T_LAT),
        grid=(BATCH, n_tiles),
        in_specs=[row(half), row(half),
                  pl.BlockSpec((half, D_MODEL), lambda b, i: (0, 0)),
                  pl.BlockSpec((half, D_MODEL), lambda b, i: (1, 0)),
                  row(D_MODEL), _lat_spec(2), _ctx_spec(2),
                  pl.BlockSpec((1, D_MODEL), lambda b, i: (0, 0)),
                  _lat_spec(4), _ctx_spec(4), _lat_spec(3), _ctx_spec(3),
                  pl.BlockSpec((D_MODEL, 2 * N_EXPERTS), lambda b, i: (0, 0))],
        out_specs=[row(D_MODEL), row(D_MODEL), row(2 * N_EXPERTS)],
        out_shape=[jax.ShapeDtypeStruct((BATCH, t_out, D_MODEL), F32),
                   jax.ShapeDtypeStruct((BATCH, t_out, D_MODEL), BF16),
                   jax.ShapeDtypeStruct((BATCH, t_out, 2 * N_EXPERTS), F32)],
        compiler_params=_cparams(2),
        name="out_proj",
    )(a1, a2, w_out_bf16, w_out_bf16, x, mod_lat, mod_ctx, norm_gain,
      mod_lat, mod_ctx, mod_lat, mod_ctx, rw_split)


def _moe_up_kernel(x_ref, wg_ref, wu_ref, o_ref):
    x = x_ref[...]
    a = _dot(x, wg_ref[...].astype(BF16))
    u = _dot(x, wu_ref[...].astype(BF16))
    o_ref[...] = (_silu(a) * u).astype(o_ref.dtype)


def _moe_down_kernel(h_ref, wd_ref, g_ref, o_ref):
    gcol = _dot_tn(g_ref[...], jnp.ones((GATE_ROWS, LANE), BF16))
    y = _dot(h_ref[...], wd_ref[...].astype(BF16))
    for c in range(o_ref.shape[1] // LANE):
        cols = pl.ds(c * LANE, LANE)
        o_ref[:, cols] = (y[:, c * LANE:(c + 1) * LANE] * gcol).astype(o_ref.dtype)


def _moe(xs, gates, w_gate, w_up, w_down, layer):
    e, m, _ = xs.shape
    tn = MOE_TN
    wspec = pl.BlockSpec((None, None, D_MODEL, tn), lambda e, f: (layer, e, 0, f))
    hmid = pl.pallas_call(
        _moe_up_kernel,
        grid=(e, D_EXPERT // tn),
        in_specs=[pl.BlockSpec((None, m, D_MODEL), lambda e, f: (e, 0, 0)), wspec, wspec],
        out_specs=pl.BlockSpec((None, m, tn), lambda e, f: (e, 0, f)),
        out_shape=jax.ShapeDtypeStruct((e, m, D_EXPERT), BF16),
        compiler_params=_cparams(2),
        name="moe_up",
    )(xs, w_gate, w_up)
    tn = MOE_DOWN_TN
    return pl.pallas_call(
        _moe_down_kernel,
        grid=(e, D_MODEL // tn),
        in_specs=[pl.BlockSpec((None, m, D_EXPERT), lambda e, f: (e, 0, 0)),
                  pl.BlockSpec((None, None, D_EXPERT, tn), lambda e, f: (layer, e, 0, f)),
                  pl.BlockSpec((None, GATE_ROWS, m), lambda e, f: (e, 0, 0))],
        out_specs=pl.BlockSpec((None, m, tn), lambda e, f: (e, 0, f)),
        out_shape=jax.ShapeDtypeStruct((e, m, D_MODEL), BF16),
        compiler_params=_cparams(2),
        name="moe_down",
    )(hmid, w_down, gates)


def _combine_moe(span_ref, pair_tok_ref, ys_hbm, acc_ref, buf_ref, sem, *, tile, n_pairs):
    b, i = pl.program_id(0), pl.program_id(1)
    n_tiles = pl.num_programs(1)
    step = b * n_tiles + i
    n_steps = pl.num_programs(0) * n_tiles
    w = PAIR_WIN
    shift = w.bit_length() - 1
    last_start = n_pairs // w - PAIR_CHUNK

    def first_window(s):
        return jnp.minimum(span_ref[2 * s] >> shift, last_start)

    def chunk_copy(sample, win, slot):
        return pltpu.make_async_copy(
            ys_hbm.at[pl.ds(sample * n_pairs + win * w, PAIR_CHUNK * w), :],
            buf_ref.at[slot], sem.at[slot])

    def window_copy(win, slot):
        return pltpu.make_async_copy(ys_hbm.at[pl.ds(b * n_pairs + win * w, w), :],
                                     buf_ref.at[slot, pl.ds(0, w), :], sem.at[slot])

    slot = step & 1
    w0 = first_window(step)

    @pl.when(step == 0)
    def _():
        chunk_copy(b, w0, slot).start()

    @pl.when(step + 1 < n_steps)
    def _():
        next_sample = jnp.where(i + 1 < n_tiles, b, b + 1)
        chunk_copy(next_sample, first_window(step + 1), 1 - slot).start()

    chunk_copy(b, w0, slot).wait()
    tok = i * tile + lax.broadcasted_iota(jnp.int32, (tile, 1), 0)

    def select(win):
        return jnp.where(tok == pair_tok_ref[pl.ds(win, 1), :], 1.0, 0.0).astype(BF16)

    sel = jnp.concatenate([select(w0 + c) for c in range(PAIR_CHUNK)], axis=1)
    acc_ref[...] = _dot(sel, buf_ref[slot])

    hi = span_ref[2 * step + 1]
    end_win = jnp.where(hi > 0, ((hi - 1) >> shift) + 1, 0)

    def body(win, carry):
        cp = window_copy(win, slot)
        cp.start()
        cp.wait()
        acc_ref[...] += _dot(select(win), buf_ref[slot, pl.ds(0, w), :])
        return carry

    lax.fori_loop(w0 + PAIR_CHUNK, end_win, body, 0)
    return acc_ref[...]


def _res_mid_kernel(span_ref, x_ref, pair_tok_ref, ys_hbm, gl_ref, gc_ref, gain_ref,
                    scl_ref, scc_ref, shl_ref, shc_ref, xo_ref, h_ref, acc_ref, buf_ref, sem,
                    *, tile, n_pairs):
    moe = _combine_moe(span_ref, pair_tok_ref, ys_hbm, acc_ref, buf_ref, sem,
                       tile=tile, n_pairs=n_pairs)
    xn = x_ref[...] + _region_select(tile, gl_ref, gc_ref) * moe
    xo_ref[...] = xn
    h_ref[...] = (_rms(xn) * gain_ref[...] * (1.0 + _region_select(tile, scl_ref, scc_ref))
                  + _region_select(tile, shl_ref, shc_ref)).astype(h_ref.dtype)


def _res_final_kernel(span_ref, x_ref, pair_tok_ref, ys_hbm, g_ref, gain_ref, o_ref,
                      acc_ref, buf_ref, sem, *, tile, n_pairs):
    moe = _combine_moe(span_ref, pair_tok_ref, ys_hbm, acc_ref, buf_ref, sem,
                       tile=tile, n_pairs=n_pairs)
    o_ref[...] = _rms(x_ref[...] + g_ref[...] * moe) * gain_ref[...]


def _combine_call(kernel_fn, name, tile, t_rows, spans, x, pair_tok, ys, params, param_specs,
                  out_dtypes):
    n_pairs = pair_tok.shape[1] * PAIR_WIN
    row = pl.BlockSpec((None, tile, D_MODEL), lambda b, i, *_: (b, i, 0))
    return pl.pallas_call(
        functools.partial(kernel_fn, tile=tile, n_pairs=n_pairs),
        grid_spec=pltpu.PrefetchScalarGridSpec(
            num_scalar_prefetch=1,
            grid=(BATCH, t_rows // tile),
            in_specs=[row,
                      pl.BlockSpec((None,) + pair_tok.shape[1:], lambda b, i, *_: (b, 0, 0)),
                      pl.BlockSpec(memory_space=pl.ANY)] + param_specs,
            out_specs=[row] * len(out_dtypes),
            scratch_shapes=[pltpu.VMEM((tile, D_MODEL), F32),
                            pltpu.VMEM((2, PAIR_CHUNK * PAIR_WIN, D_MODEL), BF16),
                            pltpu.SemaphoreType.DMA((2,))]),
        out_shape=[jax.ShapeDtypeStruct((BATCH, t_rows, D_MODEL), dt) for dt in out_dtypes],
        compiler_params=_cparams(2),
        name=name,
    )(spans, x, pair_tok, ys, *params)


def _res_mid(x, spans, pair_tok, ys, mod_lat, mod_ctx, next_lat, next_ctx, norm_gain_next):
    specs = [_lat_spec(5), _ctx_spec(5), pl.BlockSpec((1, D_MODEL), lambda b, i, *_: (0, 0)),
             _lat_spec(1), _ctx_spec(1), _lat_spec(0), _ctx_spec(0)]
    params = (mod_lat, mod_ctx, norm_gain_next, next_lat, next_ctx, next_lat, next_ctx)
    return _combine_call(_res_mid_kernel, "ffn_combine_residual_prenorm", COMB_TILE, T_ALL, spans,
                         x, pair_tok, ys, params, specs, (F32, BF16))


def _res_final(x, spans, pair_tok, ys, mod_lat, norm_final):
    specs = [_lat_spec(5), pl.BlockSpec((1, D_MODEL), lambda b, i, *_: (0, 0))]
    return _combine_call(_res_final_kernel, "ffn_combine_residual_final_norm", COMB_TILE, T_LAT,
                         spans, x, pair_tok, ys, (mod_lat, norm_final), specs, (F32,))[0]


def _rope_tables():
    half = HEAD_DIM // 2
    nf = half // 2
    t = np.arange(T_LAT)
    rows, cols = t // GRID_W, t % GRID_W
    freq = (np.float32(ROPE_BASE) ** (-np.arange(nf, dtype=np.float32) / np.float32(nf))).astype(np.float32)
    ang_r = rows.astype(np.float32)[:, None] * freq
    ang_c = cols.astype(np.float32)[:, None] * freq
    zeros = np.zeros_like(ang_r)
    cos = np.concatenate([np.cos(ang_r), np.cos(ang_r), np.cos(ang_c), np.cos(ang_c)], axis=1)
    s1 = np.concatenate([-np.sin(ang_r), zeros, -np.sin(ang_c), zeros], axis=1)
    s2 = np.concatenate([zeros, np.sin(ang_r), zeros, np.sin(ang_c)], axis=1)
    return tuple(jnp.asarray(a, F32) for a in (cos, s1, s2))


def _router_split(rw):
    hi = rw.astype(BF16)
    lo = (rw - hi.astype(F32)).astype(BF16)
    return jnp.concatenate([hi, lo], axis=1)


def _route(logits, t0, t1, cap):
    aff = jax.nn.softmax(logits[:, t0:t1], axis=-1)
    g, idx = lax.top_k(aff.transpose(0, 2, 1), cap)
    return g, idx + t0


def _expert_ffn(h, logits2, w_gate, w_up, w_down, layer, with_ctx, tile):
    logits = logits2[..., :N_EXPERTS] + logits2[..., N_EXPERTS:]
    g, idx = _route(logits, 0, T_LAT, EC_CAPACITY * T_LAT // N_EXPERTS)
    if with_ctx:
        gc, idxc = _route(logits, T_LAT, T_ALL, EC_CAPACITY * T_CTX // N_EXPERTS)
        g = jnp.concatenate([g, gc], axis=-1)
        idx = jnp.concatenate([idx, idxc], axis=-1)
    cap = idx.shape[-1]
    t_rows = h.shape[1]
    n_pairs = N_EXPERTS * cap
    assert n_pairs % PAIR_WIN == 0
    flat = (idx + jnp.arange(BATCH)[:, None, None] * t_rows).transpose(1, 0, 2).reshape(N_EXPERTS, -1)
    g_rows = g.transpose(1, 0, 2).reshape(N_EXPERTS, 1, BATCH * cap)
    g_hi = g_rows.astype(BF16)
    g_mid = (g_rows - g_hi.astype(F32)).astype(BF16)
    g_lo = (g_rows - g_hi.astype(F32) - g_mid.astype(F32)).astype(BF16)
    gates = jnp.concatenate([g_hi, g_mid, g_lo,
                             jnp.zeros((N_EXPERTS, GATE_ROWS - 3, BATCH * cap), BF16)], axis=1)
    xs = h.reshape(BATCH * t_rows, D_MODEL)[flat]
    y = _moe(xs, gates, w_gate, w_up, w_down, layer)
    src = (jnp.arange(N_EXPERTS)[None, :, None] * (BATCH * cap)
           + jnp.arange(BATCH)[:, None, None] * cap + jnp.arange(cap)[None, None, :])
    pair_tok, src = lax.sort((idx.reshape(BATCH, n_pairs), src.reshape(BATCH, n_pairs)),
                             dimension=1, num_keys=1)
    ys = y.reshape(-1, D_MODEL)[src.reshape(-1)]
    bounds = jnp.arange(0, t_rows + tile, tile)
    below = jnp.sum(pair_tok[:, :, None] < bounds[None, None, :], axis=1)
    spans = jnp.stack([below[:, :-1], below[:, 1:]], axis=-1).reshape(-1).astype(jnp.int32)
    return spans, pair_tok.reshape(BATCH, n_pairs // PAIR_WIN, PAIR_WIN), ys


def kernel(x, c, ctx, c_ctx, ada_w, ada_b, norm_mix, norm_ffn, norm_final, ev_w_in, ev_w_out,
           ret_gamma_logit, ret_norm, gmlp_norm, gmlp_ws, gmlp_bs, od_w_in, od_w_out, na_rpb,
           gla_w_up, gla_b_up, gla_norm, router_w, moe_w_gate, moe_w_up, moe_w_down):
    assert x.shape == (BATCH, T_LAT, D_MODEL) and ctx.shape == (BATCH, T_CTX, D_MODEL)
    assert ada_w.shape[0] == 2 and ev_w_in.shape[0] == 1 and od_w_in.shape[0] == 1

    cond = jnp.concatenate([jax.nn.silu(c), jax.nn.silu(c_ctx)[None],
                            jnp.zeros((11, D_MODEL), F32)], axis=0).astype(BF16)
    mod_lat, mod_ctx = [], []
    for l in range(2):
        m = _mod_matmul(cond, ada_w, l) + ada_b[l]
        m = m[:BATCH + 1].reshape(BATCH + 1, 6, 1, D_MODEL)
        mod_lat.append(m[:BATCH])
        mod_ctx.append(m[BATCH])

    cos, s1, s2 = _rope_tables()

    xa, h = _modnorm(x, ctx, norm_mix[0:1], mod_lat[0], mod_ctx[0], 1, 0)
    p = _proj_matmul(h.reshape(BATCH * T_ALL, D_MODEL), ev_w_in, 0, EVEN_COLS, 2176, 1024)
    p = p.reshape(BATCH, T_ALL, EVEN_COLS)
    log_g = jax.nn.log_sigmoid(ret_gamma_logit[0].astype(F32))
    ret = _retention(p, log_g, cos, s1, s2, ret_norm[0:1])
    gm = _gmlp(p, gmlp_norm[0:1], gmlp_ws[0].astype(BF16), gmlp_bs[0].T)
    xa, h2, logits = _out_proj(ret, gm, ev_w_out[0].astype(BF16), xa, mod_lat[0], mod_ctx[0],
                               norm_ffn[0:1], _router_split(router_w[0]), MIX_TILE,
                               T_ALL // MIX_TILE)
    spans, pair_tok, ys = _expert_ffn(h2, logits, moe_w_gate, moe_w_up, moe_w_down, 0, True,
                                      COMB_TILE)
    xa, h = _res_mid(xa, spans, pair_tok, ys, mod_lat[0], mod_ctx[0], mod_lat[1], mod_ctx[1],
                     norm_mix[1:2])

    h2d = h.reshape(BATCH * T_ALL, D_MODEL)
    w_in_t = od_w_in[0].T
    p = _proj_matmul_t(h2d, w_in_t, ODD_MAIN_COLS, 2176, 1024)
    p = p.reshape(BATCH, T_ALL, ODD_MAIN_COLS)
    w_lr_t = jnp.pad(w_in_t[ODD_MAIN_COLS:], ((0, LANE - 2 * GLA_GATE_RANK), (0, 0))).astype(BF16)
    lr = _proj_matmul_t(h2d, w_lr_t, LANE, 2176, LANE).reshape(BATCH, T_ALL, LANE)
    o_na = _na(p, _na_bias_table(na_rpb[0]))
    w_up_full = jnp.zeros((2, LANE, GLA_QK_DIM), F32)
    for d in range(2):
        w_up_full = w_up_full.at[d, d * GLA_GATE_RANK:(d + 1) * GLA_GATE_RANK].set(gla_w_up[0, d])
    w_hi = w_up_full.astype(BF16)
    w_lo = (w_up_full - w_hi.astype(F32)).astype(BF16)
    n_pair = GLA_QK_DIM // LANE
    w_cat = jnp.stack([w_hi.reshape(2, LANE, n_pair, LANE), w_lo.reshape(2, LANE, n_pair, LANE)],
                      axis=3).reshape(2, LANE, 2 * GLA_QK_DIM)
    gla = _gla(p, lr, w_cat, gla_b_up[0][:, None, :], gla_norm[0:1])
    xl, h2, logits = _out_proj(o_na, gla, od_w_out[0].astype(BF16), xa, mod_lat[1], mod_ctx[1],
                               norm_ffn[1:2], _router_split(router_w[1]), LAT_TILE,
                               T_LAT // LAT_TILE)
    spans, pair_tok, ys = _expert_ffn(h2, logits, moe_w_gate, moe_w_up, moe_w_down, 1, False,
                                      COMB_TILE)
    return _res_final(xl, spans, pair_tok, ys, mod_lat[1], norm_final[None])
```

```python
import functools

import numpy as np
import jax
import jax.numpy as jnp
from jax import lax
from jax.experimental import pallas as pl
from jax.experimental.pallas import tpu as pltpu

D_MODEL = 2048
BATCH = 4
T_LAT = 4096
T_CTX = 256
T_ALL = T_LAT + T_CTX
GRID_W = 64
GRID_ROWS = T_LAT // GRID_W
HEAD_DIM = 128
RET_HEADS = 8
RET_DIM = 1024
GMLP_DIM = 1024
GMLP_GROUPS = 8
GMLP_CHUNK = 128
NA_HEADS = 8
NA_DIM = 1024
WIN_ROWS = 8
WIN_COLS = 16
GLA_HEADS = 8
GLA_DK = 64
GLA_DV = 128
GLA_QK_DIM = 512
GLA_V_DIM = 1024
GLA_GATE_RANK = 16
GLA_TAU = 16.0
N_EXPERTS = 16
D_EXPERT = 2048
EC_CAPACITY = 2
ROPE_BASE = 10000.0
RMS_EPS = 1e-6
EVEN_COLS = 6144
ODD_COLS = 6176
ODD_MAIN_COLS = 6144
LANE = 128

ROW_TILE = 256
MIX_TILE = 544
LAT_TILE = 512
RET_CHUNK = 256
GLA_SUB = 64
GLA_SUPER = 256
GLA_PAIRS = 2
NA_HEADS_PER_STEP = 4
RET_HEADS_PER_STEP = 2
RET_UNROLL = 4
GLA_UNROLL = 2
NA_QROWS = 4
NA_QT = NA_QROWS * GRID_W
NA_KROWS = NA_QROWS + WIN_ROWS - 1
NA_KT = NA_KROWS * GRID_W
NEG_BIG = -1e30
MOE_TN = 512
MOE_DOWN_TN = 1024
GATE_ROWS = 16
PAIR_WIN = 256
PAIR_CHUNK = 3
COMB_TILE = 256
VMEM_LIMIT = 56 * 1024 * 1024

F32 = jnp.float32
BF16 = jnp.bfloat16


def _cparams(n_axes):
    return pltpu.CompilerParams(dimension_semantics=("arbitrary",) * n_axes,
                                vmem_limit_bytes=VMEM_LIMIT)


def _dot(a, b):
    return jnp.dot(a, b, preferred_element_type=F32)


def _dot_nt(a, b):
    return lax.dot_general(a, b, (((1,), (1,)), ((), ())), preferred_element_type=F32)


def _dot_tn(a, b):
    return lax.dot_general(a, b, (((0,), (0,)), ((), ())), preferred_element_type=F32)


def _silu(x):
    return x * jax.nn.sigmoid(x)


def _rms(x):
    return x * lax.rsqrt(jnp.mean(x * x, axis=-1, keepdims=True) + RMS_EPS)


def _split2(x):
    hi = x.astype(BF16)
    lo = (x - hi.astype(F32)).astype(BF16)
    return hi, lo


def _proj_kernel(a_ref, w_ref, o_ref):
    o_ref[...] = _dot(a_ref[...], w_ref[...].astype(BF16)).astype(o_ref.dtype)


def _proj_matmul(a, w, layer, n_cols, tm, tn):
    m, k = a.shape
    return pl.pallas_call(
        _proj_kernel,
        grid=(n_cols // tn, m // tm),
        in_specs=[pl.BlockSpec((tm, k), lambda j, i: (i, 0)),
                  pl.BlockSpec((None, k, tn), lambda j, i: (layer, 0, j))],
        out_specs=pl.BlockSpec((tm, tn), lambda j, i: (i, j)),
        out_shape=jax.ShapeDtypeStruct((m, n_cols), BF16),
        compiler_params=_cparams(2),
        name="proj_matmul",
    )(a, w)


def _proj_t_kernel(a_ref, wt_ref, o_ref):
    o_ref[...] = _dot_nt(a_ref[...], wt_ref[...].astype(BF16)).astype(o_ref.dtype)


def _proj_matmul_t(a, wt, n_cols, tm, tn):
    m, k = a.shape
    return pl.pallas_call(
        _proj_t_kernel,
        grid=(n_cols // tn, m // tm),
        in_specs=[pl.BlockSpec((tm, k), lambda j, i: (i, 0)),
                  pl.BlockSpec((tn, k), lambda j, i: (j, 0))],
        out_specs=pl.BlockSpec((tm, tn), lambda j, i: (i, j)),
        out_shape=jax.ShapeDtypeStruct((m, n_cols), BF16),
        compiler_params=_cparams(2),
        name="proj_matmul_t",
    )(a, wt)


def _mod_mm_kernel(a_ref, w_ref, o_ref):
    o_ref[...] = _dot(a_ref[...], w_ref[...].astype(BF16))


def _mod_matmul(cond, ada_w, layer):
    m, k = cond.shape
    n = ada_w.shape[2]
    tn = 1024
    return pl.pallas_call(
        _mod_mm_kernel,
        grid=(n // tn,),
        in_specs=[pl.BlockSpec((m, k), lambda j: (0, 0)),
                  pl.BlockSpec((None, k, tn), lambda j: (layer, 0, j))],
        out_specs=pl.BlockSpec((m, tn), lambda j: (0, j)),
        out_shape=jax.ShapeDtypeStruct((m, n), F32),
        compiler_params=_cparams(1),
        name="mod_matmul",
    )(cond, ada_w)


def _lat_spec(which):
    return pl.BlockSpec((None, None, 1, D_MODEL), lambda b, i, *_: (b, which, 0, 0))


def _ctx_spec(which):
    return pl.BlockSpec((None, 1, D_MODEL), lambda b, i, *_: (which, 0, 0))


def _region_select(tile, lat_ref, ctx_ref, row0=0, n_rows=None, has_ctx=True):
    if not has_ctx:
        return lat_ref[...]
    if T_LAT % tile == 0:
        return jnp.where(pl.program_id(1) >= T_LAT // tile, ctx_ref[...], lat_ref[...])
    n_rows = tile if n_rows is None else n_rows
    rows = pl.program_id(1) * tile + row0 + lax.broadcasted_iota(jnp.int32, (n_rows, 1), 0)
    return jnp.where(rows >= T_LAT, ctx_ref[...], lat_ref[...])


def _modnorm_kernel(x_ref, ctx_ref, gain_ref, scl_ref, scc_ref, shl_ref, shc_ref, xa_ref, o_ref,
                    *, tile):
    is_ctx = pl.program_id(1) >= T_LAT // tile
    x = jnp.where(is_ctx, ctx_ref[...], x_ref[...])
    xa_ref[...] = x
    sc = _region_select(tile, scl_ref, scc_ref)
    sh = _region_select(tile, shl_ref, shc_ref)
    o_ref[...] = (_rms(x) * gain_ref[...] * (1.0 + sc) + sh).astype(o_ref.dtype)


def _modnorm(x, ctx, gain, mod_lat, mod_ctx, sc_i, sh_i):
    tile = T_CTX
    n_lat = T_LAT // tile
    row = pl.BlockSpec((None, tile, D_MODEL), lambda b, i: (b, i, 0))
    return pl.pallas_call(
        functools.partial(_modnorm_kernel, tile=tile),
        grid=(BATCH, T_ALL // tile),
        in_specs=[pl.BlockSpec((None, tile, D_MODEL), lambda b, i: (b, jnp.minimum(i, n_lat - 1), 0)),
                  pl.BlockSpec((None, tile, D_MODEL), lambda b, i: (b, jnp.maximum(i - n_lat, 0), 0)),
                  pl.BlockSpec((1, D_MODEL), lambda b, i: (0, 0)),
                  _lat_spec(sc_i), _ctx_spec(sc_i), _lat_spec(sh_i), _ctx_spec(sh_i)],
        out_specs=[row, row],
        out_shape=[jax.ShapeDtypeStruct((BATCH, T_ALL, D_MODEL), F32),
                   jax.ShapeDtypeStruct((BATCH, T_ALL, D_MODEL), BF16)],
        compiler_params=_cparams(2),
        name="modnorm",
    )(x, ctx, gain, mod_lat, mod_ctx, mod_lat, mod_ctx)


def _rotary(z, cos, s1, s2):
    return z * cos + pltpu.roll(z, 96, 1) * s1 + pltpu.roll(z, 32, 1) * s2


def _ret_kernel(lg_ref, q_ref, g_ref, k_ref, v_ref, cos_ref, s1_ref, s2_ref, gain_ref,
                o_ref, acc_ref, qs_ref, ks_ref):
    c = RET_CHUNK
    nh = RET_HEADS_PER_STEP
    scale = HEAD_DIM ** -0.5
    dist = (lax.broadcasted_iota(jnp.int32, (c, c), 0)
            - lax.broadcasted_iota(jnp.int32, (c, c), 1)).astype(F32)
    tcol = lax.broadcasted_iota(jnp.int32, (c, 1), 0).astype(F32)

    def head_consts(hp):
        h = pl.program_id(1) * nh + hp
        la_f, la_b = lg_ref[0, h], lg_ref[1, h]
        dmat = (jnp.where(dist >= 0.0, jnp.exp(la_f * jnp.maximum(dist, 0.0)), 0.0)
                + jnp.where(dist <= 0.0, jnp.exp(la_b * jnp.maximum(-dist, 0.0)), 0.0))
        fwd_dec = (jnp.exp(la_f * (tcol + 1.0)), jnp.exp(la_f * (c - 1.0 - tcol)),
                   jnp.exp(jnp.full((1, LANE), la_f * c, F32)))
        bwd_dec = (jnp.exp(la_b * (c - tcol)), jnp.exp(la_b * tcol),
                   jnp.exp(jnp.full((1, LANE), la_b * c, F32)))
        return dmat, fwd_dec, bwd_dec

    consts = [head_consts(hp) for hp in range(nh)]

    def fwd(i, sts, base, rot):
        start = pl.multiple_of(base + i * c, c)
        rows = pl.ds(start, c)
        if rot:
            cos, s1, s2 = cos_ref[rows, :], s1_ref[rows, :], s2_ref[rows, :]
        out = []
        for hp in range(nh):
            hl = pl.ds(hp * LANE, LANE)
            dmat, (qdec, kdec, cdec), _ = consts[hp]
            q = q_ref[rows, hl].astype(F32) * scale
            k = k_ref[rows, hl].astype(F32)
            v = v_ref[rows, hl]
            if rot:
                q = _rotary(q, cos, s1, s2)
                k = _rotary(k, cos, s1, s2)
            qs_ref[rows, hl] = q
            ks_ref[rows, hl] = k
            att = _dot_nt(q.astype(BF16), k.astype(BF16)) * dmat
            st = sts[hp]
            acc_ref[rows, hl] = (_dot(att.astype(BF16), v)
                                 + _dot_nt((q * qdec).astype(BF16), st.astype(BF16)))
            out.append(cdec * st + _dot_tn(v, (k * kdec).astype(BF16)))
        return tuple(out)

    def bwd(i, sts, base, n):
        start = pl.multiple_of(base + (n - 1 - i) * c, c)
        rows = pl.ds(start, c)
        out = []
        for hp in range(nh):
            hl = pl.ds(hp * LANE, LANE)
            _, _, (qdec, kdec, cdec) = consts[hp]
            q = qs_ref[rows, hl]
            k = ks_ref[rows, hl]
            v = v_ref[rows, hl]
            st = sts[hp]
            o = acc_ref[rows, hl] + _dot_nt((q * qdec).astype(BF16), st.astype(BF16))
            gate = g_ref[rows, hl].astype(F32)
            o_ref[rows, hl] = (_rms(o) * gain_ref[:, hl] * _silu(gate)).astype(o_ref.dtype)
            out.append(cdec * st + _dot_tn(v, (k * kdec).astype(BF16)))
        return tuple(out)

    zero = tuple(jnp.zeros((HEAD_DIM, HEAD_DIM), F32) for _ in range(nh))
    n_ctx, n_lat = T_CTX // c, T_LAT // c
    sts = lax.fori_loop(0, n_ctx, functools.partial(fwd, base=T_LAT, rot=False), zero)
    lax.fori_loop(0, n_lat, functools.partial(fwd, base=0, rot=True), sts, unroll=RET_UNROLL)
    sts = lax.fori_loop(0, n_ctx, functools.partial(bwd, base=T_LAT, n=n_ctx), zero)
    lax.fori_loop(0, n_lat, functools.partial(bwd, base=0, n=n_lat), sts, unroll=RET_UNROLL)


def _retention(p, log_g, cos, s1, s2, ret_norm):
    nh = RET_HEADS_PER_STEP
    hw = nh * LANE
    col = lambda off: pl.BlockSpec((None, T_ALL, hw),
                                   lambda b, h, lg: (b, 0, off * LANE // hw + h))
    tab = pl.BlockSpec((T_LAT, LANE), lambda b, h, lg: (0, 0))
    return pl.pallas_call(
        _ret_kernel,
        grid_spec=pltpu.PrefetchScalarGridSpec(
            num_scalar_prefetch=1,
            grid=(BATCH, RET_HEADS // nh),
            in_specs=[col(0), col(8), col(32), col(40), tab, tab, tab,
                      pl.BlockSpec((1, hw), lambda b, h, lg: (0, h))],
            out_specs=pl.BlockSpec((None, T_ALL, hw), lambda b, h, lg: (b, 0, h)),
            scratch_shapes=[pltpu.VMEM((T_ALL, hw), F32), pltpu.VMEM((T_ALL, hw), F32),
                            pltpu.VMEM((T_ALL, hw), F32)]),
        out_shape=jax.ShapeDtypeStruct((BATCH, T_ALL, RET_DIM), BF16),
        compiler_params=_cparams(2),
        name="retention",
    )(log_g, p, p, p, p, cos, s1, s2, ret_norm)


def _gmlp_kernel(u_ref, v_ref, gain_ref, ws_ref, bst_ref, o_ref):
    u = jax.nn.gelu(u_ref[...].astype(F32))
    v = _rms(jax.nn.gelu(v_ref[...].astype(F32))) * gain_ref[...]
    vb = v.astype(BF16)
    for n in range(ROW_TILE // GMLP_CHUNK):
        r0 = n * GMLP_CHUNK
        for g in range(GMLP_GROUPS):
            c0 = g * LANE
            mixed = (_dot(ws_ref[g], vb[r0:r0 + GMLP_CHUNK, c0:c0 + LANE])
                     + bst_ref[:, g:g + 1])
            o_ref[r0:r0 + GMLP_CHUNK, c0:c0 + LANE] = (
                u[r0:r0 + GMLP_CHUNK, c0:c0 + LANE] * mixed).astype(o_ref.dtype)


def _gmlp(p, gmlp_norm, ws_bf16, bs_t):
    return pl.pallas_call(
        _gmlp_kernel,
        grid=(BATCH, T_ALL // ROW_TILE),
        in_specs=[pl.BlockSpec((None, ROW_TILE, GMLP_DIM), lambda b, i: (b, i, 2)),
                  pl.BlockSpec((None, ROW_TILE, GMLP_DIM), lambda b, i: (b, i, 3)),
                  pl.BlockSpec((1, GMLP_DIM), lambda b, i: (0, 0)),
                  pl.BlockSpec((GMLP_GROUPS, GMLP_CHUNK, GMLP_CHUNK), lambda b, i: (0, 0, 0)),
                  pl.BlockSpec((GMLP_CHUNK, GMLP_GROUPS), lambda b, i: (0, 0))],
        out_specs=pl.BlockSpec((None, ROW_TILE, GMLP_DIM), lambda b, i: (b, i, 0)),
        out_shape=jax.ShapeDtypeStruct((BATCH, T_ALL, GMLP_DIM), BF16),
        compiler_params=_cparams(2),
        name="gmlp",
    )(p, p, gmlp_norm, ws_bf16, bs_t)


def _na_kernel(q_ref, k_ref, v_ref, bias_ref, o_ref):
    i = pl.program_id(2)
    w0 = jnp.clip(i * NA_QROWS - WIN_ROWS // 2, 0, GRID_ROWS - NA_KROWS)
    start = pl.multiple_of(w0 * GRID_W, GRID_W)
    scale = HEAD_DIM ** -0.5
    for hp in range(NA_HEADS_PER_STEP):
        hl = pl.ds(hp * LANE, LANE)
        q = q_ref[:, hl]
        s_win = _dot_nt(q, k_ref[pl.ds(start, NA_KT), hl]) * scale + bias_ref[hp]
        s_ctx = _dot_nt(q, k_ref[pl.ds(T_LAT, T_CTX), hl]) * scale
        m = jnp.maximum(jnp.max(s_win, axis=-1, keepdims=True),
                        jnp.max(s_ctx, axis=-1, keepdims=True))
        p_win = jnp.exp(s_win - m)
        p_ctx = jnp.exp(s_ctx - m)
        denom = jnp.sum(p_win, axis=-1, keepdims=True) + jnp.sum(p_ctx, axis=-1, keepdims=True)
        o = (_dot(p_win.astype(BF16), v_ref[pl.ds(start, NA_KT), hl])
             + _dot(p_ctx.astype(BF16), v_ref[pl.ds(T_LAT, T_CTX), hl]))
        o_ref[:, hl] = (o / denom).astype(o_ref.dtype)


def _na_geometry():
    n_blk = GRID_ROWS // NA_QROWS
    dr00 = []
    row_ok = np.zeros((3, NA_QROWS, NA_KROWS), bool)
    for kind, i_rep in enumerate((0, 1, n_blk - 1)):
        r0 = i_rep * NA_QROWS
        w0 = int(np.clip(r0 - WIN_ROWS // 2, 0, GRID_ROWS - NA_KROWS))
        r = r0 + np.arange(NA_QROWS)[:, None]
        kr = w0 + np.arange(NA_KROWS)[None, :]
        rstart = np.clip(r - WIN_ROWS // 2, 0, GRID_ROWS - WIN_ROWS)
        row_ok[kind] = (kr >= rstart) & (kr < rstart + WIN_ROWS)
        dr00.append(w0 - r0 + WIN_ROWS - 1)
    c = np.arange(GRID_W)[:, None]
    kc = np.arange(GRID_W)[None, :]
    cstart = np.clip(c - WIN_COLS // 2, 0, GRID_W - WIN_COLS)
    col_ok = (kc >= cstart) & (kc < cstart + WIN_COLS)
    valid = row_ok[:, :, None, :, None] & col_ok[None, None, :, None, :]
    return dr00, valid.reshape(3, NA_QT, NA_KT)


def _na_bias_table(rpb):
    n_dr, n_dc = 2 * WIN_ROWS - 1, 2 * WIN_COLS - 1
    w = jnp.concatenate([rpb[..., WIN_COLS - 1:], jnp.zeros((NA_HEADS, n_dr, LANE - n_dc), F32),
                         rpb[..., :WIN_COLS - 1]], axis=-1)
    toep = jnp.tile(w, (1, 1, GRID_W))[..., :GRID_W * (LANE - 1)]
    toep = toep.reshape(NA_HEADS, n_dr, GRID_W, LANE - 1)[..., :GRID_W]
    dr00, valid = _na_geometry()
    front = max(0, NA_QROWS - 1 - min(dr00))
    back = max(0, max(dr00) + NA_KROWS - n_dr)
    toep = jnp.pad(toep.transpose(0, 2, 1, 3), ((0, 0), (0, 0), (front, back), (0, 0)))
    toep = toep.astype(BF16).reshape(NA_HEADS, GRID_W, -1)
    kinds = []
    for d0 in dr00:
        starts = [(d0 - rl + front) * GRID_W for rl in range(NA_QROWS)]
        kinds.append(jnp.stack([toep[:, :, s:s + NA_KT] for s in starts], axis=1))
    tab = jnp.stack(kinds, axis=0).reshape(3, NA_HEADS, NA_QT, NA_KT)
    return jnp.where(jnp.asarray(valid)[:, None], tab, jnp.asarray(NEG_BIG, BF16))


def _na(p, bias):
    n_blk = T_LAT // NA_QT
    nh = NA_HEADS_PER_STEP
    hw = nh * LANE
    k0, v0 = 20 * LANE // hw, 28 * LANE // hw
    kind = lambda i: jnp.where(i == 0, 0, jnp.where(i == n_blk - 1, 2, 1))
    return pl.pallas_call(
        _na_kernel,
        grid=(BATCH, NA_HEADS // nh, n_blk),
        in_specs=[pl.BlockSpec((None, NA_QT, hw), lambda b, h, i: (b, i, h)),
                  pl.BlockSpec((None, T_ALL, hw), lambda b, h, i: (b, 0, k0 + h)),
                  pl.BlockSpec((None, T_ALL, hw), lambda b, h, i: (b, 0, v0 + h)),
                  pl.BlockSpec((None, nh, NA_QT, NA_KT), lambda b, h, i: (kind(i), h, 0, 0))],
        out_specs=pl.BlockSpec((None, NA_QT, hw), lambda b, h, i: (b, i, h)),
        out_shape=jax.ShapeDtypeStruct((BATCH, T_LAT, NA_DIM), BF16),
        compiler_params=_cparams(3),
        name="nbr_attention",
    )(p, p, p, bias)


def _gla_kernel(q_ref, k_ref, v_ref, g_ref, lr_ref, wcat_ref, bup_ref, gain_ref,
                o_ref, acc_ref):
    r, c = GLA_SUPER, GLA_SUB
    n_sub = r // c
    dv2 = 2 * GLA_DV
    qscale = GLA_DK ** -0.5
    row = lax.broadcasted_iota(jnp.int32, (r, r), 0)
    col = lax.broadcasted_iota(jnp.int32, (r, r), 1)
    sub_shift = c.bit_length() - 1
    same = (row >> sub_shift) == (col >> sub_shift)
    blk_ones = jnp.where(same, 1.0, 0.0).astype(BF16)
    second_head_lane = lax.broadcasted_iota(jnp.int32, (1, LANE), 1) >= GLA_DK
    st_mask = ((lax.broadcasted_iota(jnp.int32, (dv2, LANE), 0) >= GLA_DV)
               == (lax.broadcasted_iota(jnp.int32, (dv2, LANE), 1) >= GLA_DK))

    for d in range(2):
        tri = same & ((row >= col) if d == 0 else (col >= row))
        tri_b = jnp.where(tri, 1.0, 0.0).astype(BF16)

        def pair_step(hp, start, lr, st, d=d, tri=tri, tri_b=tri_b):
            kl = pl.ds(hp * LANE, LANE)
            vl = pl.ds(hp * dv2, dv2)
            zz = _dot(lr, wcat_ref[d, :, pl.ds(hp * 2 * LANE, 2 * LANE)])
            z = zz[:, :LANE] + zz[:, LANE:] + bup_ref[d, :, kl]
            la = (jnp.minimum(z, 0.0) - jnp.log(1.0 + jnp.exp(-jnp.abs(z)))) * (1.0 / GLA_TAU)
            hl = jnp.concatenate(_split2(la), axis=1)
            bb = _dot(tri_b, hl)
            b = bb[:, :LANE] + bb[:, LANE:]
            bb = _dot(blk_ones, hl)
            bt = bb[:, :LANE] + bb[:, LANE:]
            q = q_ref[pl.ds(start, r), kl].astype(F32) * qscale
            k = k_ref[pl.ds(start, r), kl].astype(F32)
            v = v_ref[pl.ds(start, r), vl]
            qi = q * jnp.exp(b)
            qib = qi.astype(BF16)
            ki = (k * jnp.exp(-b)).astype(BF16)
            ko = (k * jnp.exp(bt - b)).astype(BF16)
            dec = jnp.exp(bt)
            outs = []
            for j in range(2):
                qm = jnp.where(second_head_lane == (j == 1), qi, 0.0).astype(BF16)
                att = jnp.where(tri, _dot_nt(qm, ki), 0.0)
                outs.append(_dot(att.astype(BF16), v[:, j * GLA_DV:(j + 1) * GLA_DV]))
            o_intra = jnp.concatenate(outs, axis=1)
            parts = [None] * n_sub
            for s in (range(n_sub) if d == 0 else reversed(range(n_sub))):
                lo_r, hi_r = s * c, (s + 1) * c
                parts[s] = o_intra[lo_r:hi_r] + _dot_nt(qib[lo_r:hi_r], st.astype(BF16))
                st = (st * dec[lo_r:lo_r + 1, :]
                      + jnp.where(st_mask, _dot_tn(v[lo_r:hi_r], ko[lo_r:hi_r]), 0.0))
            o = jnp.concatenate(parts, axis=0)
            if d == 0:
                acc_ref[pl.ds(start, r), vl] = o
            else:
                o = acc_ref[pl.ds(start, r), vl] + o
                gate = g_ref[pl.ds(start, r), vl].astype(F32)
                on = (jnp.concatenate([_rms(o[:, :GLA_DV]), _rms(o[:, GLA_DV:])], axis=1)
                      * gain_ref[:, vl])
                o_ref[pl.ds(start, r), vl] = (on * _silu(gate)).astype(o_ref.dtype)
            return st

        def step(i, sts, base, n, d=d):
            idx = i if d == 0 else n - 1 - i
            start = pl.multiple_of(base + idx * r, r)
            lr = lr_ref[pl.ds(start, r), :]
            return tuple(pair_step(hp, start, lr, sts[hp]) for hp in range(GLA_PAIRS))

        sts = tuple(jnp.zeros((dv2, LANE), F32) for _ in range(GLA_PAIRS))
        sts = lax.fori_loop(0, T_CTX // r, functools.partial(step, base=T_LAT, n=T_CTX // r), sts)
        lax.fori_loop(0, T_LAT // r, functools.partial(step, base=0, n=T_LAT // r), sts,
                      unroll=GLA_UNROLL)


def _gla(p, lr, w_cat, b_up, gla_norm):
    kw = GLA_PAIRS * LANE
    vw = GLA_PAIRS * 2 * GLA_DV
    q0, k0 = 8 * LANE // kw, 36 * LANE // kw
    v0, g0 = 40 * LANE // vw, 12 * LANE // vw
    return pl.pallas_call(
        _gla_kernel,
        grid=(BATCH, GLA_HEADS // (2 * GLA_PAIRS)),
        in_specs=[pl.BlockSpec((None, T_ALL, kw), lambda b, j: (b, 0, q0 + j)),
                  pl.BlockSpec((None, T_ALL, kw), lambda b, j: (b, 0, k0 + j)),
                  pl.BlockSpec((None, T_ALL, vw), lambda b, j: (b, 0, v0 + j)),
                  pl.BlockSpec((None, T_ALL, vw), lambda b, j: (b, 0, g0 + j)),
                  pl.BlockSpec((None, T_ALL, LANE), lambda b, j: (b, 0, 0)),
                  pl.BlockSpec((2, LANE, 2 * kw), lambda b, j: (0, 0, j)),
                  pl.BlockSpec((2, 1, kw), lambda b, j: (0, 0, j)),
                  pl.BlockSpec((1, vw), lambda b, j: (0, j))],
        out_specs=pl.BlockSpec((None, T_ALL, vw), lambda b, j: (b, 0, j)),
        out_shape=jax.ShapeDtypeStruct((BATCH, T_ALL, GLA_V_DIM), BF16),
        scratch_shapes=[pltpu.VMEM((T_ALL, vw), F32)],
        compiler_params=_cparams(2),
        name="gla",
    )(p, p, p, p, lr, w_cat, b_up, gla_norm)


def _out_kernel(a1_ref, a2_ref, w1_ref, w2_ref, x_ref, gl_ref, gc_ref, gain_ref,
                scl_ref, scc_ref, shl_ref, shc_ref, rw_ref, xo_ref, h_ref, lg_ref, *, tile,
                has_ctx):
    half = tile // 2
    for r0 in (0, half):
        rows = pl.ds(r0, half)
        sel = functools.partial(_region_select, tile, row0=r0, n_rows=half, has_ctx=has_ctx)
        y = _dot(a1_ref[rows, :], w1_ref[...]) + _dot(a2_ref[rows, :], w2_ref[...])
        xn = x_ref[rows, :] + sel(gl_ref, gc_ref) * y
        xo_ref[rows, :] = xn
        h = _rms(xn) * gain_ref[...] * (1.0 + sel(scl_ref, scc_ref)) + sel(shl_ref, shc_ref)
        h_hi = h.astype(BF16)
        h_ref[rows, :] = h_hi
        h_lo = (h - h_hi.astype(F32)).astype(BF16)
        lg_ref[rows, :] = _dot(h_hi, rw_ref[...]) + _dot(h_lo, rw_ref[...])


def _out_proj(a1, a2, w_out_bf16, x, mod_lat, mod_ctx, norm_gain, rw_split, tile, n_tiles):
    half = w_out_bf16.shape[0] // 2
    t_out = n_tiles * tile
    row = lambda width: pl.BlockSpec((None, tile, width), lambda b, i: (b, i, 0))
    return pl.pallas_call(
        functools.partial(_out_kernel, tile=tile, has_ctx=t_out > T_LAT),
        grid=(BATCH, n_tiles),
        in_specs=[row(half), row(half),
                  pl.BlockSpec((half, D_MODEL), lambda b, i: (0, 0)),
                  pl.BlockSpec((half, D_MODEL), lambda b, i: (1, 0)),
                  row(D_MODEL), _lat_spec(2), _ctx_spec(2),
                  pl.BlockSpec((1, D_MODEL), lambda b, i: (0, 0)),
                  _lat_spec(4), _ctx_spec(4), _lat_spec(3), _ctx_spec(3),
                  pl.BlockSpec((D_MODEL, 2 * N_EXPERTS), lambda b, i: (0, 0))],
        out_specs=[row(D_MODEL), row(D_MODEL), row(2 * N_EXPERTS)],
        out_shape=[jax.ShapeDtypeStruct((BATCH, t_out, D_MODEL), F32),
                   jax.ShapeDtypeStruct((BATCH, t_out, D_MODEL), BF16),
                   jax.ShapeDtypeStruct((BATCH, t_out, 2 * N_EXPERTS), F32)],
        compiler_params=_cparams(2),
        name="out_proj",
    )(a1, a2, w_out_bf16, w_out_bf16, x, mod_lat, mod_ctx, norm_gain,
      mod_lat, mod_ctx, mod_lat, mod_ctx, rw_split)


def _moe_up_kernel(x_ref, wg_ref, wu_ref, o_ref):
    x = x_ref[...]
    a = _dot(x, wg_ref[...].astype(BF16))
    u = _dot(x, wu_ref[...].astype(BF16))
    o_ref[...] = (_silu(a) * u).astype(o_ref.dtype)


def _moe_down_kernel(h_ref, wd_ref, g_ref, o_ref):
    gcol = _dot_tn(g_ref[...], jnp.ones((GATE_ROWS, LANE), BF16))
    y = _dot(h_ref[...], wd_ref[...].astype(BF16))
    for c in range(o_ref.shape[1] // LANE):
        cols = pl.ds(c * LANE, LANE)
        o_ref[:, cols] = (y[:, c * LANE:(c + 1) * LANE] * gcol).astype(o_ref.dtype)


def _moe(xs, gates, w_gate, w_up, w_down, layer):
    e, m, _ = xs.shape
    tn = MOE_TN
    wspec = pl.BlockSpec((None, None, D_MODEL, tn), lambda e, f: (layer, e, 0, f))
    hmid = pl.pallas_call(
        _moe_up_kernel,
        grid=(e, D_EXPERT // tn),
        in_specs=[pl.BlockSpec((None, m, D_MODEL), lambda e, f: (e, 0, 0)), wspec, wspec],
        out_specs=pl.BlockSpec((None, m, tn), lambda e, f: (e, 0, f)),
        out_shape=jax.ShapeDtypeStruct((e, m, D_EXPERT), BF16),
        compiler_params=_cparams(2),
        name="moe_up",
    )(xs, w_gate, w_up)
    tn = MOE_DOWN_TN
    return pl.pallas_call(
        _moe_down_kernel,
        grid=(e, D_MODEL // tn),
        in_specs=[pl.BlockSpec((None, m, D_EXPERT), lambda e, f: (e, 0, 0)),
                  pl.BlockSpec((None, None, D_EXPERT, tn), lambda e, f: (layer, e, 0, f)),
                  pl.BlockSpec((None, GATE_ROWS, m), lambda e, f: (e, 0, 0))],
        out_specs=pl.BlockSpec((None, m, tn), lambda e, f: (e, 0, f)),
        out_shape=jax.ShapeDtypeStruct((e, m, D_MODEL), BF16),
        compiler_params=_cparams(2),
        name="moe_down",
    )(hmid, w_down, gates)


def _combine_moe(span_ref, pair_tok_ref, ys_hbm, acc_ref, buf_ref, sem, *, tile, n_pairs):
    b, i = pl.program_id(0), pl.program_id(1)
    n_tiles = pl.num_programs(1)
    step = b * n_tiles + i
    n_steps = pl.num_programs(0) * n_tiles
    w = PAIR_WIN
    shift = w.bit_length() - 1
    last_start = n_pairs // w - PAIR_CHUNK

    def first_window(s):
        return jnp.minimum(span_ref[2 * s] >> shift, last_start)

    def chunk_copy(sample, win, slot):
        return pltpu.make_async_copy(
            ys_hbm.at[pl.ds(sample * n_pairs + win * w, PAIR_CHUNK * w), :],
            buf_ref.at[slot], sem.at[slot])

    def window_copy(win, slot):
        return pltpu.make_async_copy(ys_hbm.at[pl.ds(b * n_pairs + win * w, w), :],
                                     buf_ref.at[slot, pl.ds(0, w), :], sem.at[slot])

    slot = step & 1
    w0 = first_window(step)

    @pl.when(step == 0)
    def _():
        chunk_copy(b, w0, slot).start()

    @pl.when(step + 1 < n_steps)
    def _():
        next_sample = jnp.where(i + 1 < n_tiles, b, b + 1)
        chunk_copy(next_sample, first_window(step + 1), 1 - slot).start()

    chunk_copy(b, w0, slot).wait()
    tok = i * tile + lax.broadcasted_iota(jnp.int32, (tile, 1), 0)

    def select(win):
        return jnp.where(tok == pair_tok_ref[pl.ds(win, 1), :], 1.0, 0.0).astype(BF16)

    sel = jnp.concatenate([select(w0 + c) for c in range(PAIR_CHUNK)], axis=1)
    acc_ref[...] = _dot(sel, buf_ref[slot])

    hi = span_ref[2 * step + 1]
    end_win = jnp.where(hi > 0, ((hi - 1) >> shift) + 1, 0)

    def body(win, carry):
        cp = window_copy(win, slot)
        cp.start()
        cp.wait()
        acc_ref[...] += _dot(select(win), buf_ref[slot, pl.ds(0, w), :])
        return carry

    lax.fori_loop(w0 + PAIR_CHUNK, end_win, body, 0)
    return acc_ref[...]


def _res_mid_kernel(span_ref, x_ref, pair_tok_ref, ys_hbm, gl_ref, gc_ref, gain_ref,
                    scl_ref, scc_ref, shl_ref, shc_ref, xo_ref, h_ref, acc_ref, buf_ref, sem,
                    *, tile, n_pairs):
    moe = _combine_moe(span_ref, pair_tok_ref, ys_hbm, acc_ref, buf_ref, sem,
                       tile=tile, n_pairs=n_pairs)
    xn = x_ref[...] + _region_select(tile, gl_ref, gc_ref) * moe
    xo_ref[...] = xn
    h_ref[...] = (_rms(xn) * gain_ref[...] * (1.0 + _region_select(tile, scl_ref, scc_ref))
                  + _region_select(tile, shl_ref, shc_ref)).astype(h_ref.dtype)


def _res_final_kernel(span_ref, x_ref, pair_tok_ref, ys_hbm, g_ref, gain_ref, o_ref,
                      acc_ref, buf_ref, sem, *, tile, n_pairs):
    moe = _combine_moe(span_ref, pair_tok_ref, ys_hbm, acc_ref, buf_ref, sem,
                       tile=tile, n_pairs=n_pairs)
    o_ref[...] = _rms(x_ref[...] + g_ref[...] * moe) * gain_ref[...]


def _combine_call(kernel_fn, name, tile, t_rows, spans, x, pair_tok, ys, params, param_specs,
                  out_dtypes):
    n_pairs = pair_tok.shape[1] * PAIR_WIN
    row = pl.BlockSpec((None, tile, D_MODEL), lambda b, i, *_: (b, i, 0))
    return pl.pallas_call(
        functools.partial(kernel_fn, tile=tile, n_pairs=n_pairs),
        grid_spec=pltpu.PrefetchScalarGridSpec(
            num_scalar_prefetch=1,
            grid=(BATCH, t_rows // tile),
            in_specs=[row,
                      pl.BlockSpec((None,) + pair_tok.shape[1:], lambda b, i, *_: (b, 0, 0)),
                      pl.BlockSpec(memory_space=pl.ANY)] + param_specs,
            out_specs=[row] * len(out_dtypes),
            scratch_shapes=[pltpu.VMEM((tile, D_MODEL), F32),
                            pltpu.VMEM((2, PAIR_CHUNK * PAIR_WIN, D_MODEL), BF16),
                            pltpu.SemaphoreType.DMA((2,))]),
        out_shape=[jax.ShapeDtypeStruct((BATCH, t_rows, D_MODEL), dt) for dt in out_dtypes],
        compiler_params=_cparams(2),
        name=name,
    )(spans, x, pair_tok, ys, *params)


def _res_mid(x, spans, pair_tok, ys, mod_lat, mod_ctx, next_lat, next_ctx, norm_gain_next):
    specs = [_lat_spec(5), _ctx_spec(5), pl.BlockSpec((1, D_MODEL), lambda b, i, *_: (0, 0)),
             _lat_spec(1), _ctx_spec(1), _lat_spec(0), _ctx_spec(0)]
    params = (mod_lat, mod_ctx, norm_gain_next, next_lat, next_ctx, next_lat, next_ctx)
    return _combine_call(_res_mid_kernel, "ffn_combine_residual_prenorm", COMB_TILE, T_ALL, spans,
                         x, pair_tok, ys, params, specs, (F32, BF16))


def _res_final(x, spans, pair_tok, ys, mod_lat, norm_final):
    specs = [_lat_spec(5), pl.BlockSpec((1, D_MODEL), lambda b, i, *_: (0, 0))]
    return _combine_call(_res_final_kernel, "ffn_combine_residual_final_norm", COMB_TILE, T_LAT,
                         spans, x, pair_tok, ys, (mod_lat, norm_final), specs, (F32,))[0]


def _rope_tables():
    half = HEAD_DIM // 2
    nf = half // 2
    t = np.arange(T_LAT)
    rows, cols = t // GRID_W, t % GRID_W
    freq = (np.float32(ROPE_BASE) ** (-np.arange(nf, dtype=np.float32) / np.float32(nf))).astype(np.float32)
    ang_r = rows.astype(np.float32)[:, None] * freq
    ang_c = cols.astype(np.float32)[:, None] * freq
    zeros = np.zeros_like(ang_r)
    cos = np.concatenate([np.cos(ang_r), np.cos(ang_r), np.cos(ang_c), np.cos(ang_c)], axis=1)
    s1 = np.concatenate([-np.sin(ang_r), zeros, -np.sin(ang_c), zeros], axis=1)
    s2 = np.concatenate([zeros, np.sin(ang_r), zeros, np.sin(ang_c)], axis=1)
    return tuple(jnp.asarray(a, F32) for a in (cos, s1, s2))


def _router_split(rw):
    hi = rw.astype(BF16)
    lo = (rw - hi.astype(F32)).astype(BF16)
    return jnp.concatenate([hi, lo], axis=1)


def _route(logits, t0, t1, cap):
    aff = jax.nn.softmax(logits[:, t0:t1], axis=-1)
    g, idx = lax.top_k(aff.transpose(0, 2, 1), cap)
    return g, idx + t0


def _expert_ffn(h, logits2, w_gate, w_up, w_down, layer, with_ctx, tile):
    logits = logits2[..., :N_EXPERTS] + logits2[..., N_EXPERTS:]
    g, idx = _route(logits, 0, T_LAT, EC_CAPACITY * T_LAT // N_EXPERTS)
    if with_ctx:
        gc, idxc = _route(logits, T_LAT, T_ALL, EC_CAPACITY * T_CTX // N_EXPERTS)
        g = jnp.concatenate([g, gc], axis=-1)
        idx = jnp.concatenate([idx, idxc], axis=-1)
    cap = idx.shape[-1]
    t_rows = h.shape[1]
    n_pairs = N_EXPERTS * cap
    assert n_pairs % PAIR_WIN == 0
    flat = (idx + jnp.arange(BATCH)[:, None, None] * t_rows).transpose(1, 0, 2).reshape(N_EXPERTS, -1)
    g_rows = g.transpose(1, 0, 2).reshape(N_EXPERTS, 1, BATCH * cap)
    g_hi = g_rows.astype(BF16)
    g_mid = (g_rows - g_hi.astype(F32)).astype(BF16)
    g_lo = (g_rows - g_hi.astype(F32) - g_mid.astype(F32)).astype(BF16)
    gates = jnp.concatenate([g_hi, g_mid, g_lo,
                             jnp.zeros((N_EXPERTS, GATE_ROWS - 3, BATCH * cap), BF16)], axis=1)
    xs = h.reshape(BATCH * t_rows, D_MODEL)[flat]
    y = _moe(xs, gates, w_gate, w_up, w_down, layer)
    src = (jnp.arange(N_EXPERTS)[None, :, None] * (BATCH * cap)
           + jnp.arange(BATCH)[:, None, None] * cap + jnp.arange(cap)[None, None, :])
    cap_lat = EC_CAPACITY * T_LAT // N_EXPERTS
    parts = []
    for sl in ((slice(0, cap_lat), slice(cap_lat, cap)) if with_ctx else (slice(0, cap),)):
        parts.append(lax.sort((idx[..., sl].reshape(BATCH, -1), src[..., sl].reshape(BATCH, -1)),
                              dimension=1, num_keys=1))
    pair_tok = jnp.concatenate([p[0] for p in parts], axis=1)
    src = jnp.concatenate([p[1] for p in parts], axis=1)
    ys = y.reshape(-1, D_MODEL)[src.reshape(-1)]
    bounds = jnp.arange(0, t_rows + tile, tile)
    below = jnp.sum(pair_tok[:, :, None] < bounds[None, None, :], axis=1)
    spans = jnp.stack([below[:, :-1], below[:, 1:]], axis=-1).reshape(-1).astype(jnp.int32)
    return spans, pair_tok.reshape(BATCH, n_pairs // PAIR_WIN, PAIR_WIN), ys


def kernel(x, c, ctx, c_ctx, ada_w, ada_b, norm_mix, norm_ffn, norm_final, ev_w_in, ev_w_out,
           ret_gamma_logit, ret_norm, gmlp_norm, gmlp_ws, gmlp_bs, od_w_in, od_w_out, na_rpb,
           gla_w_up, gla_b_up, gla_norm, router_w, moe_w_gate, moe_w_up, moe_w_down):
    assert x.shape == (BATCH, T_LAT, D_MODEL) and ctx.shape == (BATCH, T_CTX, D_MODEL)
    assert ada_w.shape[0] == 2 and ev_w_in.shape[0] == 1 and od_w_in.shape[0] == 1

    cond = jnp.concatenate([jax.nn.silu(c), jax.nn.silu(c_ctx)[None],
                            jnp.zeros((11, D_MODEL), F32)], axis=0).astype(BF16)
    mod_lat, mod_ctx = [], []
    for l in range(2):
        m = _mod_matmul(cond, ada_w, l) + ada_b[l]
        m = m[:BATCH + 1].reshape(BATCH + 1, 6, 1, D_MODEL)
        mod_lat.append(m[:BATCH])
        mod_ctx.append(m[BATCH])

    cos, s1, s2 = _rope_tables()

    xa, h = _modnorm(x, ctx, norm_mix[0:1], mod_lat[0], mod_ctx[0], 1, 0)
    p = _proj_matmul(h.reshape(BATCH * T_ALL, D_MODEL), ev_w_in, 0, EVEN_COLS, 2176, 1024)
    p = p.reshape(BATCH, T_ALL, EVEN_COLS)
    log_g = jax.nn.log_sigmoid(ret_gamma_logit[0].astype(F32))
    ret = _retention(p, log_g, cos, s1, s2, ret_norm[0:1])
    gm = _gmlp(p, gmlp_norm[0:1], gmlp_ws[0].astype(BF16), gmlp_bs[0].T)
    xa, h2, logits = _out_proj(ret, gm, ev_w_out[0].astype(BF16), xa, mod_lat[0], mod_ctx[0],
                               norm_ffn[0:1], _router_split(router_w[0]), MIX_TILE,
                               T_ALL // MIX_TILE)
    spans, pair_tok, ys = _expert_ffn(h2, logits, moe_w_gate, moe_w_up, moe_w_down, 0, True,
                                      COMB_TILE)
    xa, h = _res_mid(xa, spans, pair_tok, ys, mod_lat[0], mod_ctx[0], mod_lat[1], mod_ctx[1],
                     norm_mix[1:2])

    h2d = h.reshape(BATCH * T_ALL, D_MODEL)
    w_in_t = od_w_in[0].T
    p = _proj_matmul_t(h2d, w_in_t, ODD_MAIN_COLS, 2176, 1024)
    p = p.reshape(BATCH, T_ALL, ODD_MAIN_COLS)
    w_lr_t = jnp.pad(w_in_t[ODD_MAIN_COLS:], ((0, LANE - 2 * GLA_GATE_RANK), (0, 0))).astype(BF16)
    lr = _proj_matmul_t(h2d, w_lr_t, LANE, 2176, LANE).reshape(BATCH, T_ALL, LANE)
    o_na = _na(p, _na_bias_table(na_rpb[0]))
    w_up_full = jnp.zeros((2, LANE, GLA_QK_DIM), F32)
    for d in range(2):
        w_up_full = w_up_full.at[d, d * GLA_GATE_RANK:(d + 1) * GLA_GATE_RANK].set(gla_w_up[0, d])
    w_hi = w_up_full.astype(BF16)
    w_lo = (w_up_full - w_hi.astype(F32)).astype(BF16)
    n_pair = GLA_QK_DIM // LANE
    w_cat = jnp.stack([w_hi.reshape(2, LANE, n_pair, LANE), w_lo.reshape(2, LANE, n_pair, LANE)],
                      axis=3).reshape(2, LANE, 2 * GLA_QK_DIM)
    gla = _gla(p, lr, w_cat, gla_b_up[0][:, None, :], gla_norm[0:1])
    xl, h2, logits = _out_proj(o_na, gla, od_w_out[0].astype(BF16), xa, mod_lat[1], mod_ctx[1],
                               norm_ffn[1:2], _router_split(router_w[1]), LAT_TILE,
                               T_LAT // LAT_TILE)
    spans, pair_tok, ys = _expert_ffn(h2, logits, moe_w_gate, moe_w_up, moe_w_down, 1, False,
                                      COMB_TILE)
    return _res_final(xl, spans, pair_tok, ys, mod_lat[1], norm_final[None])
```

```python
import functools

import numpy as np
import jax
import jax.numpy as jnp
from jax import lax
from jax.experimental import pallas as pl
from jax.experimental.pallas import tpu as pltpu

D_MODEL = 2048
BATCH = 4
T_LAT = 4096
T_CTX = 256
T_ALL = T_LAT + T_CTX
GRID_W = 64
GRID_ROWS = T_LAT // GRID_W
HEAD_DIM = 128
RET_HEADS = 8
RET_DIM = 1024
GMLP_DIM = 1024
GMLP_GROUPS = 8
GMLP_CHUNK = 128
NA_HEADS = 8
NA_DIM = 1024
WIN_ROWS = 8
WIN_COLS = 16
GLA_HEADS = 8
GLA_DK = 64
GLA_DV = 128
GLA_QK_DIM = 512
GLA_V_DIM = 1024
GLA_GATE_RANK = 16
GLA_TAU = 16.0
N_EXPERTS = 16
D_EXPERT = 2048
EC_CAPACITY = 2
ROPE_BASE = 10000.0
RMS_EPS = 1e-6
EVEN_COLS = 6144
ODD_COLS = 6176
ODD_MAIN_COLS = 6144
LANE = 128

ROW_TILE = 256
MIX_TILE = 544
LAT_TILE = 512
RET_CHUNK = 256
GLA_SUB = 64
GLA_SUPER = 256
GLA_PAIRS = 2
NA_HEADS_PER_STEP = 4
RET_HEADS_PER_STEP = 2
RET_UNROLL = 4
GLA_UNROLL = 2
NA_QROWS = 4
NA_QT = NA_QROWS * GRID_W
NA_KROWS = NA_QROWS + WIN_ROWS - 1
NA_KT = NA_KROWS * GRID_W
NEG_BIG = -1e30
MOE_TN = 512
MOE_DOWN_TN = 1024
GATE_ROWS = 16
MOE_GROUPS = 2
PAIR_WIN = 256
PAIR_CHUNK = 3
COMB_TILE = 256
VMEM_LIMIT = 56 * 1024 * 1024

F32 = jnp.float32
BF16 = jnp.bfloat16


def _cparams(n_axes):
    return pltpu.CompilerParams(dimension_semantics=("arbitrary",) * n_axes,
                                vmem_limit_bytes=VMEM_LIMIT)


def _dot(a, b):
    return jnp.dot(a, b, preferred_element_type=F32)


def _dot_nt(a, b):
    return lax.dot_general(a, b, (((1,), (1,)), ((), ())), preferred_element_type=F32)


def _dot_tn(a, b):
    return lax.dot_general(a, b, (((0,), (0,)), ((), ())), preferred_element_type=F32)


def _silu(x):
    return x * jax.nn.sigmoid(x)


def _rms(x):
    return x * lax.rsqrt(jnp.mean(x * x, axis=-1, keepdims=True) + RMS_EPS)


def _split2(x):
    hi = x.astype(BF16)
    lo = (x - hi.astype(F32)).astype(BF16)
    return hi, lo


def _proj_kernel(a_ref, w_ref, o_ref):
    o_ref[...] = _dot(a_ref[...], w_ref[...].astype(BF16)).astype(o_ref.dtype)


def _proj_matmul(a, w, layer, n_cols, tm, tn):
    m, k = a.shape
    return pl.pallas_call(
        _proj_kernel,
        grid=(n_cols // tn, m // tm),
        in_specs=[pl.BlockSpec((tm, k), lambda j, i: (i, 0)),
                  pl.BlockSpec((None, k, tn), lambda j, i: (layer, 0, j))],
        out_specs=pl.BlockSpec((tm, tn), lambda j, i: (i, j)),
        out_shape=jax.ShapeDtypeStruct((m, n_cols), BF16),
        compiler_params=_cparams(2),
        name="proj_matmul",
    )(a, w)


def _proj_t_kernel(a_ref, wt_ref, o_ref):
    o_ref[...] = _dot_nt(a_ref[...], wt_ref[...].astype(BF16)).astype(o_ref.dtype)


def _proj_matmul_t(a, wt, n_cols, tm, tn):
    m, k = a.shape
    return pl.pallas_call(
        _proj_t_kernel,
        grid=(n_cols // tn, m // tm),
        in_specs=[pl.BlockSpec((tm, k), lambda j, i: (i, 0)),
                  pl.BlockSpec((tn, k), lambda j, i: (j, 0))],
        out_specs=pl.BlockSpec((tm, tn), lambda j, i: (i, j)),
        out_shape=jax.ShapeDtypeStruct((m, n_cols), BF16),
        compiler_params=_cparams(2),
        name="proj_matmul_t",
    )(a, wt)


def _mod_mm_kernel(a_ref, w_ref, o_ref):
    o_ref[...] = _dot(a_ref[...], w_ref[...].astype(BF16))


def _mod_matmul(cond, ada_w, layer):
    m, k = cond.shape
    n = ada_w.shape[2]
    tn = 1024
    return pl.pallas_call(
        _mod_mm_kernel,
        grid=(n // tn,),
        in_specs=[pl.BlockSpec((m, k), lambda j: (0, 0)),
                  pl.BlockSpec((None, k, tn), lambda j: (layer, 0, j))],
        out_specs=pl.BlockSpec((m, tn), lambda j: (0, j)),
        out_shape=jax.ShapeDtypeStruct((m, n), F32),
        compiler_params=_cparams(1),
        name="mod_matmul",
    )(cond, ada_w)


def _lat_spec(which):
    return pl.BlockSpec((None, None, 1, D_MODEL), lambda b, i, *_: (b, which, 0, 0))


def _ctx_spec(which):
    return pl.BlockSpec((None, 1, D_MODEL), lambda b, i, *_: (which, 0, 0))


def _region_select(tile, lat_ref, ctx_ref, row0=0, n_rows=None, has_ctx=True):
    if not has_ctx:
        return lat_ref[...]
    if T_LAT % tile == 0:
        return jnp.where(pl.program_id(1) >= T_LAT // tile, ctx_ref[...], lat_ref[...])
    n_rows = tile if n_rows is None else n_rows
    rows = pl.program_id(1) * tile + row0 + lax.broadcasted_iota(jnp.int32, (n_rows, 1), 0)
    return jnp.where(rows >= T_LAT, ctx_ref[...], lat_ref[...])


def _modnorm_kernel(x_ref, ctx_ref, gain_ref, scl_ref, scc_ref, shl_ref, shc_ref, xa_ref, o_ref,
                    *, tile):
    is_ctx = pl.program_id(1) >= T_LAT // tile
    x = jnp.where(is_ctx, ctx_ref[...], x_ref[...])
    xa_ref[...] = x
    sc = _region_select(tile, scl_ref, scc_ref)
    sh = _region_select(tile, shl_ref, shc_ref)
    o_ref[...] = (_rms(x) * gain_ref[...] * (1.0 + sc) + sh).astype(o_ref.dtype)


def _modnorm(x, ctx, gain, mod_lat, mod_ctx, sc_i, sh_i):
    tile = T_CTX
    n_lat = T_LAT // tile
    row = pl.BlockSpec((None, tile, D_MODEL), lambda b, i: (b, i, 0))
    return pl.pallas_call(
        functools.partial(_modnorm_kernel, tile=tile),
        grid=(BATCH, T_ALL // tile),
        in_specs=[pl.BlockSpec((None, tile, D_MODEL), lambda b, i: (b, jnp.minimum(i, n_lat - 1), 0)),
                  pl.BlockSpec((None, tile, D_MODEL), lambda b, i: (b, jnp.maximum(i - n_lat, 0), 0)),
                  pl.BlockSpec((1, D_MODEL), lambda b, i: (0, 0)),
                  _lat_spec(sc_i), _ctx_spec(sc_i), _lat_spec(sh_i), _ctx_spec(sh_i)],
        out_specs=[row, row],
        out_shape=[jax.ShapeDtypeStruct((BATCH, T_ALL, D_MODEL), F32),
                   jax.ShapeDtypeStruct((BATCH, T_ALL, D_MODEL), BF16)],
        compiler_params=_cparams(2),
        name="modnorm",
    )(x, ctx, gain, mod_lat, mod_ctx, mod_lat, mod_ctx)


def _rotary(z, cos, s1, s2):
    return z * cos + pltpu.roll(z, 96, 1) * s1 + pltpu.roll(z, 32, 1) * s2


def _ret_kernel(lg_ref, q_ref, g_ref, k_ref, v_ref, cos_ref, s1_ref, s2_ref, gain_ref,
                o_ref, acc_ref, qs_ref, ks_ref):
    c = RET_CHUNK
    nh = RET_HEADS_PER_STEP
    scale = HEAD_DIM ** -0.5
    dist = (lax.broadcasted_iota(jnp.int32, (c, c), 0)
            - lax.broadcasted_iota(jnp.int32, (c, c), 1)).astype(F32)
    tcol = lax.broadcasted_iota(jnp.int32, (c, 1), 0).astype(F32)

    def head_consts(hp):
        h = pl.program_id(1) * nh + hp
        la_f, la_b = lg_ref[0, h], lg_ref[1, h]
        dmat = (jnp.where(dist >= 0.0, jnp.exp(la_f * jnp.maximum(dist, 0.0)), 0.0)
                + jnp.where(dist <= 0.0, jnp.exp(la_b * jnp.maximum(-dist, 0.0)), 0.0))
        fwd_dec = (jnp.exp(la_f * (tcol + 1.0)), jnp.exp(la_f * (c - 1.0 - tcol)),
                   jnp.exp(jnp.full((1, LANE), la_f * c, F32)))
        bwd_dec = (jnp.exp(la_b * (c - tcol)), jnp.exp(la_b * tcol),
                   jnp.exp(jnp.full((1, LANE), la_b * c, F32)))
        return dmat, fwd_dec, bwd_dec

    consts = [head_consts(hp) for hp in range(nh)]

    def fwd(i, sts, base, rot):
        start = pl.multiple_of(base + i * c, c)
        rows = pl.ds(start, c)
        if rot:
            cos, s1, s2 = cos_ref[rows, :], s1_ref[rows, :], s2_ref[rows, :]
        out = []
        for hp in range(nh):
            hl = pl.ds(hp * LANE, LANE)
            dmat, (qdec, kdec, cdec), _ = consts[hp]
            q = q_ref[rows, hl].astype(F32) * scale
            k = k_ref[rows, hl].astype(F32)
            v = v_ref[rows, hl]
            if rot:
                q = _rotary(q, cos, s1, s2)
                k = _rotary(k, cos, s1, s2)
            qs_ref[rows, hl] = q
            ks_ref[rows, hl] = k
            att = _dot_nt(q.astype(BF16), k.astype(BF16)) * dmat
            st = sts[hp]
            acc_ref[rows, hl] = (_dot(att.astype(BF16), v)
                                 + _dot_nt((q * qdec).astype(BF16), st.astype(BF16)))
            out.append(cdec * st + _dot_tn(v, (k * kdec).astype(BF16)))
        return tuple(out)

    def bwd(i, sts, base, n):
        start = pl.multiple_of(base + (n - 1 - i) * c, c)
        rows = pl.ds(start, c)
        out = []
        for hp in range(nh):
            hl = pl.ds(hp * LANE, LANE)
            _, _, (qdec, kdec, cdec) = consts[hp]
            q = qs_ref[rows, hl]
            k = ks_ref[rows, hl]
            v = v_ref[rows, hl]
            st = sts[hp]
            o = acc_ref[rows, hl] + _dot_nt((q * qdec).astype(BF16), st.astype(BF16))
            gate = g_ref[rows, hl].astype(F32)
            o_ref[rows, hl] = (_rms(o) * gain_ref[:, hl] * _silu(gate)).astype(o_ref.dtype)
            out.append(cdec * st + _dot_tn(v, (k * kdec).astype(BF16)))
        return tuple(out)

    zero = tuple(jnp.zeros((HEAD_DIM, HEAD_DIM), F32) for _ in range(nh))
    n_ctx, n_lat = T_CTX // c, T_LAT // c
    sts = lax.fori_loop(0, n_ctx, functools.partial(fwd, base=T_LAT, rot=False), zero)
    lax.fori_loop(0, n_lat, functools.partial(fwd, base=0, rot=True), sts, unroll=RET_UNROLL)
    sts = lax.fori_loop(0, n_ctx, functools.partial(bwd, base=T_LAT, n=n_ctx), zero)
    lax.fori_loop(0, n_lat, functools.partial(bwd, base=0, n=n_lat), sts, unroll=RET_UNROLL)


def _retention(p, log_g, cos, s1, s2, ret_norm):
    nh = RET_HEADS_PER_STEP
    hw = nh * LANE
    col = lambda off: pl.BlockSpec((None, T_ALL, hw),
                                   lambda b, h, lg: (b, 0, off * LANE // hw + h))
    tab = pl.BlockSpec((T_LAT, LANE), lambda b, h, lg: (0, 0))
    return pl.pallas_call(
        _ret_kernel,
        grid_spec=pltpu.PrefetchScalarGridSpec(
            num_scalar_prefetch=1,
            grid=(BATCH, RET_HEADS // nh),
            in_specs=[col(0), col(8), col(32), col(40), tab, tab, tab,
                      pl.BlockSpec((1, hw), lambda b, h, lg: (0, h))],
            out_specs=pl.BlockSpec((None, T_ALL, hw), lambda b, h, lg: (b, 0, h)),
            scratch_shapes=[pltpu.VMEM((T_ALL, hw), F32), pltpu.VMEM((T_ALL, hw), F32),
                            pltpu.VMEM((T_ALL, hw), F32)]),
        out_shape=jax.ShapeDtypeStruct((BATCH, T_ALL, RET_DIM), BF16),
        compiler_params=_cparams(2),
        name="retention",
    )(log_g, p, p, p, p, cos, s1, s2, ret_norm)


def _gmlp_kernel(u_ref, v_ref, gain_ref, ws_ref, bst_ref, o_ref):
    u = jax.nn.gelu(u_ref[...].astype(F32))
    v = _rms(jax.nn.gelu(v_ref[...].astype(F32))) * gain_ref[...]
    vb = v.astype(BF16)
    for n in range(ROW_TILE // GMLP_CHUNK):
        r0 = n * GMLP_CHUNK
        for g in range(GMLP_GROUPS):
            c0 = g * LANE
            mixed = (_dot(ws_ref[g], vb[r0:r0 + GMLP_CHUNK, c0:c0 + LANE])
                     + bst_ref[:, g:g + 1])
            o_ref[r0:r0 + GMLP_CHUNK, c0:c0 + LANE] = (
                u[r0:r0 + GMLP_CHUNK, c0:c0 + LANE] * mixed).astype(o_ref.dtype)


def _gmlp(p, gmlp_norm, ws_bf16, bs_t):
    return pl.pallas_call(
        _gmlp_kernel,
        grid=(BATCH, T_ALL // ROW_TILE),
        in_specs=[pl.BlockSpec((None, ROW_TILE, GMLP_DIM), lambda b, i: (b, i, 2)),
                  pl.BlockSpec((None, ROW_TILE, GMLP_DIM), lambda b, i: (b, i, 3)),
                  pl.BlockSpec((1, GMLP_DIM), lambda b, i: (0, 0)),
                  pl.BlockSpec((GMLP_GROUPS, GMLP_CHUNK, GMLP_CHUNK), lambda b, i: (0, 0, 0)),
                  pl.BlockSpec((GMLP_CHUNK, GMLP_GROUPS), lambda b, i: (0, 0))],
        out_specs=pl.BlockSpec((None, ROW_TILE, GMLP_DIM), lambda b, i: (b, i, 0)),
        out_shape=jax.ShapeDtypeStruct((BATCH, T_ALL, GMLP_DIM), BF16),
        compiler_params=_cparams(2),
        name="gmlp",
    )(p, p, gmlp_norm, ws_bf16, bs_t)


def _na_kernel(q_ref, k_ref, v_ref, bias_ref, o_ref):
    i = pl.program_id(2)
    w0 = jnp.clip(i * NA_QROWS - WIN_ROWS // 2, 0, GRID_ROWS - NA_KROWS)
    start = pl.multiple_of(w0 * GRID_W, GRID_W)
    scale = HEAD_DIM ** -0.5
    for hp in range(NA_HEADS_PER_STEP):
        hl = pl.ds(hp * LANE, LANE)
        q = q_ref[:, hl]
        s_win = _dot_nt(q, k_ref[pl.ds(start, NA_KT), hl]) * scale + bias_ref[hp]
        s_ctx = _dot_nt(q, k_ref[pl.ds(T_LAT, T_CTX), hl]) * scale
        m = jnp.maximum(jnp.max(s_win, axis=-1, keepdims=True),
                        jnp.max(s_ctx, axis=-1, keepdims=True))
        p_win = jnp.exp(s_win - m)
        p_ctx = jnp.exp(s_ctx - m)
        denom = jnp.sum(p_win, axis=-1, keepdims=True) + jnp.sum(p_ctx, axis=-1, keepdims=True)
        o = (_dot(p_win.astype(BF16), v_ref[pl.ds(start, NA_KT), hl])
             + _dot(p_ctx.astype(BF16), v_ref[pl.ds(T_LAT, T_CTX), hl]))
        o_ref[:, hl] = (o / denom).astype(o_ref.dtype)


def _na_geometry():
    n_blk = GRID_ROWS // NA_QROWS
    dr00 = []
    row_ok = np.zeros((3, NA_QROWS, NA_KROWS), bool)
    for kind, i_rep in enumerate((0, 1, n_blk - 1)):
        r0 = i_rep * NA_QROWS
        w0 = int(np.clip(r0 - WIN_ROWS // 2, 0, GRID_ROWS - NA_KROWS))
        r = r0 + np.arange(NA_QROWS)[:, None]
        kr = w0 + np.arange(NA_KROWS)[None, :]
        rstart = np.clip(r - WIN_ROWS // 2, 0, GRID_ROWS - WIN_ROWS)
        row_ok[kind] = (kr >= rstart) & (kr < rstart + WIN_ROWS)
        dr00.append(w0 - r0 + WIN_ROWS - 1)
    c = np.arange(GRID_W)[:, None]
    kc = np.arange(GRID_W)[None, :]
    cstart = np.clip(c - WIN_COLS // 2, 0, GRID_W - WIN_COLS)
    col_ok = (kc >= cstart) & (kc < cstart + WIN_COLS)
    valid = row_ok[:, :, None, :, None] & col_ok[None, None, :, None, :]
    return dr00, valid.reshape(3, NA_QT, NA_KT)


def _na_bias_table(rpb):
    n_dr, n_dc = 2 * WIN_ROWS - 1, 2 * WIN_COLS - 1
    w = jnp.concatenate([rpb[..., WIN_COLS - 1:], jnp.zeros((NA_HEADS, n_dr, LANE - n_dc), F32),
                         rpb[..., :WIN_COLS - 1]], axis=-1)
    toep = jnp.tile(w, (1, 1, GRID_W))[..., :GRID_W * (LANE - 1)]
    toep = toep.reshape(NA_HEADS, n_dr, GRID_W, LANE - 1)[..., :GRID_W]
    dr00, valid = _na_geometry()
    front = max(0, NA_QROWS - 1 - min(dr00))
    back = max(0, max(dr00) + NA_KROWS - n_dr)
    toep = jnp.pad(toep.transpose(0, 2, 1, 3), ((0, 0), (0, 0), (front, back), (0, 0)))
    toep = toep.astype(BF16).reshape(NA_HEADS, GRID_W, -1)
    kinds = []
    for d0 in dr00:
        starts = [(d0 - rl + front) * GRID_W for rl in range(NA_QROWS)]
        kinds.append(jnp.stack([toep[:, :, s:s + NA_KT] for s in starts], axis=1))
    tab = jnp.stack(kinds, axis=0).reshape(3, NA_HEADS, NA_QT, NA_KT)
    return jnp.where(jnp.asarray(valid)[:, None], tab, jnp.asarray(NEG_BIG, BF16))


def _na(p, bias):
    n_blk = T_LAT // NA_QT
    nh = NA_HEADS_PER_STEP
    hw = nh * LANE
    k0, v0 = 20 * LANE // hw, 28 * LANE // hw
    kind = lambda i: jnp.where(i == 0, 0, jnp.where(i == n_blk - 1, 2, 1))
    return pl.pallas_call(
        _na_kernel,
        grid=(BATCH, NA_HEADS // nh, n_blk),
        in_specs=[pl.BlockSpec((None, NA_QT, hw), lambda b, h, i: (b, i, h)),
                  pl.BlockSpec((None, T_ALL, hw), lambda b, h, i: (b, 0, k0 + h)),
                  pl.BlockSpec((None, T_ALL, hw), lambda b, h, i: (b, 0, v0 + h)),
                  pl.BlockSpec((None, nh, NA_QT, NA_KT), lambda b, h, i: (kind(i), h, 0, 0))],
        out_specs=pl.BlockSpec((None, NA_QT, hw), lambda b, h, i: (b, i, h)),
        out_shape=jax.ShapeDtypeStruct((BATCH, T_LAT, NA_DIM), BF16),
        compiler_params=_cparams(3),
        name="nbr_attention",
    )(p, p, p, bias)


def _gla_kernel(q_ref, k_ref, v_ref, g_ref, lr_ref, wcat_ref, bup_ref, gain_ref,
                o_ref, acc_ref):
    r, c = GLA_SUPER, GLA_SUB
    n_sub = r // c
    dv2 = 2 * GLA_DV
    qscale = GLA_DK ** -0.5
    row = lax.broadcasted_iota(jnp.int32, (r, r), 0)
    col = lax.broadcasted_iota(jnp.int32, (r, r), 1)
    sub_shift = c.bit_length() - 1
    same = (row >> sub_shift) == (col >> sub_shift)
    blk_ones = jnp.where(same, 1.0, 0.0).astype(BF16)
    second_head_lane = lax.broadcasted_iota(jnp.int32, (1, LANE), 1) >= GLA_DK
    st_mask = ((lax.broadcasted_iota(jnp.int32, (dv2, LANE), 0) >= GLA_DV)
               == (lax.broadcasted_iota(jnp.int32, (dv2, LANE), 1) >= GLA_DK))

    for d in range(2):
        tri = same & ((row >= col) if d == 0 else (col >= row))
        tri_b = jnp.where(tri, 1.0, 0.0).astype(BF16)

        def pair_step(hp, start, lr, st, d=d, tri=tri, tri_b=tri_b):
            kl = pl.ds(hp * LANE, LANE)
            vl = pl.ds(hp * dv2, dv2)
            zz = _dot(lr, wcat_ref[d, :, pl.ds(hp * 2 * LANE, 2 * LANE)])
            z = zz[:, :LANE] + zz[:, LANE:] + bup_ref[d, :, kl]
            la = (jnp.minimum(z, 0.0) - jnp.log(1.0 + jnp.exp(-jnp.abs(z)))) * (1.0 / GLA_TAU)
            hl = jnp.concatenate(_split2(la), axis=1)
            bb = _dot(tri_b, hl)
            b = bb[:, :LANE] + bb[:, LANE:]
            bb = _dot(blk_ones, hl)
            bt = bb[:, :LANE] + bb[:, LANE:]
            q = q_ref[pl.ds(start, r), kl].astype(F32) * qscale
            k = k_ref[pl.ds(start, r), kl].astype(F32)
            v = v_ref[pl.ds(start, r), vl]
            qi = q * jnp.exp(b)
            qib = qi.astype(BF16)
            ki = (k * jnp.exp(-b)).astype(BF16)
            ko = (k * jnp.exp(bt - b)).astype(BF16)
            dec = jnp.exp(bt)
            outs = []
            for j in range(2):
                qm = jnp.where(second_head_lane == (j == 1), qi, 0.0).astype(BF16)
                att = jnp.where(tri, _dot_nt(qm, ki), 0.0)
                outs.append(_dot(att.astype(BF16), v[:, j * GLA_DV:(j + 1) * GLA_DV]))
            o_intra = jnp.concatenate(outs, axis=1)
            parts = [None] * n_sub
            for s in (range(n_sub) if d == 0 else reversed(range(n_sub))):
                lo_r, hi_r = s * c, (s + 1) * c
                parts[s] = o_intra[lo_r:hi_r] + _dot_nt(qib[lo_r:hi_r], st.astype(BF16))
                st = (st * dec[lo_r:lo_r + 1, :]
                      + jnp.where(st_mask, _dot_tn(v[lo_r:hi_r], ko[lo_r:hi_r]), 0.0))
            o = jnp.concatenate(parts, axis=0)
            if d == 0:
                acc_ref[pl.ds(start, r), vl] = o
            else:
                o = acc_ref[pl.ds(start, r), vl] + o
                gate = g_ref[pl.ds(start, r), vl].astype(F32)
                on = (jnp.concatenate([_rms(o[:, :GLA_DV]), _rms(o[:, GLA_DV:])], axis=1)
                      * gain_ref[:, vl])
                o_ref[pl.ds(start, r), vl] = (on * _silu(gate)).astype(o_ref.dtype)
            return st

        def step(i, sts, base, n, d=d):
            idx = i if d == 0 else n - 1 - i
            start = pl.multiple_of(base + idx * r, r)
            lr = lr_ref[pl.ds(start, r), :]
            return tuple(pair_step(hp, start, lr, sts[hp]) for hp in range(GLA_PAIRS))

        sts = tuple(jnp.zeros((dv2, LANE), F32) for _ in range(GLA_PAIRS))
        sts = lax.fori_loop(0, T_CTX // r, functools.partial(step, base=T_LAT, n=T_CTX // r), sts)
        lax.fori_loop(0, T_LAT // r, functools.partial(step, base=0, n=T_LAT // r), sts,
                      unroll=GLA_UNROLL)


def _gla(p, lr, w_cat, b_up, gla_norm):
    kw = GLA_PAIRS * LANE
    vw = GLA_PAIRS * 2 * GLA_DV
    q0, k0 = 8 * LANE // kw, 36 * LANE // kw
    v0, g0 = 40 * LANE // vw, 12 * LANE // vw
    return pl.pallas_call(
        _gla_kernel,
        grid=(BATCH, GLA_HEADS // (2 * GLA_PAIRS)),
        in_specs=[pl.BlockSpec((None, T_ALL, kw), lambda b, j: (b, 0, q0 + j)),
                  pl.BlockSpec((None, T_ALL, kw), lambda b, j: (b, 0, k0 + j)),
                  pl.BlockSpec((None, T_ALL, vw), lambda b, j: (b, 0, v0 + j)),
                  pl.BlockSpec((None, T_ALL, vw), lambda b, j: (b, 0, g0 + j)),
                  pl.BlockSpec((None, T_ALL, LANE), lambda b, j: (b, 0, 0)),
                  pl.BlockSpec((2, LANE, 2 * kw), lambda b, j: (0, 0, j)),
                  pl.BlockSpec((2, 1, kw), lambda b, j: (0, 0, j)),
                  pl.BlockSpec((1, vw), lambda b, j: (0, j))],
        out_specs=pl.BlockSpec((None, T_ALL, vw), lambda b, j: (b, 0, j)),
        out_shape=jax.ShapeDtypeStruct((BATCH, T_ALL, GLA_V_DIM), BF16),
        scratch_shapes=[pltpu.VMEM((T_ALL, vw), F32)],
        compiler_params=_cparams(2),
        name="gla",
    )(p, p, p, p, lr, w_cat, b_up, gla_norm)


def _out_kernel(a1_ref, a2_ref, w1_ref, w2_ref, x_ref, gl_ref, gc_ref, gain_ref,
                scl_ref, scc_ref, shl_ref, shc_ref, rw_ref, xo_ref, h_ref, lg_ref, *, tile,
                has_ctx):
    half = tile // 2
    for r0 in (0, half):
        rows = pl.ds(r0, half)
        sel = functools.partial(_region_select, tile, row0=r0, n_rows=half, has_ctx=has_ctx)
        y = _dot(a1_ref[rows, :], w1_ref[...]) + _dot(a2_ref[rows, :], w2_ref[...])
        xn = x_ref[rows, :] + sel(gl_ref, gc_ref) * y
        xo_ref[rows, :] = xn
        h = _rms(xn) * gain_ref[...] * (1.0 + sel(scl_ref, scc_ref)) + sel(shl_ref, shc_ref)
        h_hi = h.astype(BF16)
        h_ref[rows, :] = h_hi
        h_lo = (h - h_hi.astype(F32)).astype(BF16)
        lg_ref[rows, :] = _dot(h_hi, rw_ref[...]) + _dot(h_lo, rw_ref[...])


def _out_proj(a1, a2, w_out_bf16, x, mod_lat, mod_ctx, norm_gain, rw_split, tile, n_tiles):
    half = w_out_bf16.shape[0] // 2
    t_out = n_tiles * tile
    row = lambda width: pl.BlockSpec((None, tile, width), lambda b, i: (b, i, 0))
    return pl.pallas_call(
        functools.partial(_out_kernel, tile=tile, has_ctx=t_out > T_LAT),
        grid=(BATCH, n_tiles),
        in_specs=[row(half), row(half),
                  pl.BlockSpec((half, D_MODEL), lambda b, i: (0, 0)),
                  pl.BlockSpec((half, D_MODEL), lambda b, i: (1, 0)),
                  row(D_MODEL), _lat_spec(2), _ctx_spec(2),
                  pl.BlockSpec((1, D_MODEL), lambda b, i: (0, 0)),
                  _lat_spec(4), _ctx_spec(4), _lat_spec(3), _ctx_spec(3),
                  pl.BlockSpec((D_MODEL, 2 * N_EXPERTS), lambda b, i: (0, 0))],
        out_specs=[row(D_MODEL), row(D_MODEL), row(2 * N_EXPERTS)],
        out_shape=[jax.ShapeDtypeStruct((BATCH, t_out, D_MODEL), F32),
                   jax.ShapeDtypeStruct((BATCH, t_out, D_MODEL), BF16),
                   jax.ShapeDtypeStruct((BATCH, t_out, 2 * N_EXPERTS), F32)],
        compiler_params=_cparams(2),
        name="out_proj",
    )(a1, a2, w_out_bf16, w_out_bf16, x, mod_lat, mod_ctx, norm_gain,
      mod_lat, mod_ctx, mod_lat, mod_ctx, rw_split)


def _moe_up_kernel(x_ref, wg_ref, wu_ref, o_ref):
    x = x_ref[...]
    a = _dot(x, wg_ref[...].astype(BF16))
    u = _dot(x, wu_ref[...].astype(BF16))
    o_ref[...] = (_silu(a) * u).astype(o_ref.dtype)


def _moe_down_kernel(h_ref, wd_ref, g_ref, o_ref):
    gcol = _dot_tn(g_ref[...], jnp.ones((GATE_ROWS, LANE), BF16))
    y = _dot(h_ref[...], wd_ref[...].astype(BF16))
    for c in range(o_ref.shape[1] // LANE):
        cols = pl.ds(c * LANE, LANE)
        o_ref[:, cols] = (y[:, c * LANE:(c + 1) * LANE] * gcol).astype(o_ref.dtype)


def _moe(xs, gates, w_gate, w_up, w_down, layer):
    e, m, _ = xs.shape
    tn = MOE_TN
    wspec = pl.BlockSpec((None, None, D_MODEL, tn), lambda e, f: (layer, e, 0, f))
    hmid = pl.pallas_call(
        _moe_up_kernel,
        grid=(e, D_EXPERT // tn),
        in_specs=[pl.BlockSpec((None, m, D_MODEL), lambda e, f: (e, 0, 0)), wspec, wspec],
        out_specs=pl.BlockSpec((None, m, tn), lambda e, f: (e, 0, f)),
        out_shape=jax.ShapeDtypeStruct((e, m, D_EXPERT), BF16),
        compiler_params=_cparams(2),
        name="moe_up",
    )(xs, w_gate, w_up)
    tn = MOE_DOWN_TN
    return pl.pallas_call(
        _moe_down_kernel,
        grid=(e, D_MODEL // tn),
        in_specs=[pl.BlockSpec((None, m, D_EXPERT), lambda e, f: (e, 0, 0)),
                  pl.BlockSpec((None, None, D_EXPERT, tn), lambda e, f: (layer, e, 0, f)),
                  pl.BlockSpec((None, GATE_ROWS, m), lambda e, f: (e, 0, 0))],
        out_specs=pl.BlockSpec((None, m, tn), lambda e, f: (e, 0, f)),
        out_shape=jax.ShapeDtypeStruct((e, m, D_MODEL), BF16),
        compiler_params=_cparams(2),
        name="moe_down",
    )(hmid, w_down, gates)


def _combine_moe(span_ref, pair_tok_ref, ys_refs, acc_ref, buf_ref, sem, *, tile, n_pairs):
    b, i = pl.program_id(0), pl.program_id(1)
    n_tiles = pl.num_programs(1)
    step = b * n_tiles + i
    n_steps = pl.num_programs(0) * n_tiles
    w = PAIR_WIN
    shift = w.bit_length() - 1
    last_start = n_pairs // w - PAIR_CHUNK
    group_size = BATCH // len(ys_refs)

    def first_window(s):
        return jnp.minimum(span_ref[2 * s] >> shift, last_start)

    def buf_rows(n_win, slot):
        return buf_ref.at[slot, pl.ds(0, n_win * w), :]

    def start_copy(sample, win, n_win, slot):
        for gi, ys_hbm in enumerate(ys_refs):
            @pl.when((sample >= gi * group_size) & (sample < (gi + 1) * group_size))
            def _(gi=gi, ys_hbm=ys_hbm):
                row0 = (sample - gi * group_size) * n_pairs + win * w
                pltpu.make_async_copy(ys_hbm.at[pl.ds(row0, n_win * w), :],
                                      buf_rows(n_win, slot), sem.at[slot]).start()

    def wait_copy(n_win, slot):
        pltpu.make_async_copy(ys_refs[0].at[pl.ds(0, n_win * w), :],
                              buf_rows(n_win, slot), sem.at[slot]).wait()

    slot = step & 1
    w0 = first_window(step)

    @pl.when(step == 0)
    def _():
        start_copy(b, w0, PAIR_CHUNK, slot)

    @pl.when(step + 1 < n_steps)
    def _():
        next_sample = jnp.where(i + 1 < n_tiles, b, b + 1)
        start_copy(next_sample, first_window(step + 1), PAIR_CHUNK, 1 - slot)

    wait_copy(PAIR_CHUNK, slot)
    tok = i * tile + lax.broadcasted_iota(jnp.int32, (tile, 1), 0)

    def select(win):
        return jnp.where(tok == pair_tok_ref[pl.ds(win, 1), :], 1.0, 0.0).astype(BF16)

    sel = jnp.concatenate([select(w0 + c) for c in range(PAIR_CHUNK)], axis=1)
    acc_ref[...] = _dot(sel, buf_ref[slot])

    hi = span_ref[2 * step + 1]
    end_win = jnp.where(hi > 0, ((hi - 1) >> shift) + 1, 0)

    def body(win, carry):
        start_copy(b, win, 1, slot)
        wait_copy(1, slot)
        acc_ref[...] += _dot(select(win), buf_ref[slot, pl.ds(0, w), :])
        return carry

    lax.fori_loop(w0 + PAIR_CHUNK, end_win, body, 0)
    return acc_ref[...]


def _res_mid_kernel(span_ref, x_ref, pair_tok_ref, *rest, tile, n_pairs):
    ys_refs, rest = rest[:MOE_GROUPS], rest[MOE_GROUPS:]
    (gl_ref, gc_ref, gain_ref, scl_ref, scc_ref, shl_ref, shc_ref,
     xo_ref, h_ref, acc_ref, buf_ref, sem) = rest
    moe = _combine_moe(span_ref, pair_tok_ref, ys_refs, acc_ref, buf_ref, sem,
                       tile=tile, n_pairs=n_pairs)
    xn = x_ref[...] + _region_select(tile, gl_ref, gc_ref) * moe
    xo_ref[...] = xn
    h_ref[...] = (_rms(xn) * gain_ref[...] * (1.0 + _region_select(tile, scl_ref, scc_ref))
                  + _region_select(tile, shl_ref, shc_ref)).astype(h_ref.dtype)


def _res_final_kernel(span_ref, x_ref, pair_tok_ref, *rest, tile, n_pairs):
    ys_refs, rest = rest[:MOE_GROUPS], rest[MOE_GROUPS:]
    g_ref, gain_ref, o_ref, acc_ref, buf_ref, sem = rest
    moe = _combine_moe(span_ref, pair_tok_ref, ys_refs, acc_ref, buf_ref, sem,
                       tile=tile, n_pairs=n_pairs)
    o_ref[...] = _rms(x_ref[...] + g_ref[...] * moe) * gain_ref[...]


def _combine_call(kernel_fn, name, tile, t_rows, spans, x, pair_tok, ys, params, param_specs,
                  out_dtypes):
    n_pairs = pair_tok.shape[1] * PAIR_WIN
    row = pl.BlockSpec((None, tile, D_MODEL), lambda b, i, *_: (b, i, 0))
    return pl.pallas_call(
        functools.partial(kernel_fn, tile=tile, n_pairs=n_pairs),
        grid_spec=pltpu.PrefetchScalarGridSpec(
            num_scalar_prefetch=1,
            grid=(BATCH, t_rows // tile),
            in_specs=[row,
                      pl.BlockSpec((None,) + pair_tok.shape[1:], lambda b, i, *_: (b, 0, 0))]
                     + [pl.BlockSpec(memory_space=pl.ANY)] * len(ys) + param_specs,
            out_specs=[row] * len(out_dtypes),
            scratch_shapes=[pltpu.VMEM((tile, D_MODEL), F32),
                            pltpu.VMEM((2, PAIR_CHUNK * PAIR_WIN, D_MODEL), BF16),
                            pltpu.SemaphoreType.DMA((2,))]),
        out_shape=[jax.ShapeDtypeStruct((BATCH, t_rows, D_MODEL), dt) for dt in out_dtypes],
        compiler_params=_cparams(2),
        name=name,
    )(spans, x, pair_tok, *ys, *params)


def _res_mid(x, spans, pair_tok, ys, mod_lat, mod_ctx, next_lat, next_ctx, norm_gain_next):
    specs = [_lat_spec(5), _ctx_spec(5), pl.BlockSpec((1, D_MODEL), lambda b, i, *_: (0, 0)),
             _lat_spec(1), _ctx_spec(1), _lat_spec(0), _ctx_spec(0)]
    params = (mod_lat, mod_ctx, norm_gain_next, next_lat, next_ctx, next_lat, next_ctx)
    return _combine_call(_res_mid_kernel, "ffn_combine_residual_prenorm", COMB_TILE, T_ALL, spans,
                         x, pair_tok, ys, params, specs, (F32, BF16))


def _res_final(x, spans, pair_tok, ys, mod_lat, norm_final):
    specs = [_lat_spec(5), pl.BlockSpec((1, D_MODEL), lambda b, i, *_: (0, 0))]
    return _combine_call(_res_final_kernel, "ffn_combine_residual_final_norm", COMB_TILE, T_LAT,
                         spans, x, pair_tok, ys, (mod_lat, norm_final), specs, (F32,))[0]


def _rope_tables():
    half = HEAD_DIM // 2
    nf = half // 2
    t = np.arange(T_LAT)
    rows, cols = t // GRID_W, t % GRID_W
    freq = (np.float32(ROPE_BASE) ** (-np.arange(nf, dtype=np.float32) / np.float32(nf))).astype(np.float32)
    ang_r = rows.astype(np.float32)[:, None] * freq
    ang_c = cols.astype(np.float32)[:, None] * freq
    zeros = np.zeros_like(ang_r)
    cos = np.concatenate([np.cos(ang_r), np.cos(ang_r), np.cos(ang_c), np.cos(ang_c)], axis=1)
    s1 = np.concatenate([-np.sin(ang_r), zeros, -np.sin(ang_c), zeros], axis=1)
    s2 = np.concatenate([zeros, np.sin(ang_r), zeros, np.sin(ang_c)], axis=1)
    return tuple(jnp.asarray(a, F32) for a in (cos, s1, s2))


def _router_split(rw):
    hi = rw.astype(BF16)
    lo = (rw - hi.astype(F32)).astype(BF16)
    return jnp.concatenate([hi, lo], axis=1)


def _route(logits, t0, t1, cap):
    aff = jax.nn.softmax(logits[:, t0:t1], axis=-1)
    g, idx = lax.top_k(aff.transpose(0, 2, 1), cap)
    return g, idx + t0


def _expert_ffn(h, logits2, w_gate, w_up, w_down, layer, with_ctx, tile):
    logits = logits2[..., :N_EXPERTS] + logits2[..., N_EXPERTS:]
    g, idx = _route(logits, 0, T_LAT, EC_CAPACITY * T_LAT // N_EXPERTS)
    if with_ctx:
        gc, idxc = _route(logits, T_LAT, T_ALL, EC_CAPACITY * T_CTX // N_EXPERTS)
        g = jnp.concatenate([g, gc], axis=-1)
        idx = jnp.concatenate([idx, idxc], axis=-1)
    cap = idx.shape[-1]
    t_rows = h.shape[1]
    n_pairs = N_EXPERTS * cap
    assert n_pairs % PAIR_WIN == 0
    h2d = h.reshape(BATCH * t_rows, D_MODEL)
    nb = BATCH // MOE_GROUPS
    cap_lat = EC_CAPACITY * T_LAT // N_EXPERTS
    ys_groups, pair_toks = [], []
    for b0 in range(0, BATCH, nb):
        idx_g, g_g = idx[b0:b0 + nb], g[b0:b0 + nb]
        flat = (idx_g + jnp.arange(b0, b0 + nb)[:, None, None] * t_rows)
        flat = flat.transpose(1, 0, 2).reshape(N_EXPERTS, -1)
        g_rows = g_g.transpose(1, 0, 2).reshape(N_EXPERTS, 1, nb * cap)
        g_hi = g_rows.astype(BF16)
        g_mid = (g_rows - g_hi.astype(F32)).astype(BF16)
        g_lo = (g_rows - g_hi.astype(F32) - g_mid.astype(F32)).astype(BF16)
        gates = jnp.concatenate([g_hi, g_mid, g_lo,
                                 jnp.zeros((N_EXPERTS, GATE_ROWS - 3, nb * cap), BF16)], axis=1)
        y = _moe(h2d[flat], gates, w_gate, w_up, w_down, layer)
        src = (jnp.arange(N_EXPERTS)[None, :, None] * (nb * cap)
               + jnp.arange(nb)[:, None, None] * cap + jnp.arange(cap)[None, None, :])
        parts = []
        for sl in ((slice(0, cap_lat), slice(cap_lat, cap)) if with_ctx else (slice(0, cap),)):
            parts.append(lax.sort((idx_g[..., sl].reshape(nb, -1), src[..., sl].reshape(nb, -1)),
                                  dimension=1, num_keys=1))
        pair_toks.append(jnp.concatenate([p[0] for p in parts], axis=1))
        src = jnp.concatenate([p[1] for p in parts], axis=1)
        ys_groups.append(y.reshape(-1, D_MODEL)[src.reshape(-1)])
    pair_tok = jnp.concatenate(pair_toks, axis=0)
    bounds = jnp.arange(0, t_rows + tile, tile)
    below = jnp.sum(pair_tok[:, :, None] < bounds[None, None, :], axis=1)
    spans = jnp.stack([below[:, :-1], below[:, 1:]], axis=-1).reshape(-1).astype(jnp.int32)
    return spans, pair_tok.reshape(BATCH, n_pairs // PAIR_WIN, PAIR_WIN), ys_groups


def kernel(x, c, ctx, c_ctx, ada_w, ada_b, norm_mix, norm_ffn, norm_final, ev_w_in, ev_w_out,
           ret_gamma_logit, ret_norm, gmlp_norm, gmlp_ws, gmlp_bs, od_w_in, od_w_out, na_rpb,
           gla_w_up, gla_b_up, gla_norm, router_w, moe_w_gate, moe_w_up, moe_w_down):
    assert x.shape == (BATCH, T_LAT, D_MODEL) and ctx.shape == (BATCH, T_CTX, D_MODEL)
    assert ada_w.shape[0] == 2 and ev_w_in.shape[0] == 1 and od_w_in.shape[0] == 1

    cond = jnp.concatenate([jax.nn.silu(c), jax.nn.silu(c_ctx)[None],
                            jnp.zeros((11, D_MODEL), F32)], axis=0).astype(BF16)
    mod_lat, mod_ctx = [], []
    for l in range(2):
        m = _mod_matmul(cond, ada_w, l) + ada_b[l]
        m = m[:BATCH + 1].reshape(BATCH + 1, 6, 1, D_MODEL)
        mod_lat.append(m[:BATCH])
        mod_ctx.append(m[BATCH])

    cos, s1, s2 = _rope_tables()

    xa, h = _modnorm(x, ctx, norm_mix[0:1], mod_lat[0], mod_ctx[0], 1, 0)
    p = _proj_matmul(h.reshape(BATCH * T_ALL, D_MODEL), ev_w_in, 0, EVEN_COLS, 2176, 1024)
    p = p.reshape(BATCH, T_ALL, EVEN_COLS)
    log_g = jax.nn.log_sigmoid(ret_gamma_logit[0].astype(F32))
    ret = _retention(p, log_g, cos, s1, s2, ret_norm[0:1])
    gm = _gmlp(p, gmlp_norm[0:1], gmlp_ws[0].astype(BF16), gmlp_bs[0].T)
    xa, h2, logits = _out_proj(ret, gm, ev_w_out[0].astype(BF16), xa, mod_lat[0], mod_ctx[0],
                               norm_ffn[0:1], _router_split(router_w[0]), MIX_TILE,
                               T_ALL // MIX_TILE)
    spans, pair_tok, ys = _expert_ffn(h2, logits, moe_w_gate, moe_w_up, moe_w_down, 0, True,
                                      COMB_TILE)
    xa, h = _res_mid(xa, spans, pair_tok, ys, mod_lat[0], mod_ctx[0], mod_lat[1], mod_ctx[1],
                     norm_mix[1:2])

    h2d = h.reshape(BATCH * T_ALL, D_MODEL)
    w_in_t = od_w_in[0].T
    p = _proj_matmul_t(h2d, w_in_t, ODD_MAIN_COLS, 2176, 1024)
    p = p.reshape(BATCH, T_ALL, ODD_MAIN_COLS)
    w_lr_t = jnp.pad(w_in_t[ODD_MAIN_COLS:], ((0, LANE - 2 * GLA_GATE_RANK), (0, 0))).astype(BF16)
    lr = _proj_matmul_t(h2d, w_lr_t, LANE, 2176, LANE).reshape(BATCH, T_ALL, LANE)
    o_na = _na(p, _na_bias_table(na_rpb[0]))
    w_up_full = jnp.zeros((2, LANE, GLA_QK_DIM), F32)
    for d in range(2):
        w_up_full = w_up_full.at[d, d * GLA_GATE_RANK:(d + 1) * GLA_GATE_RANK].set(gla_w_up[0, d])
    w_hi = w_up_full.astype(BF16)
    w_lo = (w_up_full - w_hi.astype(F32)).astype(BF16)
    n_pair = GLA_QK_DIM // LANE
    w_cat = jnp.stack([w_hi.reshape(2, LANE, n_pair, LANE), w_lo.reshape(2, LANE, n_pair, LANE)],
                      axis=3).reshape(2, LANE, 2 * GLA_QK_DIM)
    gla = _gla(p, lr, w_cat, gla_b_up[0][:, None, :], gla_norm[0:1])
    xl, h2, logits = _out_proj(o_na, gla, od_w_out[0].astype(BF16), xa, mod_lat[1], mod_ctx[1],
                               norm_ffn[1:2], _router_split(router_w[1]), LAT_TILE,
                               T_LAT // LAT_TILE)
    spans, pair_tok, ys = _expert_ffn(h2, logits, moe_w_gate, moe_w_up, moe_w_down, 1, False,
                                      COMB_TILE)
    return _res_final(xl, spans, pair_tok, ys, mod_lat[1], norm_final[None])
```

```python
import functools

import numpy as np
import jax
import jax.numpy as jnp
from jax import lax
from jax.experimental import pallas as pl
from jax.experimental.pallas import tpu as pltpu

D_MODEL = 2048
BATCH = 4
T_LAT = 4096
T_CTX = 256
T_ALL = T_LAT + T_CTX
GRID_W = 64
GRID_ROWS = T_LAT // GRID_W
HEAD_DIM = 128
RET_HEADS = 8
RET_DIM = 1024
GMLP_DIM = 1024
GMLP_GROUPS = 8
GMLP_CHUNK = 128
NA_HEADS = 8
NA_DIM = 1024
WIN_ROWS = 8
WIN_COLS = 16
GLA_HEADS = 8
GLA_DK = 64
GLA_DV = 128
GLA_QK_DIM = 512
GLA_V_DIM = 1024
GLA_GATE_RANK = 16
GLA_TAU = 16.0
N_EXPERTS = 16
D_EXPERT = 2048
EC_CAPACITY = 2
ROPE_BASE = 10000.0
RMS_EPS = 1e-6
EVEN_COLS = 6144
ODD_COLS = 6176
ODD_MAIN_COLS = 6144
LANE = 128

ROW_TILE = 256
MIX_TILE = 544
LAT_TILE = 512
RET_CHUNK = 256
GLA_SUB = 64
GLA_SUPER = 256
GLA_PAIRS = 2
NA_HEADS_PER_STEP = 4
RET_HEADS_PER_STEP = 2
RET_UNROLL = 4
GLA_UNROLL = 2
NA_QROWS = 4
NA_QT = NA_QROWS * GRID_W
NA_KROWS = NA_QROWS + WIN_ROWS - 1
NA_KT = NA_KROWS * GRID_W
NEG_BIG = -1e30
MOE_TN = 512
MOE_DOWN_TN = 1024
GATE_ROWS = 16
PAIR_WIN = 256
PAIR_CHUNK = 3
COMB_TILE = 256
VMEM_LIMIT = 56 * 1024 * 1024

F32 = jnp.float32
BF16 = jnp.bfloat16


def _offsets(names, widths):
    out, col = {}, 0
    for name, width in zip(names, widths):
        out[name] = col
        col += width
    return out


EVEN_OFF = _offsets(("ret_q", "ret_gate", "gmlp_u", "gmlp_v", "ret_k", "ret_v"),
                    (RET_DIM, RET_DIM, GMLP_DIM, GMLP_DIM, RET_DIM, RET_DIM))
ODD_OFF = _offsets(("na_q", "gla_q", "gla_gate", "na_k", "na_v", "gla_k", "gla_v"),
                   (NA_DIM, GLA_QK_DIM, GLA_V_DIM, NA_DIM, NA_DIM, GLA_QK_DIM, GLA_V_DIM))


def _block_index(col, width):
    index, rem = divmod(col, width)
    assert rem == 0, (col, width)
    return index


def _cparams(n_axes):
    return pltpu.CompilerParams(dimension_semantics=("arbitrary",) * n_axes,
                                vmem_limit_bytes=VMEM_LIMIT)


def _dot(a, b):
    return jnp.dot(a, b, preferred_element_type=F32)


def _dot_nt(a, b):
    return lax.dot_general(a, b, (((1,), (1,)), ((), ())), preferred_element_type=F32)


def _dot_tn(a, b):
    return lax.dot_general(a, b, (((0,), (0,)), ((), ())), preferred_element_type=F32)


def _silu(x):
    return x * jax.nn.sigmoid(x)


def _rms(x):
    return x * lax.rsqrt(jnp.mean(x * x, axis=-1, keepdims=True) + RMS_EPS)


def _split2(x):
    hi = x.astype(BF16)
    lo = (x - hi.astype(F32)).astype(BF16)
    return hi, lo


def _proj_kernel(a_ref, w_ref, o_ref):
    o_ref[...] = _dot(a_ref[...], w_ref[...].astype(BF16)).astype(o_ref.dtype)


def _proj_matmul(a, w, layer, n_cols, tm, tn):
    m, k = a.shape
    return pl.pallas_call(
        _proj_kernel,
        grid=(n_cols // tn, m // tm),
        in_specs=[pl.BlockSpec((tm, k), lambda j, i: (i, 0)),
                  pl.BlockSpec((None, k, tn), lambda j, i: (layer, 0, j))],
        out_specs=pl.BlockSpec((tm, tn), lambda j, i: (i, j)),
        out_shape=jax.ShapeDtypeStruct((m, n_cols), BF16),
        compiler_params=_cparams(2),
        name="proj_matmul",
    )(a, w)


def _proj_t_kernel(a_ref, wt_ref, o_ref):
    o_ref[...] = _dot_nt(a_ref[...], wt_ref[...].astype(BF16)).astype(o_ref.dtype)


def _proj_matmul_t(a, wt, n_cols, tm, tn):
    m, k = a.shape
    return pl.pallas_call(
        _proj_t_kernel,
        grid=(n_cols // tn, m // tm),
        in_specs=[pl.BlockSpec((tm, k), lambda j, i: (i, 0)),
                  pl.BlockSpec((tn, k), lambda j, i: (j, 0))],
        out_specs=pl.BlockSpec((tm, tn), lambda j, i: (i, j)),
        out_shape=jax.ShapeDtypeStruct((m, n_cols), BF16),
        compiler_params=_cparams(2),
        name="proj_matmul_t",
    )(a, wt)


def _mod_mm_kernel(a_ref, w_ref, o_ref):
    o_ref[...] = _dot(a_ref[...], w_ref[...].astype(BF16))


def _mod_matmul(cond, ada_w, layer):
    m, k = cond.shape
    n = ada_w.shape[2]
    tn = 1024
    return pl.pallas_call(
        _mod_mm_kernel,
        grid=(n // tn,),
        in_specs=[pl.BlockSpec((m, k), lambda j: (0, 0)),
                  pl.BlockSpec((None, k, tn), lambda j: (layer, 0, j))],
        out_specs=pl.BlockSpec((m, tn), lambda j: (0, j)),
        out_shape=jax.ShapeDtypeStruct((m, n), F32),
        compiler_params=_cparams(1),
        name="mod_matmul",
    )(cond, ada_w)


def _lat_spec(which):
    return pl.BlockSpec((None, None, 1, D_MODEL), lambda b, i, *_: (b, which, 0, 0))


def _ctx_spec(which):
    return pl.BlockSpec((None, 1, D_MODEL), lambda b, i, *_: (which, 0, 0))


def _region_select(tile, lat_ref, ctx_ref, row0=0, n_rows=None, has_ctx=True):
    if not has_ctx:
        return lat_ref[...]
    if T_LAT % tile == 0:
        return jnp.where(pl.program_id(1) >= T_LAT // tile, ctx_ref[...], lat_ref[...])
    n_rows = tile if n_rows is None else n_rows
    rows = pl.program_id(1) * tile + row0 + lax.broadcasted_iota(jnp.int32, (n_rows, 1), 0)
    return jnp.where(rows >= T_LAT, ctx_ref[...], lat_ref[...])


def _modnorm_kernel(x_ref, ctx_ref, gain_ref, scl_ref, scc_ref, shl_ref, shc_ref, xa_ref, o_ref,
                    *, tile):
    is_ctx = pl.program_id(1) >= T_LAT // tile
    x = jnp.where(is_ctx, ctx_ref[...], x_ref[...])
    xa_ref[...] = x
    sc = _region_select(tile, scl_ref, scc_ref)
    sh = _region_select(tile, shl_ref, shc_ref)
    o_ref[...] = (_rms(x) * gain_ref[...] * (1.0 + sc) + sh).astype(o_ref.dtype)


def _modnorm(x, ctx, gain, mod_lat, mod_ctx, sc_i, sh_i):
    tile = T_CTX
    n_lat = T_LAT // tile
    row = pl.BlockSpec((None, tile, D_MODEL), lambda b, i: (b, i, 0))
    return pl.pallas_call(
        functools.partial(_modnorm_kernel, tile=tile),
        grid=(BATCH, T_ALL // tile),
        in_specs=[pl.BlockSpec((None, tile, D_MODEL), lambda b, i: (b, jnp.minimum(i, n_lat - 1), 0)),
                  pl.BlockSpec((None, tile, D_MODEL), lambda b, i: (b, jnp.maximum(i - n_lat, 0), 0)),
                  pl.BlockSpec((1, D_MODEL), lambda b, i: (0, 0)),
                  _lat_spec(sc_i), _ctx_spec(sc_i), _lat_spec(sh_i), _ctx_spec(sh_i)],
        out_specs=[row, row],
        out_shape=[jax.ShapeDtypeStruct((BATCH, T_ALL, D_MODEL), F32),
                   jax.ShapeDtypeStruct((BATCH, T_ALL, D_MODEL), BF16)],
        compiler_params=_cparams(2),
        name="modnorm",
    )(x, ctx, gain, mod_lat, mod_ctx, mod_lat, mod_ctx)


def _rotary(z, cos, s1, s2):
    return z * cos + pltpu.roll(z, 96, 1) * s1 + pltpu.roll(z, 32, 1) * s2


def _ret_kernel(lg_ref, q_ref, g_ref, k_ref, v_ref, cos_ref, s1_ref, s2_ref, gain_ref,
                o_ref, acc_ref, qs_ref, ks_ref):
    c = RET_CHUNK
    nh = RET_HEADS_PER_STEP
    scale = HEAD_DIM ** -0.5
    dist = (lax.broadcasted_iota(jnp.int32, (c, c), 0)
            - lax.broadcasted_iota(jnp.int32, (c, c), 1)).astype(F32)
    tcol = lax.broadcasted_iota(jnp.int32, (c, 1), 0).astype(F32)

    def head_consts(hp):
        h = pl.program_id(1) * nh + hp
        la_f, la_b = lg_ref[0, h], lg_ref[1, h]
        dmat = (jnp.where(dist >= 0.0, jnp.exp(la_f * jnp.maximum(dist, 0.0)), 0.0)
                + jnp.where(dist <= 0.0, jnp.exp(la_b * jnp.maximum(-dist, 0.0)), 0.0))
        fwd_dec = (jnp.exp(la_f * (tcol + 1.0)), jnp.exp(la_f * (c - 1.0 - tcol)),
                   jnp.exp(jnp.full((1, LANE), la_f * c, F32)))
        bwd_dec = (jnp.exp(la_b * (c - tcol)), jnp.exp(la_b * tcol),
                   jnp.exp(jnp.full((1, LANE), la_b * c, F32)))
        return dmat, fwd_dec, bwd_dec

    consts = [head_consts(hp) for hp in range(nh)]

    def fwd(i, sts, base, rot):
        start = pl.multiple_of(base + i * c, c)
        rows = pl.ds(start, c)
        if rot:
            cos, s1, s2 = cos_ref[rows, :], s1_ref[rows, :], s2_ref[rows, :]
        out = []
        for hp in range(nh):
            hl = pl.ds(hp * LANE, LANE)
            dmat, (qdec, kdec, cdec), _ = consts[hp]
            q = q_ref[rows, hl].astype(F32) * scale
            k = k_ref[rows, hl].astype(F32)
            v = v_ref[rows, hl]
            if rot:
                q = _rotary(q, cos, s1, s2)
                k = _rotary(k, cos, s1, s2)
            qs_ref[rows, hl] = q
            ks_ref[rows, hl] = k
            att = _dot_nt(q.astype(BF16), k.astype(BF16)) * dmat
            st = sts[hp]
            acc_ref[rows, hl] = (_dot(att.astype(BF16), v)
                                 + _dot_nt((q * qdec).astype(BF16), st.astype(BF16)))
            out.append(cdec * st + _dot_tn(v, (k * kdec).astype(BF16)))
        return tuple(out)

    def bwd(i, sts, base, n):
        start = pl.multiple_of(base + (n - 1 - i) * c, c)
        rows = pl.ds(start, c)
        out = []
        for hp in range(nh):
            hl = pl.ds(hp * LANE, LANE)
            _, _, (qdec, kdec, cdec) = consts[hp]
            q = qs_ref[rows, hl]
            k = ks_ref[rows, hl]
            v = v_ref[rows, hl]
            st = sts[hp]
            o = acc_ref[rows, hl] + _dot_nt((q * qdec).astype(BF16), st.astype(BF16))
            gate = g_ref[rows, hl].astype(F32)
            o_ref[rows, hl] = (_rms(o) * gain_ref[:, hl] * _silu(gate)).astype(o_ref.dtype)
            out.append(cdec * st + _dot_tn(v, (k * kdec).astype(BF16)))
        return tuple(out)

    zero = tuple(jnp.zeros((HEAD_DIM, HEAD_DIM), F32) for _ in range(nh))
    n_ctx, n_lat = T_CTX // c, T_LAT // c
    sts = lax.fori_loop(0, n_ctx, functools.partial(fwd, base=T_LAT, rot=False), zero)
    lax.fori_loop(0, n_lat, functools.partial(fwd, base=0, rot=True), sts, unroll=RET_UNROLL)
    sts = lax.fori_loop(0, n_ctx, functools.partial(bwd, base=T_LAT, n=n_ctx), zero)
    lax.fori_loop(0, n_lat, functools.partial(bwd, base=0, n=n_lat), sts, unroll=RET_UNROLL)


def _retention(p, log_g, cos, s1, s2, ret_norm):
    nh = RET_HEADS_PER_STEP
    hw = nh * LANE
    def col(field):
        first = _block_index(EVEN_OFF[field], hw)
        return pl.BlockSpec((None, T_ALL, hw), lambda b, h, lg: (b, 0, first + h))

    tab = pl.BlockSpec((T_LAT, LANE), lambda b, h, lg: (0, 0))
    return pl.pallas_call(
        _ret_kernel,
        grid_spec=pltpu.PrefetchScalarGridSpec(
            num_scalar_prefetch=1,
            grid=(BATCH, RET_HEADS // nh),
            in_specs=[col("ret_q"), col("ret_gate"), col("ret_k"), col("ret_v"), tab, tab, tab,
                      pl.BlockSpec((1, hw), lambda b, h, lg: (0, h))],
            out_specs=pl.BlockSpec((None, T_ALL, hw), lambda b, h, lg: (b, 0, h)),
            scratch_shapes=[pltpu.VMEM((T_ALL, hw), F32), pltpu.VMEM((T_ALL, hw), F32),
                            pltpu.VMEM((T_ALL, hw), F32)]),
        out_shape=jax.ShapeDtypeStruct((BATCH, T_ALL, RET_DIM), BF16),
        compiler_params=_cparams(2),
        name="retention",
    )(log_g, p, p, p, p, cos, s1, s2, ret_norm)


def _gmlp_kernel(u_ref, v_ref, gain_ref, ws_ref, bst_ref, o_ref):
    u = jax.nn.gelu(u_ref[...].astype(F32))
    v = _rms(jax.nn.gelu(v_ref[...].astype(F32))) * gain_ref[...]
    vb = v.astype(BF16)
    for n in range(ROW_TILE // GMLP_CHUNK):
        r0 = n * GMLP_CHUNK
        for g in range(GMLP_GROUPS):
            c0 = g * LANE
            mixed = (_dot(ws_ref[g], vb[r0:r0 + GMLP_CHUNK, c0:c0 + LANE])
                     + bst_ref[:, g:g + 1])
            o_ref[r0:r0 + GMLP_CHUNK, c0:c0 + LANE] = (
                u[r0:r0 + GMLP_CHUNK, c0:c0 + LANE] * mixed).astype(o_ref.dtype)


def _gmlp(p, gmlp_norm, ws_bf16, bs_t):
    u_blk = _block_index(EVEN_OFF["gmlp_u"], GMLP_DIM)
    v_blk = _block_index(EVEN_OFF["gmlp_v"], GMLP_DIM)
    assert T_ALL % ROW_TILE == 0 and ROW_TILE % GMLP_CHUNK == 0
    return pl.pallas_call(
        _gmlp_kernel,
        grid=(BATCH, T_ALL // ROW_TILE),
        in_specs=[pl.BlockSpec((None, ROW_TILE, GMLP_DIM), lambda b, i: (b, i, u_blk)),
                  pl.BlockSpec((None, ROW_TILE, GMLP_DIM), lambda b, i: (b, i, v_blk)),
                  pl.BlockSpec((1, GMLP_DIM), lambda b, i: (0, 0)),
                  pl.BlockSpec((GMLP_GROUPS, GMLP_CHUNK, GMLP_CHUNK), lambda b, i: (0, 0, 0)),
                  pl.BlockSpec((GMLP_CHUNK, GMLP_GROUPS), lambda b, i: (0, 0))],
        out_specs=pl.BlockSpec((None, ROW_TILE, GMLP_DIM), lambda b, i: (b, i, 0)),
        out_shape=jax.ShapeDtypeStruct((BATCH, T_ALL, GMLP_DIM), BF16),
        compiler_params=_cparams(2),
        name="gmlp",
    )(p, p, gmlp_norm, ws_bf16, bs_t)


def _na_kernel(q_ref, k_ref, v_ref, bias_ref, o_ref):
    i = pl.program_id(2)
    w0 = jnp.clip(i * NA_QROWS - WIN_ROWS // 2, 0, GRID_ROWS - NA_KROWS)
    start = pl.multiple_of(w0 * GRID_W, GRID_W)
    scale = HEAD_DIM ** -0.5
    for hp in range(NA_HEADS_PER_STEP):
        hl = pl.ds(hp * LANE, LANE)
        q = q_ref[:, hl]
        s_win = _dot_nt(q, k_ref[pl.ds(start, NA_KT), hl]) * scale + bias_ref[hp]
        s_ctx = _dot_nt(q, k_ref[pl.ds(T_LAT, T_CTX), hl]) * scale
        m = jnp.maximum(jnp.max(s_win, axis=-1, keepdims=True),
                        jnp.max(s_ctx, axis=-1, keepdims=True))
        p_win = jnp.exp(s_win - m)
        p_ctx = jnp.exp(s_ctx - m)
        denom = jnp.sum(p_win, axis=-1, keepdims=True) + jnp.sum(p_ctx, axis=-1, keepdims=True)
        o = (_dot(p_win.astype(BF16), v_ref[pl.ds(start, NA_KT), hl])
             + _dot(p_ctx.astype(BF16), v_ref[pl.ds(T_LAT, T_CTX), hl]))
        o_ref[:, hl] = (o / denom).astype(o_ref.dtype)


def _na_geometry():
    n_blk = GRID_ROWS // NA_QROWS
    dr00 = []
    row_ok = np.zeros((3, NA_QROWS, NA_KROWS), bool)
    for kind, i_rep in enumerate((0, 1, n_blk - 1)):
        r0 = i_rep * NA_QROWS
        w0 = int(np.clip(r0 - WIN_ROWS // 2, 0, GRID_ROWS - NA_KROWS))
        r = r0 + np.arange(NA_QROWS)[:, None]
        kr = w0 + np.arange(NA_KROWS)[None, :]
        rstart = np.clip(r - WIN_ROWS // 2, 0, GRID_ROWS - WIN_ROWS)
        row_ok[kind] = (kr >= rstart) & (kr < rstart + WIN_ROWS)
        dr00.append(w0 - r0 + WIN_ROWS - 1)
    c = np.arange(GRID_W)[:, None]
    kc = np.arange(GRID_W)[None, :]
    cstart = np.clip(c - WIN_COLS // 2, 0, GRID_W - WIN_COLS)
    col_ok = (kc >= cstart) & (kc < cstart + WIN_COLS)
    valid = row_ok[:, :, None, :, None] & col_ok[None, None, :, None, :]
    return dr00, valid.reshape(3, NA_QT, NA_KT)


def _na_bias_table(rpb):
    n_dr, n_dc = 2 * WIN_ROWS - 1, 2 * WIN_COLS - 1
    w = jnp.concatenate([rpb[..., WIN_COLS - 1:], jnp.zeros((NA_HEADS, n_dr, LANE - n_dc), F32),
                         rpb[..., :WIN_COLS - 1]], axis=-1)
    toep = jnp.tile(w, (1, 1, GRID_W))[..., :GRID_W * (LANE - 1)]
    toep = toep.reshape(NA_HEADS, n_dr, GRID_W, LANE - 1)[..., :GRID_W]
    dr00, valid = _na_geometry()
    front = max(0, NA_QROWS - 1 - min(dr00))
    back = max(0, max(dr00) + NA_KROWS - n_dr)
    toep = jnp.pad(toep.transpose(0, 2, 1, 3), ((0, 0), (0, 0), (front, back), (0, 0)))
    toep = toep.astype(BF16).reshape(NA_HEADS, GRID_W, -1)
    kinds = []
    for d0 in dr00:
        starts = [(d0 - rl + front) * GRID_W for rl in range(NA_QROWS)]
        kinds.append(jnp.stack([toep[:, :, s:s + NA_KT] for s in starts], axis=1))
    tab = jnp.stack(kinds, axis=0).reshape(3, NA_HEADS, NA_QT, NA_KT)
    return jnp.where(jnp.asarray(valid)[:, None], tab, jnp.asarray(NEG_BIG, BF16))


def _na(p, bias):
    n_blk = T_LAT // NA_QT
    nh = NA_HEADS_PER_STEP
    hw = nh * LANE
    k0, v0 = _block_index(ODD_OFF["na_k"], hw), _block_index(ODD_OFF["na_v"], hw)
    kind = lambda i: jnp.where(i == 0, 0, jnp.where(i == n_blk - 1, 2, 1))
    return pl.pallas_call(
        _na_kernel,
        grid=(BATCH, NA_HEADS // nh, n_blk),
        in_specs=[pl.BlockSpec((None, NA_QT, hw), lambda b, h, i: (b, i, h)),
                  pl.BlockSpec((None, T_ALL, hw), lambda b, h, i: (b, 0, k0 + h)),
                  pl.BlockSpec((None, T_ALL, hw), lambda b, h, i: (b, 0, v0 + h)),
                  pl.BlockSpec((None, nh, NA_QT, NA_KT), lambda b, h, i: (kind(i), h, 0, 0))],
        out_specs=pl.BlockSpec((None, NA_QT, hw), lambda b, h, i: (b, i, h)),
        out_shape=jax.ShapeDtypeStruct((BATCH, T_LAT, NA_DIM), BF16),
        compiler_params=_cparams(3),
        name="nbr_attention",
    )(p, p, p, bias)


def _gla_kernel(q_ref, k_ref, v_ref, g_ref, lr_ref, wcat_ref, bup_ref, gain_ref,
                o_ref, acc_ref):
    r, c = GLA_SUPER, GLA_SUB
    n_sub = r // c
    dv2 = 2 * GLA_DV
    qscale = GLA_DK ** -0.5
    row = lax.broadcasted_iota(jnp.int32, (r, r), 0)
    col = lax.broadcasted_iota(jnp.int32, (r, r), 1)
    sub_shift = c.bit_length() - 1
    same = (row >> sub_shift) == (col >> sub_shift)
    blk_ones = jnp.where(same, 1.0, 0.0).astype(BF16)
    second_head_lane = lax.broadcasted_iota(jnp.int32, (1, LANE), 1) >= GLA_DK
    st_mask = ((lax.broadcasted_iota(jnp.int32, (dv2, LANE), 0) >= GLA_DV)
               == (lax.broadcasted_iota(jnp.int32, (dv2, LANE), 1) >= GLA_DK))

    for d in range(2):
        tri = same & ((row >= col) if d == 0 else (col >= row))
        tri_b = jnp.where(tri, 1.0, 0.0).astype(BF16)

        def pair_step(hp, start, lr, st, d=d, tri=tri, tri_b=tri_b):
            kl = pl.ds(hp * LANE, LANE)
            vl = pl.ds(hp * dv2, dv2)
            zz = _dot(lr, wcat_ref[d, :, pl.ds(hp * 2 * LANE, 2 * LANE)])
            z = zz[:, :LANE] + zz[:, LANE:] + bup_ref[d, :, kl]
            la = (jnp.minimum(z, 0.0) - jnp.log(1.0 + jnp.exp(-jnp.abs(z)))) * (1.0 / GLA_TAU)
            hl = jnp.concatenate(_split2(la), axis=1)
            bb = _dot(tri_b, hl)
            b = bb[:, :LANE] + bb[:, LANE:]
            bb = _dot(blk_ones, hl)
            bt = bb[:, :LANE] + bb[:, LANE:]
            q = q_ref[pl.ds(start, r), kl].astype(F32) * qscale
            k = k_ref[pl.ds(start, r), kl].astype(F32)
            v = v_ref[pl.ds(start, r), vl]
            qi = q * jnp.exp(b)
            qib = qi.astype(BF16)
            ki = (k * jnp.exp(-b)).astype(BF16)
            ko = (k * jnp.exp(bt - b)).astype(BF16)
            dec = jnp.exp(bt)
            outs = []
            for j in range(2):
                qm = jnp.where(second_head_lane == (j == 1), qi, 0.0).astype(BF16)
                att = jnp.where(tri, _dot_nt(qm, ki), 0.0)
                outs.append(_dot(att.astype(BF16), v[:, j * GLA_DV:(j + 1) * GLA_DV]))
            o_intra = jnp.concatenate(outs, axis=1)
            parts = [None] * n_sub
            for s in (range(n_sub) if d == 0 else reversed(range(n_sub))):
                lo_r, hi_r = s * c, (s + 1) * c
                parts[s] = o_intra[lo_r:hi_r] + _dot_nt(qib[lo_r:hi_r], st.astype(BF16))
                st = (st * dec[lo_r:lo_r + 1, :]
                      + jnp.where(st_mask, _dot_tn(v[lo_r:hi_r], ko[lo_r:hi_r]), 0.0))
            o = jnp.concatenate(parts, axis=0)
            if d == 0:
                acc_ref[pl.ds(start, r), vl] = o
            else:
                o = acc_ref[pl.ds(start, r), vl] + o
                gate = g_ref[pl.ds(start, r), vl].astype(F32)
                on = (jnp.concatenate([_rms(o[:, :GLA_DV]), _rms(o[:, GLA_DV:])], axis=1)
                      * gain_ref[:, vl])
                o_ref[pl.ds(start, r), vl] = (on * _silu(gate)).astype(o_ref.dtype)
            return st

        def step(i, sts, base, n, d=d):
            idx = i if d == 0 else n - 1 - i
            start = pl.multiple_of(base + idx * r, r)
            lr = lr_ref[pl.ds(start, r), :]
            return tuple(pair_step(hp, start, lr, sts[hp]) for hp in range(GLA_PAIRS))

        sts = tuple(jnp.zeros((dv2, LANE), F32) for _ in range(GLA_PAIRS))
        sts = lax.fori_loop(0, T_CTX // r, functools.partial(step, base=T_LAT, n=T_CTX // r), sts)
        lax.fori_loop(0, T_LAT // r, functools.partial(step, base=0, n=T_LAT // r), sts,
                      unroll=GLA_UNROLL)


def _gla(p, lr, w_cat, b_up, gla_norm):
    kw = GLA_PAIRS * LANE
    vw = GLA_PAIRS * 2 * GLA_DV
    q0, k0 = _block_index(ODD_OFF["gla_q"], kw), _block_index(ODD_OFF["gla_k"], kw)
    v0, g0 = _block_index(ODD_OFF["gla_v"], vw), _block_index(ODD_OFF["gla_gate"], vw)
    return pl.pallas_call(
        _gla_kernel,
        grid=(BATCH, GLA_HEADS // (2 * GLA_PAIRS)),
        in_specs=[pl.BlockSpec((None, T_ALL, kw), lambda b, j: (b, 0, q0 + j)),
                  pl.BlockSpec((None, T_ALL, kw), lambda b, j: (b, 0, k0 + j)),
                  pl.BlockSpec((None, T_ALL, vw), lambda b, j: (b, 0, v0 + j)),
                  pl.BlockSpec((None, T_ALL, vw), lambda b, j: (b, 0, g0 + j)),
                  pl.BlockSpec((None, T_ALL, LANE), lambda b, j: (b, 0, 0)),
                  pl.BlockSpec((2, LANE, 2 * kw), lambda b, j: (0, 0, j)),
                  pl.BlockSpec((2, 1, kw), lambda b, j: (0, 0, j)),
                  pl.BlockSpec((1, vw), lambda b, j: (0, j))],
        out_specs=pl.BlockSpec((None, T_ALL, vw), lambda b, j: (b, 0, j)),
        out_shape=jax.ShapeDtypeStruct((BATCH, T_ALL, GLA_V_DIM), BF16),
        scratch_shapes=[pltpu.VMEM((T_ALL, vw), F32)],
        compiler_params=_cparams(2),
        name="gla",
    )(p, p, p, p, lr, w_cat, b_up, gla_norm)


def _out_kernel(a1_ref, a2_ref, w1_ref, w2_ref, x_ref, gl_ref, gc_ref, gain_ref,
                scl_ref, scc_ref, shl_ref, shc_ref, rw_ref, xo_ref, h_ref, lg_ref, *, tile,
                has_ctx):
    half = tile // 2
    for r0 in (0, half):
        rows = pl.ds(r0, half)
        sel = functools.partial(_region_select, tile, row0=r0, n_rows=half, has_ctx=has_ctx)
        y = _dot(a1_ref[rows, :], w1_ref[...]) + _dot(a2_ref[rows, :], w2_ref[...])
        xn = x_ref[rows, :] + sel(gl_ref, gc_ref) * y
        xo_ref[rows, :] = xn
        h = _rms(xn) * gain_ref[...] * (1.0 + sel(scl_ref, scc_ref)) + sel(shl_ref, shc_ref)
        h_hi = h.astype(BF16)
        h_ref[rows, :] = h_hi
        h_lo = (h - h_hi.astype(F32)).astype(BF16)
        lg_ref[rows, :] = _dot(h_hi, rw_ref[...]) + _dot(h_lo, rw_ref[...])


def _out_proj(a1, a2, w_out_bf16, x, mod_lat, mod_ctx, norm_gain, rw_split, tile, n_tiles):
    half = w_out_bf16.shape[0] // 2
    t_out = n_tiles * tile
    assert t_out in (T_LAT, T_ALL) and tile % 32 == 0
    row = lambda width: pl.BlockSpec((None, tile, width), lambda b, i: (b, i, 0))
    return pl.pallas_call(
        functools.partial(_out_kernel, tile=tile, has_ctx=t_out > T_LAT),
        grid=(BATCH, n_tiles),
        in_specs=[row(half), row(half),
                  pl.BlockSpec((half, D_MODEL), lambda b, i: (0, 0)),
                  pl.BlockSpec((half, D_MODEL), lambda b, i: (1, 0)),
                  row(D_MODEL), _lat_spec(2), _ctx_spec(2),
                  pl.BlockSpec((1, D_MODEL), lambda b, i: (0, 0)),
                  _lat_spec(4), _ctx_spec(4), _lat_spec(3), _ctx_spec(3),
                  pl.BlockSpec((D_MODEL, 2 * N_EXPERTS), lambda b, i: (0, 0))],
        out_specs=[row(D_MODEL), row(D_MODEL), row(2 * N_EXPERTS)],
        out_shape=[jax.ShapeDtypeStruct((BATCH, t_out, D_MODEL), F32),
                   jax.ShapeDtypeStruct((BATCH, t_out, D_MODEL), BF16),
                   jax.ShapeDtypeStruct((BATCH, t_out, 2 * N_EXPERTS), F32)],
        compiler_params=_cparams(2),
        name="out_proj",
    )(a1, a2, w_out_bf16, w_out_bf16, x, mod_lat, mod_ctx, norm_gain,
      mod_lat, mod_ctx, mod_lat, mod_ctx, rw_split)


def _moe_up_kernel(x_ref, wg_ref, wu_ref, o_ref):
    x = x_ref[...]
    a = _dot(x, wg_ref[...].astype(BF16))
    u = _dot(x, wu_ref[...].astype(BF16))
    o_ref[...] = (_silu(a) * u).astype(o_ref.dtype)


def _moe_down_kernel(h_ref, wd_ref, g_ref, o_ref):
    gcol = _dot_tn(g_ref[...], jnp.ones((GATE_ROWS, LANE), BF16))
    y = _dot(h_ref[...], wd_ref[...].astype(BF16))
    for c in range(o_ref.shape[1] // LANE):
        cols = pl.ds(c * LANE, LANE)
        o_ref[:, cols] = (y[:, c * LANE:(c + 1) * LANE] * gcol).astype(o_ref.dtype)


def _moe(xs, gates, w_gate, w_up, w_down, layer):
    e, m, _ = xs.shape
    tn = MOE_TN
    wspec = pl.BlockSpec((None, None, D_MODEL, tn), lambda e, f: (layer, e, 0, f))
    hmid = pl.pallas_call(
        _moe_up_kernel,
        grid=(e, D_EXPERT // tn),
        in_specs=[pl.BlockSpec((None, m, D_MODEL), lambda e, f: (e, 0, 0)), wspec, wspec],
        out_specs=pl.BlockSpec((None, m, tn), lambda e, f: (e, 0, f)),
        out_shape=jax.ShapeDtypeStruct((e, m, D_EXPERT), BF16),
        compiler_params=_cparams(2),
        name="moe_up",
    )(xs, w_gate, w_up)
    tn = MOE_DOWN_TN
    return pl.pallas_call(
        _moe_down_kernel,
        grid=(e, D_MODEL // tn),
        in_specs=[pl.BlockSpec((None, m, D_EXPERT), lambda e, f: (e, 0, 0)),
                  pl.BlockSpec((None, None, D_EXPERT, tn), lambda e, f: (layer, e, 0, f)),
                  pl.BlockSpec((None, GATE_ROWS, m), lambda e, f: (e, 0, 0))],
        out_specs=pl.BlockSpec((None, m, tn), lambda e, f: (e, 0, f)),
        out_shape=jax.ShapeDtypeStruct((e, m, D_MODEL), BF16),
        compiler_params=_cparams(2),
        name="moe_down",
    )(hmid, w_down, gates)


def _combine_moe(span_ref, pair_tok_ref, ys_hbm, acc_ref, buf_ref, sem, *, tile, n_pairs):
    b, i = pl.program_id(0), pl.program_id(1)
    n_tiles = pl.num_programs(1)
    step = b * n_tiles + i
    n_steps = pl.num_programs(0) * n_tiles
    w = PAIR_WIN
    shift = w.bit_length() - 1
    last_start = n_pairs // w - PAIR_CHUNK

    def first_window(s):
        return jnp.minimum(span_ref[2 * s] >> shift, last_start)

    def chunk_copy(sample, win, slot):
        return pltpu.make_async_copy(
            ys_hbm.at[pl.ds(sample * n_pairs + win * w, PAIR_CHUNK * w), :],
            buf_ref.at[slot], sem.at[slot])

    def window_copy(win, slot):
        return pltpu.make_async_copy(ys_hbm.at[pl.ds(b * n_pairs + win * w, w), :],
                                     buf_ref.at[slot, pl.ds(0, w), :], sem.at[slot])

    slot = step & 1
    w0 = first_window(step)

    @pl.when(step == 0)
    def _():
        chunk_copy(b, w0, slot).start()

    @pl.when(step + 1 < n_steps)
    def _():
        next_sample = jnp.where(i + 1 < n_tiles, b, b + 1)
        chunk_copy(next_sample, first_window(step + 1), 1 - slot).start()

    chunk_copy(b, w0, slot).wait()
    tok = i * tile + lax.broadcasted_iota(jnp.int32, (tile, 1), 0)

    def select(win):
        return jnp.where(tok == pair_tok_ref[pl.ds(win, 1), :], 1.0, 0.0).astype(BF16)

    sel = jnp.concatenate([select(w0 + c) for c in range(PAIR_CHUNK)], axis=1)
    acc_ref[...] = _dot(sel, buf_ref[slot])

    hi = span_ref[2 * step + 1]
    end_win = jnp.where(hi > 0, ((hi - 1) >> shift) + 1, 0)

    def body(win, carry):
        cp = window_copy(win, slot)
        cp.start()
        cp.wait()
        acc_ref[...] += _dot(select(win), buf_ref[slot, pl.ds(0, w), :])
        return carry

    lax.fori_loop(w0 + PAIR_CHUNK, end_win, body, 0)
    return acc_ref[...]


def _res_mid_kernel(span_ref, x_ref, pair_tok_ref, ys_hbm, gl_ref, gc_ref, gain_ref,
                    scl_ref, scc_ref, shl_ref, shc_ref, xo_ref, h_ref, acc_ref, buf_ref, sem,
                    *, tile, n_pairs):
    moe = _combine_moe(span_ref, pair_tok_ref, ys_hbm, acc_ref, buf_ref, sem,
                       tile=tile, n_pairs=n_pairs)
    xn = x_ref[...] + _region_select(tile, gl_ref, gc_ref) * moe
    xo_ref[...] = xn
    h_ref[...] = (_rms(xn) * gain_ref[...] * (1.0 + _region_select(tile, scl_ref, scc_ref))
                  + _region_select(tile, shl_ref, shc_ref)).astype(h_ref.dtype)


def _res_final_kernel(span_ref, x_ref, pair_tok_ref, ys_hbm, g_ref, gain_ref, o_ref,
                      acc_ref, buf_ref, sem, *, tile, n_pairs):
    moe = _combine_moe(span_ref, pair_tok_ref, ys_hbm, acc_ref, buf_ref, sem,
                       tile=tile, n_pairs=n_pairs)
    o_ref[...] = _rms(x_ref[...] + g_ref[...] * moe) * gain_ref[...]


def _combine_call(kernel_fn, name, tile, t_rows, spans, x, pair_tok, ys, params, param_specs,
                  out_dtypes):
    n_pairs = pair_tok.shape[1] * PAIR_WIN
    assert t_rows % tile == 0 and n_pairs // PAIR_WIN >= PAIR_CHUNK
    row = pl.BlockSpec((None, tile, D_MODEL), lambda b, i, *_: (b, i, 0))
    return pl.pallas_call(
        functools.partial(kernel_fn, tile=tile, n_pairs=n_pairs),
        grid_spec=pltpu.PrefetchScalarGridSpec(
            num_scalar_prefetch=1,
            grid=(BATCH, t_rows // tile),
            in_specs=[row,
                      pl.BlockSpec((None,) + pair_tok.shape[1:], lambda b, i, *_: (b, 0, 0)),
                      pl.BlockSpec(memory_space=pl.ANY)] + param_specs,
            out_specs=[row] * len(out_dtypes),
            scratch_shapes=[pltpu.VMEM((tile, D_MODEL), F32),
                            pltpu.VMEM((2, PAIR_CHUNK * PAIR_WIN, D_MODEL), BF16),
                            pltpu.SemaphoreType.DMA((2,))]),
        out_shape=[jax.ShapeDtypeStruct((BATCH, t_rows, D_MODEL), dt) for dt in out_dtypes],
        compiler_params=_cparams(2),
        name=name,
    )(spans, x, pair_tok, ys, *params)


def _res_mid(x, spans, pair_tok, ys, mod_lat, mod_ctx, next_lat, next_ctx, norm_gain_next):
    specs = [_lat_spec(5), _ctx_spec(5), pl.BlockSpec((1, D_MODEL), lambda b, i, *_: (0, 0)),
             _lat_spec(1), _ctx_spec(1), _lat_spec(0), _ctx_spec(0)]
    params = (mod_lat, mod_ctx, norm_gain_next, next_lat, next_ctx, next_lat, next_ctx)
    return _combine_call(_res_mid_kernel, "ffn_combine_residual_prenorm", COMB_TILE, T_ALL, spans,
                         x, pair_tok, ys, params, specs, (F32, BF16))


def _res_final(x, spans, pair_tok, ys, mod_lat, norm_final):
    specs = [_lat_spec(5), pl.BlockSpec((1, D_MODEL), lambda b, i, *_: (0, 0))]
    return _combine_call(_res_final_kernel, "ffn_combine_residual_final_norm", COMB_TILE, T_LAT,
                         spans, x, pair_tok, ys, (mod_lat, norm_final), specs, (F32,))[0]


def _rope_tables():
    half = HEAD_DIM // 2
    nf = half // 2
    t = np.arange(T_LAT)
    rows, cols = t // GRID_W, t % GRID_W
    freq = (np.float32(ROPE_BASE) ** (-np.arange(nf, dtype=np.float32) / np.float32(nf))).astype(np.float32)
    ang_r = rows.astype(np.float32)[:, None] * freq
    ang_c = cols.astype(np.float32)[:, None] * freq
    zeros = np.zeros_like(ang_r)
    cos = np.concatenate([np.cos(ang_r), np.cos(ang_r), np.cos(ang_c), np.cos(ang_c)], axis=1)
    s1 = np.concatenate([-np.sin(ang_r), zeros, -np.sin(ang_c), zeros], axis=1)
    s2 = np.concatenate([zeros, np.sin(ang_r), zeros, np.sin(ang_c)], axis=1)
    return tuple(jnp.asarray(a, F32) for a in (cos, s1, s2))


def _router_split(rw):
    hi = rw.astype(BF16)
    lo = (rw - hi.astype(F32)).astype(BF16)
    return jnp.concatenate([hi, lo], axis=1)


def _route(logits, t0, t1, cap):
    aff = jax.nn.softmax(logits[:, t0:t1], axis=-1)
    g, idx = lax.top_k(aff.transpose(0, 2, 1), cap)
    return g, idx + t0


def _expert_ffn(h, logits2, w_gate, w_up, w_down, layer, with_ctx, tile):
    logits = logits2[..., :N_EXPERTS] + logits2[..., N_EXPERTS:]
    g, idx = _route(logits, 0, T_LAT, EC_CAPACITY * T_LAT // N_EXPERTS)
    if with_ctx:
        gc, idxc = _route(logits, T_LAT, T_ALL, EC_CAPACITY * T_CTX // N_EXPERTS)
        g = jnp.concatenate([g, gc], axis=-1)
        idx = jnp.concatenate([idx, idxc], axis=-1)
    cap = idx.shape[-1]
    t_rows = h.shape[1]
    n_pairs = N_EXPERTS * cap
    assert n_pairs % PAIR_WIN == 0
    flat = (idx + jnp.arange(BATCH)[:, None, None] * t_rows).transpose(1, 0, 2).reshape(N_EXPERTS, -1)
    g_rows = g.transpose(1, 0, 2).reshape(N_EXPERTS, 1, BATCH * cap)
    g_hi = g_rows.astype(BF16)
    g_mid = (g_rows - g_hi.astype(F32)).astype(BF16)
    g_lo = (g_rows - g_hi.astype(F32) - g_mid.astype(F32)).astype(BF16)
    gates = jnp.concatenate([g_hi, g_mid, g_lo,
                             jnp.zeros((N_EXPERTS, GATE_ROWS - 3, BATCH * cap), BF16)], axis=1)
    xs = h.reshape(BATCH * t_rows, D_MODEL)[flat]
    y = _moe(xs, gates, w_gate, w_up, w_down, layer)
    src = (jnp.arange(N_EXPERTS)[None, :, None] * (BATCH * cap)
           + jnp.arange(BATCH)[:, None, None] * cap + jnp.arange(cap)[None, None, :])
    cap_lat = EC_CAPACITY * T_LAT // N_EXPERTS
    parts = []
    for sl in ((slice(0, cap_lat), slice(cap_lat, cap)) if with_ctx else (slice(0, cap),)):
        parts.append(lax.sort((idx[..., sl].reshape(BATCH, -1), src[..., sl].reshape(BATCH, -1)),
                              dimension=1, num_keys=1))
    pair_tok = jnp.concatenate([p[0] for p in parts], axis=1)
    src = jnp.concatenate([p[1] for p in parts], axis=1)
    ys = y.reshape(-1, D_MODEL)[src.reshape(-1)]
    bounds = jnp.arange(0, t_rows + tile, tile)
    below = jnp.sum(pair_tok[:, :, None] < bounds[None, None, :], axis=1)
    spans = jnp.stack([below[:, :-1], below[:, 1:]], axis=-1).reshape(-1).astype(jnp.int32)
    return spans, pair_tok.reshape(BATCH, n_pairs // PAIR_WIN, PAIR_WIN), ys


def kernel(x, c, ctx, c_ctx, ada_w, ada_b, norm_mix, norm_ffn, norm_final, ev_w_in, ev_w_out,
           ret_gamma_logit, ret_norm, gmlp_norm, gmlp_ws, gmlp_bs, od_w_in, od_w_out, na_rpb,
           gla_w_up, gla_b_up, gla_norm, router_w, moe_w_gate, moe_w_up, moe_w_down):
    assert x.shape == (BATCH, T_LAT, D_MODEL) and ctx.shape == (BATCH, T_CTX, D_MODEL)
    assert ada_w.shape[0] == 2 and ev_w_in.shape[0] == 1 and od_w_in.shape[0] == 1

    cond = jnp.concatenate([jax.nn.silu(c), jax.nn.silu(c_ctx)[None],
                            jnp.zeros((11, D_MODEL), F32)], axis=0).astype(BF16)
    mod_lat, mod_ctx = [], []
    for l in range(2):
        m = _mod_matmul(cond, ada_w, l) + ada_b[l]
        m = m[:BATCH + 1].reshape(BATCH + 1, 6, 1, D_MODEL)
        mod_lat.append(m[:BATCH])
        mod_ctx.append(m[BATCH])

    cos, s1, s2 = _rope_tables()

    xa, h = _modnorm(x, ctx, norm_mix[0:1], mod_lat[0], mod_ctx[0], 1, 0)
    p = _proj_matmul(h.reshape(BATCH * T_ALL, D_MODEL), ev_w_in, 0, EVEN_COLS, 2176, 1024)
    p = p.reshape(BATCH, T_ALL, EVEN_COLS)
    log_g = jax.nn.log_sigmoid(ret_gamma_logit[0].astype(F32))
    ret = _retention(p, log_g, cos, s1, s2, ret_norm[0:1])
    gm = _gmlp(p, gmlp_norm[0:1], gmlp_ws[0].astype(BF16), gmlp_bs[0].T)
    xa, h2, logits = _out_proj(ret, gm, ev_w_out[0].astype(BF16), xa, mod_lat[0], mod_ctx[0],
                               norm_ffn[0:1], _router_split(router_w[0]), MIX_TILE,
                               T_ALL // MIX_TILE)
    spans, pair_tok, ys = _expert_ffn(h2, logits, moe_w_gate, moe_w_up, moe_w_down, 0, True,
                                      COMB_TILE)
    xa, h = _res_mid(xa, spans, pair_tok, ys, mod_lat[0], mod_ctx[0], mod_lat[1], mod_ctx[1],
                     norm_mix[1:2])

    h2d = h.reshape(BATCH * T_ALL, D_MODEL)
    w_in_t = od_w_in[0].T
    p = _proj_matmul_t(h2d, w_in_t, ODD_MAIN_COLS, 2176, 1024)
    p = p.reshape(BATCH, T_ALL, ODD_MAIN_COLS)
    w_lr_t = jnp.pad(w_in_t[ODD_MAIN_COLS:], ((0, LANE - 2 * GLA_GATE_RANK), (0, 0))).astype(BF16)
    lr = _proj_matmul_t(h2d, w_lr_t, LANE, 2176, LANE).reshape(BATCH, T_ALL, LANE)
    o_na = _na(p, _na_bias_table(na_rpb[0]))
    w_up_full = jnp.zeros((2, LANE, GLA_QK_DIM), F32)
    for d in range(2):
        w_up_full = w_up_full.at[d, d * GLA_GATE_RANK:(d + 1) * GLA_GATE_RANK].set(gla_w_up[0, d])
    w_hi = w_up_full.astype(BF16)
    w_lo = (w_up_full - w_hi.astype(F32)).astype(BF16)
    n_pair = GLA_QK_DIM // LANE
    w_cat = jnp.stack([w_hi.reshape(2, LANE, n_pair, LANE), w_lo.reshape(2, LANE, n_pair, LANE)],
                      axis=3).reshape(2, LANE, 2 * GLA_QK_DIM)
    gla = _gla(p, lr, w_cat, gla_b_up[0][:, None, :], gla_norm[0:1])
    xl, h2, logits = _out_proj(o_na, gla, od_w_out[0].astype(BF16), xa, mod_lat[1], mod_ctx[1],
                               norm_ffn[1:2], _router_split(router_w[1]), LAT_TILE,
                               T_LAT // LAT_TILE)
    spans, pair_tok, ys = _expert_ffn(h2, logits, moe_w_gate, moe_w_up, moe_w_down, 1, False,
                                      COMB_TILE)
    return _res_final(xl, spans, pair_tok, ys, mod_lat[1], norm_final[None])
```

```python
import functools

import numpy as np
import jax
import jax.numpy as jnp
from jax import lax
from jax.experimental import pallas as pl
from jax.experimental.pallas import tpu as pltpu

D_MODEL = 2048
BATCH = 4
T_LAT = 4096
T_CTX = 256
T_ALL = T_LAT + T_CTX
GRID_W = 64
GRID_ROWS = T_LAT // GRID_W
HEAD_DIM = 128
RET_HEADS = 8
RET_DIM = 1024
GMLP_DIM = 1024
GMLP_GROUPS = 8
GMLP_CHUNK = 128
NA_HEADS = 8
NA_DIM = 1024
WIN_ROWS = 8
WIN_COLS = 16
GLA_HEADS = 8
GLA_DK = 64
GLA_DV = 128
GLA_QK_DIM = 512
GLA_V_DIM = 1024
GLA_GATE_RANK = 16
GLA_TAU = 16.0
N_EXPERTS = 16
D_EXPERT = 2048
EC_CAPACITY = 2
ROPE_BASE = 10000.0
RMS_EPS = 1e-6
EVEN_COLS = 6144
ODD_COLS = 6176
ODD_MAIN_COLS = 6144
LANE = 128

ROW_TILE = 256
MIX_TILE = 544
LAT_TILE = 512
RET_CHUNK = 256
GLA_SUB = 64
GLA_SUPER = 256
GLA_PAIRS = 2
NA_HEADS_PER_STEP = 4
RET_HEADS_PER_STEP = 2
RET_UNROLL = 4
GLA_UNROLL = 4
NA_QROWS = 4
NA_QT = NA_QROWS * GRID_W
NA_KROWS = NA_QROWS + WIN_ROWS - 1
NA_KT = NA_KROWS * GRID_W
NEG_BIG = -1e30
MOE_TN = 512
MOE_DOWN_TN = 1024
GATE_ROWS = 16
PAIR_WIN = 256
PAIR_CHUNK = 3
COMB_TILE = 256
VMEM_LIMIT = 56 * 1024 * 1024

F32 = jnp.float32
BF16 = jnp.bfloat16


def _offsets(names, widths):
    out, col = {}, 0
    for name, width in zip(names, widths):
        out[name] = col
        col += width
    return out


EVEN_OFF = _offsets(("ret_q", "ret_gate", "gmlp_u", "gmlp_v", "ret_k", "ret_v"),
                    (RET_DIM, RET_DIM, GMLP_DIM, GMLP_DIM, RET_DIM, RET_DIM))
ODD_OFF = _offsets(("na_q", "gla_q", "gla_gate", "na_k", "na_v", "gla_k", "gla_v"),
                   (NA_DIM, GLA_QK_DIM, GLA_V_DIM, NA_DIM, NA_DIM, GLA_QK_DIM, GLA_V_DIM))


def _block_index(col, width):
    index, rem = divmod(col, width)
    assert rem == 0, (col, width)
    return index


def _cparams(n_axes):
    return pltpu.CompilerParams(dimension_semantics=("arbitrary",) * n_axes,
                                vmem_limit_bytes=VMEM_LIMIT)


def _dot(a, b):
    return jnp.dot(a, b, preferred_element_type=F32)


def _dot_nt(a, b):
    return lax.dot_general(a, b, (((1,), (1,)), ((), ())), preferred_element_type=F32)


def _dot_tn(a, b):
    return lax.dot_general(a, b, (((0,), (0,)), ((), ())), preferred_element_type=F32)


def _silu(x):
    return x * jax.nn.sigmoid(x)


def _rms(x):
    return x * lax.rsqrt(jnp.mean(x * x, axis=-1, keepdims=True) + RMS_EPS)


def _split2(x):
    hi = x.astype(BF16)
    lo = (x - hi.astype(F32)).astype(BF16)
    return hi, lo


def _proj_kernel(a_ref, w_ref, o_ref):
    o_ref[...] = _dot(a_ref[...], w_ref[...].astype(BF16)).astype(o_ref.dtype)


def _proj_matmul(a, w, layer, n_cols, tm, tn):
    m, k = a.shape
    return pl.pallas_call(
        _proj_kernel,
        grid=(n_cols // tn, m // tm),
        in_specs=[pl.BlockSpec((tm, k), lambda j, i: (i, 0)),
                  pl.BlockSpec((None, k, tn), lambda j, i: (layer, 0, j))],
        out_specs=pl.BlockSpec((tm, tn), lambda j, i: (i, j)),
        out_shape=jax.ShapeDtypeStruct((m, n_cols), BF16),
        compiler_params=_cparams(2),
        name="proj_matmul",
    )(a, w)


def _proj_t_kernel(a_ref, wt_ref, o_ref):
    o_ref[...] = _dot_nt(a_ref[...], wt_ref[...].astype(BF16)).astype(o_ref.dtype)


def _proj_matmul_t(a, wt, n_cols, tm, tn):
    m, k = a.shape
    return pl.pallas_call(
        _proj_t_kernel,
        grid=(n_cols // tn, m // tm),
        in_specs=[pl.BlockSpec((tm, k), lambda j, i: (i, 0)),
                  pl.BlockSpec((tn, k), lambda j, i: (j, 0))],
        out_specs=pl.BlockSpec((tm, tn), lambda j, i: (i, j)),
        out_shape=jax.ShapeDtypeStruct((m, n_cols), BF16),
        compiler_params=_cparams(2),
        name="proj_matmul_t",
    )(a, wt)


def _mod_mm_kernel(a_ref, w_ref, o_ref):
    o_ref[...] = _dot(a_ref[...], w_ref[...].astype(BF16))


def _mod_matmul(cond, ada_w, layer):
    m, k = cond.shape
    n = ada_w.shape[2]
    tn = 1024
    return pl.pallas_call(
        _mod_mm_kernel,
        grid=(n // tn,),
        in_specs=[pl.BlockSpec((m, k), lambda j: (0, 0)),
                  pl.BlockSpec((None, k, tn), lambda j: (layer, 0, j))],
        out_specs=pl.BlockSpec((m, tn), lambda j: (0, j)),
        out_shape=jax.ShapeDtypeStruct((m, n), F32),
        compiler_params=_cparams(1),
        name="mod_matmul",
    )(cond, ada_w)


def _lat_spec(which):
    return pl.BlockSpec((None, None, 1, D_MODEL), lambda b, i, *_: (b, which, 0, 0))


def _ctx_spec(which):
    return pl.BlockSpec((None, 1, D_MODEL), lambda b, i, *_: (which, 0, 0))


def _region_select(tile, lat_ref, ctx_ref, row0=0, n_rows=None, has_ctx=True):
    if not has_ctx:
        return lat_ref[...]
    if T_LAT % tile == 0:
        return jnp.where(pl.program_id(1) >= T_LAT // tile, ctx_ref[...], lat_ref[...])
    n_rows = tile if n_rows is None else n_rows
    rows = pl.program_id(1) * tile + row0 + lax.broadcasted_iota(jnp.int32, (n_rows, 1), 0)
    return jnp.where(rows >= T_LAT, ctx_ref[...], lat_ref[...])


def _modnorm_kernel(x_ref, ctx_ref, gain_ref, scl_ref, scc_ref, shl_ref, shc_ref, xa_ref, o_ref,
                    *, tile):
    is_ctx = pl.program_id(1) >= T_LAT // tile
    x = jnp.where(is_ctx, ctx_ref[...], x_ref[...])
    xa_ref[...] = x
    sc = _region_select(tile, scl_ref, scc_ref)
    sh = _region_select(tile, shl_ref, shc_ref)
    o_ref[...] = (_rms(x) * gain_ref[...] * (1.0 + sc) + sh).astype(o_ref.dtype)


def _modnorm(x, ctx, gain, mod_lat, mod_ctx, sc_i, sh_i):
    tile = T_CTX
    n_lat = T_LAT // tile
    row = pl.BlockSpec((None, tile, D_MODEL), lambda b, i: (b, i, 0))
    return pl.pallas_call(
        functools.partial(_modnorm_kernel, tile=tile),
        grid=(BATCH, T_ALL // tile),
        in_specs=[pl.BlockSpec((None, tile, D_MODEL), lambda b, i: (b, jnp.minimum(i, n_lat - 1), 0)),
                  pl.BlockSpec((None, tile, D_MODEL), lambda b, i: (b, jnp.maximum(i - n_lat, 0), 0)),
                  pl.BlockSpec((1, D_MODEL), lambda b, i: (0, 0)),
                  _lat_spec(sc_i), _ctx_spec(sc_i), _lat_spec(sh_i), _ctx_spec(sh_i)],
        out_specs=[row, row],
        out_shape=[jax.ShapeDtypeStruct((BATCH, T_ALL, D_MODEL), F32),
                   jax.ShapeDtypeStruct((BATCH, T_ALL, D_MODEL), BF16)],
        compiler_params=_cparams(2),
        name="modnorm",
    )(x, ctx, gain, mod_lat, mod_ctx, mod_lat, mod_ctx)


def _rotary(z, cos, s1, s2):
    return z * cos + pltpu.roll(z, 96, 1) * s1 + pltpu.roll(z, 32, 1) * s2


def _ret_kernel(lg_ref, q_ref, g_ref, k_ref, v_ref, cos_ref, s1_ref, s2_ref, gain_ref,
                o_ref, acc_ref, qs_ref, ks_ref):
    c = RET_CHUNK
    nh = RET_HEADS_PER_STEP
    scale = HEAD_DIM ** -0.5
    dist = (lax.broadcasted_iota(jnp.int32, (c, c), 0)
            - lax.broadcasted_iota(jnp.int32, (c, c), 1)).astype(F32)
    tcol = lax.broadcasted_iota(jnp.int32, (c, 1), 0).astype(F32)

    def head_consts(hp):
        h = pl.program_id(1) * nh + hp
        la_f, la_b = lg_ref[0, h], lg_ref[1, h]
        dmat = (jnp.where(dist >= 0.0, jnp.exp(la_f * jnp.maximum(dist, 0.0)), 0.0)
                + jnp.where(dist <= 0.0, jnp.exp(la_b * jnp.maximum(-dist, 0.0)), 0.0))
        fwd_dec = (jnp.exp(la_f * (tcol + 1.0)), jnp.exp(la_f * (c - 1.0 - tcol)),
                   jnp.exp(jnp.full((1, LANE), la_f * c, F32)))
        bwd_dec = (jnp.exp(la_b * (c - tcol)), jnp.exp(la_b * tcol),
                   jnp.exp(jnp.full((1, LANE), la_b * c, F32)))
        return dmat, fwd_dec, bwd_dec

    consts = [head_consts(hp) for hp in range(nh)]

    def fwd(i, sts, base, rot):
        start = pl.multiple_of(base + i * c, c)
        rows = pl.ds(start, c)
        if rot:
            cos, s1, s2 = cos_ref[rows, :], s1_ref[rows, :], s2_ref[rows, :]
        out = []
        for hp in range(nh):
            hl = pl.ds(hp * LANE, LANE)
            dmat, (qdec, kdec, cdec), _ = consts[hp]
            q = q_ref[rows, hl].astype(F32) * scale
            k = k_ref[rows, hl].astype(F32)
            v = v_ref[rows, hl]
            if rot:
                q = _rotary(q, cos, s1, s2)
                k = _rotary(k, cos, s1, s2)
            qs_ref[rows, hl] = q
            ks_ref[rows, hl] = k
            att = _dot_nt(q.astype(BF16), k.astype(BF16)) * dmat
            st = sts[hp]
            acc_ref[rows, hl] = (_dot(att.astype(BF16), v)
                                 + _dot_nt((q * qdec).astype(BF16), st.astype(BF16)))
            out.append(cdec * st + _dot_tn(v, (k * kdec).astype(BF16)))
        return tuple(out)

    def bwd(i, sts, base, n):
        start = pl.multiple_of(base + (n - 1 - i) * c, c)
        rows = pl.ds(start, c)
        out = []
        for hp in range(nh):
            hl = pl.ds(hp * LANE, LANE)
            _, _, (qdec, kdec, cdec) = consts[hp]
            q = qs_ref[rows, hl]
            k = ks_ref[rows, hl]
            v = v_ref[rows, hl]
            st = sts[hp]
            o = acc_ref[rows, hl] + _dot_nt((q * qdec).astype(BF16), st.astype(BF16))
            gate = g_ref[rows, hl].astype(F32)
            o_ref[rows, hl] = (_rms(o) * gain_ref[:, hl] * _silu(gate)).astype(o_ref.dtype)
            out.append(cdec * st + _dot_tn(v, (k * kdec).astype(BF16)))
        return tuple(out)

    zero = tuple(jnp.zeros((HEAD_DIM, HEAD_DIM), F32) for _ in range(nh))
    n_ctx, n_lat = T_CTX // c, T_LAT // c
    sts = lax.fori_loop(0, n_ctx, functools.partial(fwd, base=T_LAT, rot=False), zero)
    lax.fori_loop(0, n_lat, functools.partial(fwd, base=0, rot=True), sts, unroll=RET_UNROLL)
    sts = lax.fori_loop(0, n_ctx, functools.partial(bwd, base=T_LAT, n=n_ctx), zero)
    lax.fori_loop(0, n_lat, functools.partial(bwd, base=0, n=n_lat), sts, unroll=RET_UNROLL)


def _retention(p, log_g, cos, s1, s2, ret_norm):
    nh = RET_HEADS_PER_STEP
    hw = nh * LANE
    def col(field):
        first = _block_index(EVEN_OFF[field], hw)
        return pl.BlockSpec((None, T_ALL, hw), lambda b, h, lg: (b, 0, first + h))

    tab = pl.BlockSpec((T_LAT, LANE), lambda b, h, lg: (0, 0))
    return pl.pallas_call(
        _ret_kernel,
        grid_spec=pltpu.PrefetchScalarGridSpec(
            num_scalar_prefetch=1,
            grid=(BATCH, RET_HEADS // nh),
            in_specs=[col("ret_q"), col("ret_gate"), col("ret_k"), col("ret_v"), tab, tab, tab,
                      pl.BlockSpec((1, hw), lambda b, h, lg: (0, h))],
            out_specs=pl.BlockSpec((None, T_ALL, hw), lambda b, h, lg: (b, 0, h)),
            scratch_shapes=[pltpu.VMEM((T_ALL, hw), F32), pltpu.VMEM((T_ALL, hw), F32),
                            pltpu.VMEM((T_ALL, hw), F32)]),
        out_shape=jax.ShapeDtypeStruct((BATCH, T_ALL, RET_DIM), BF16),
        compiler_params=_cparams(2),
        name="retention",
    )(log_g, p, p, p, p, cos, s1, s2, ret_norm)


def _gmlp_kernel(u_ref, v_ref, gain_ref, ws_ref, bst_ref, o_ref):
    u = jax.nn.gelu(u_ref[...].astype(F32))
    v = _rms(jax.nn.gelu(v_ref[...].astype(F32))) * gain_ref[...]
    vb = v.astype(BF16)
    for n in range(ROW_TILE // GMLP_CHUNK):
        r0 = n * GMLP_CHUNK
        for g in range(GMLP_GROUPS):
            c0 = g * LANE
            mixed = (_dot(ws_ref[g], vb[r0:r0 + GMLP_CHUNK, c0:c0 + LANE])
                     + bst_ref[:, g:g + 1])
            o_ref[r0:r0 + GMLP_CHUNK, c0:c0 + LANE] = (
                u[r0:r0 + GMLP_CHUNK, c0:c0 + LANE] * mixed).astype(o_ref.dtype)


def _gmlp(p, gmlp_norm, ws_bf16, bs_t):
    u_blk = _block_index(EVEN_OFF["gmlp_u"], GMLP_DIM)
    v_blk = _block_index(EVEN_OFF["gmlp_v"], GMLP_DIM)
    assert T_ALL % ROW_TILE == 0 and ROW_TILE % GMLP_CHUNK == 0
    return pl.pallas_call(
        _gmlp_kernel,
        grid=(BATCH, T_ALL // ROW_TILE),
        in_specs=[pl.BlockSpec((None, ROW_TILE, GMLP_DIM), lambda b, i: (b, i, u_blk)),
                  pl.BlockSpec((None, ROW_TILE, GMLP_DIM), lambda b, i: (b, i, v_blk)),
                  pl.BlockSpec((1, GMLP_DIM), lambda b, i: (0, 0)),
                  pl.BlockSpec((GMLP_GROUPS, GMLP_CHUNK, GMLP_CHUNK), lambda b, i: (0, 0, 0)),
                  pl.BlockSpec((GMLP_CHUNK, GMLP_GROUPS), lambda b, i: (0, 0))],
        out_specs=pl.BlockSpec((None, ROW_TILE, GMLP_DIM), lambda b, i: (b, i, 0)),
        out_shape=jax.ShapeDtypeStruct((BATCH, T_ALL, GMLP_DIM), BF16),
        compiler_params=_cparams(2),
        name="gmlp",
    )(p, p, gmlp_norm, ws_bf16, bs_t)


def _na_kernel(q_ref, k_ref, v_ref, bias_ref, o_ref):
    i = pl.program_id(2)
    w0 = jnp.clip(i * NA_QROWS - WIN_ROWS // 2, 0, GRID_ROWS - NA_KROWS)
    start = pl.multiple_of(w0 * GRID_W, GRID_W)
    scale = HEAD_DIM ** -0.5
    for hp in range(NA_HEADS_PER_STEP):
        hl = pl.ds(hp * LANE, LANE)
        q = q_ref[:, hl]
        s_win = _dot_nt(q, k_ref[pl.ds(start, NA_KT), hl]) * scale + bias_ref[hp]
        s_ctx = _dot_nt(q, k_ref[pl.ds(T_LAT, T_CTX), hl]) * scale
        m = jnp.maximum(jnp.max(s_win, axis=-1, keepdims=True),
                        jnp.max(s_ctx, axis=-1, keepdims=True))
        p_win = jnp.exp(s_win - m)
        p_ctx = jnp.exp(s_ctx - m)
        denom = jnp.sum(p_win, axis=-1, keepdims=True) + jnp.sum(p_ctx, axis=-1, keepdims=True)
        o = (_dot(p_win.astype(BF16), v_ref[pl.ds(start, NA_KT), hl])
             + _dot(p_ctx.astype(BF16), v_ref[pl.ds(T_LAT, T_CTX), hl]))
        o_ref[:, hl] = (o / denom).astype(o_ref.dtype)


def _na_geometry():
    n_blk = GRID_ROWS // NA_QROWS
    dr00 = []
    row_ok = np.zeros((3, NA_QROWS, NA_KROWS), bool)
    for kind, i_rep in enumerate((0, 1, n_blk - 1)):
        r0 = i_rep * NA_QROWS
        w0 = int(np.clip(r0 - WIN_ROWS // 2, 0, GRID_ROWS - NA_KROWS))
        r = r0 + np.arange(NA_QROWS)[:, None]
        kr = w0 + np.arange(NA_KROWS)[None, :]
        rstart = np.clip(r - WIN_ROWS // 2, 0, GRID_ROWS - WIN_ROWS)
        row_ok[kind] = (kr >= rstart) & (kr < rstart + WIN_ROWS)
        dr00.append(w0 - r0 + WIN_ROWS - 1)
    c = np.arange(GRID_W)[:, None]
    kc = np.arange(GRID_W)[None, :]
    cstart = np.clip(c - WIN_COLS // 2, 0, GRID_W - WIN_COLS)
    col_ok = (kc >= cstart) & (kc < cstart + WIN_COLS)
    valid = row_ok[:, :, None, :, None] & col_ok[None, None, :, None, :]
    return dr00, valid.reshape(3, NA_QT, NA_KT)


def _na_bias_table(rpb):
    n_dr, n_dc = 2 * WIN_ROWS - 1, 2 * WIN_COLS - 1
    w = jnp.concatenate([rpb[..., WIN_COLS - 1:], jnp.zeros((NA_HEADS, n_dr, LANE - n_dc), F32),
                         rpb[..., :WIN_COLS - 1]], axis=-1)
    toep = jnp.tile(w, (1, 1, GRID_W))[..., :GRID_W * (LANE - 1)]
    toep = toep.reshape(NA_HEADS, n_dr, GRID_W, LANE - 1)[..., :GRID_W]
    dr00, valid = _na_geometry()
    front = max(0, NA_QROWS - 1 - min(dr00))
    back = max(0, max(dr00) + NA_KROWS - n_dr)
    toep = jnp.pad(toep.transpose(0, 2, 1, 3), ((0, 0), (0, 0), (front, back), (0, 0)))
    toep = toep.astype(BF16).reshape(NA_HEADS, GRID_W, -1)
    kinds = []
    for d0 in dr00:
        starts = [(d0 - rl + front) * GRID_W for rl in range(NA_QROWS)]
        kinds.append(jnp.stack([toep[:, :, s:s + NA_KT] for s in starts], axis=1))
    tab = jnp.stack(kinds, axis=0).reshape(3, NA_HEADS, NA_QT, NA_KT)
    return jnp.where(jnp.asarray(valid)[:, None], tab, jnp.asarray(NEG_BIG, BF16))


def _na(p, bias):
    n_blk = T_LAT // NA_QT
    nh = NA_HEADS_PER_STEP
    hw = nh * LANE
    k0, v0 = _block_index(ODD_OFF["na_k"], hw), _block_index(ODD_OFF["na_v"], hw)
    kind = lambda i: jnp.where(i == 0, 0, jnp.where(i == n_blk - 1, 2, 1))
    return pl.pallas_call(
        _na_kernel,
        grid=(BATCH, NA_HEADS // nh, n_blk),
        in_specs=[pl.BlockSpec((None, NA_QT, hw), lambda b, h, i: (b, i, h)),
                  pl.BlockSpec((None, T_ALL, hw), lambda b, h, i: (b, 0, k0 + h)),
                  pl.BlockSpec((None, T_ALL, hw), lambda b, h, i: (b, 0, v0 + h)),
                  pl.BlockSpec((None, nh, NA_QT, NA_KT), lambda b, h, i: (kind(i), h, 0, 0))],
        out_specs=pl.BlockSpec((None, NA_QT, hw), lambda b, h, i: (b, i, h)),
        out_shape=jax.ShapeDtypeStruct((BATCH, T_LAT, NA_DIM), BF16),
        compiler_params=_cparams(3),
        name="nbr_attention",
    )(p, p, p, bias)


def _gla_kernel(q_ref, k_ref, v_ref, g_ref, lr_ref, wcat_ref, bup_ref, gain_ref,
                o_ref, acc_ref):
    r, c = GLA_SUPER, GLA_SUB
    n_sub = r // c
    dv2 = 2 * GLA_DV
    qscale = GLA_DK ** -0.5
    row = lax.broadcasted_iota(jnp.int32, (r, r), 0)
    col = lax.broadcasted_iota(jnp.int32, (r, r), 1)
    sub_shift = c.bit_length() - 1
    same = (row >> sub_shift) == (col >> sub_shift)
    blk_ones = jnp.where(same, 1.0, 0.0).astype(BF16)
    second_head_lane = lax.broadcasted_iota(jnp.int32, (1, LANE), 1) >= GLA_DK
    st_mask = ((lax.broadcasted_iota(jnp.int32, (dv2, LANE), 0) >= GLA_DV)
               == (lax.broadcasted_iota(jnp.int32, (dv2, LANE), 1) >= GLA_DK))

    for d in range(2):
        tri = same & ((row >= col) if d == 0 else (col >= row))
        tri_b = jnp.where(tri, 1.0, 0.0).astype(BF16)

        def pair_step(hp, start, lr, st, d=d, tri=tri, tri_b=tri_b):
            kl = pl.ds(hp * LANE, LANE)
            vl = pl.ds(hp * dv2, dv2)
            zz = _dot(lr, wcat_ref[d, :, pl.ds(hp * 2 * LANE, 2 * LANE)])
            z = zz[:, :LANE] + zz[:, LANE:] + bup_ref[d, :, kl]
            la = (jnp.minimum(z, 0.0) - jnp.log(1.0 + jnp.exp(-jnp.abs(z)))) * (1.0 / GLA_TAU)
            hl = jnp.concatenate(_split2(la), axis=1)
            bb = _dot(tri_b, hl)
            b = bb[:, :LANE] + bb[:, LANE:]
            bb = _dot(blk_ones, hl)
            bt = bb[:, :LANE] + bb[:, LANE:]
            q = q_ref[pl.ds(start, r), kl].astype(F32) * qscale
            k = k_ref[pl.ds(start, r), kl].astype(F32)
            v = v_ref[pl.ds(start, r), vl]
            qi = q * jnp.exp(b)
            qib = qi.astype(BF16)
            ki = (k * jnp.exp(-b)).astype(BF16)
            ko = (k * jnp.exp(bt - b)).astype(BF16)
            dec = jnp.exp(bt)
            outs = []
            for j in range(2):
                qm = jnp.where(second_head_lane == (j == 1), qi, 0.0).astype(BF16)
                att = jnp.where(tri, _dot_nt(qm, ki), 0.0)
                outs.append(_dot(att.astype(BF16), v[:, j * GLA_DV:(j + 1) * GLA_DV]))
            o_intra = jnp.concatenate(outs, axis=1)
            parts = [None] * n_sub
            for s in (range(n_sub) if d == 0 else reversed(range(n_sub))):
                lo_r, hi_r = s * c, (s + 1) * c
                parts[s] = o_intra[lo_r:hi_r] + _dot_nt(qib[lo_r:hi_r], st.astype(BF16))
                st = (st * dec[lo_r:lo_r + 1, :]
                      + jnp.where(st_mask, _dot_tn(v[lo_r:hi_r], ko[lo_r:hi_r]), 0.0))
            o = jnp.concatenate(parts, axis=0)
            if d == 0:
                acc_ref[pl.ds(start, r), vl] = o
            else:
                o = acc_ref[pl.ds(start, r), vl] + o
                gate = g_ref[pl.ds(start, r), vl].astype(F32)
                on = (jnp.concatenate([_rms(o[:, :GLA_DV]), _rms(o[:, GLA_DV:])], axis=1)
                      * gain_ref[:, vl])
                o_ref[pl.ds(start, r), vl] = (on * _silu(gate)).astype(o_ref.dtype)
            return st

        def step(i, sts, base, n, d=d):
            idx = i if d == 0 else n - 1 - i
            start = pl.multiple_of(base + idx * r, r)
            lr = lr_ref[pl.ds(start, r), :]
            return tuple(pair_step(hp, start, lr, sts[hp]) for hp in range(GLA_PAIRS))

        sts = tuple(jnp.zeros((dv2, LANE), F32) for _ in range(GLA_PAIRS))
        sts = lax.fori_loop(0, T_CTX // r, functools.partial(step, base=T_LAT, n=T_CTX // r), sts)
        lax.fori_loop(0, T_LAT // r, functools.partial(step, base=0, n=T_LAT // r), sts,
                      unroll=GLA_UNROLL)


def _gla(p, lr, w_cat, b_up, gla_norm):
    kw = GLA_PAIRS * LANE
    vw = GLA_PAIRS * 2 * GLA_DV
    q0, k0 = _block_index(ODD_OFF["gla_q"], kw), _block_index(ODD_OFF["gla_k"], kw)
    v0, g0 = _block_index(ODD_OFF["gla_v"], vw), _block_index(ODD_OFF["gla_gate"], vw)
    return pl.pallas_call(
        _gla_kernel,
        grid=(BATCH, GLA_HEADS // (2 * GLA_PAIRS)),
        in_specs=[pl.BlockSpec((None, T_ALL, kw), lambda b, j: (b, 0, q0 + j)),
                  pl.BlockSpec((None, T_ALL, kw), lambda b, j: (b, 0, k0 + j)),
                  pl.BlockSpec((None, T_ALL, vw), lambda b, j: (b, 0, v0 + j)),
                  pl.BlockSpec((None, T_ALL, vw), lambda b, j: (b, 0, g0 + j)),
                  pl.BlockSpec((None, T_ALL, LANE), lambda b, j: (b, 0, 0)),
                  pl.BlockSpec((2, LANE, 2 * kw), lambda b, j: (0, 0, j)),
                  pl.BlockSpec((2, 1, kw), lambda b, j: (0, 0, j)),
                  pl.BlockSpec((1, vw), lambda b, j: (0, j))],
        out_specs=pl.BlockSpec((None, T_ALL, vw), lambda b, j: (b, 0, j)),
        out_shape=jax.ShapeDtypeStruct((BATCH, T_ALL, GLA_V_DIM), BF16),
        scratch_shapes=[pltpu.VMEM((T_ALL, vw), F32)],
        compiler_params=_cparams(2),
        name="gla",
    )(p, p, p, p, lr, w_cat, b_up, gla_norm)


def _out_kernel(a1_ref, a2_ref, w1_ref, w2_ref, x_ref, gl_ref, gc_ref, gain_ref,
                scl_ref, scc_ref, shl_ref, shc_ref, rw_ref, xo_ref, h_ref, lg_ref, *, tile,
                has_ctx):
    half = tile // 2
    for r0 in (0, half):
        rows = pl.ds(r0, half)
        sel = functools.partial(_region_select, tile, row0=r0, n_rows=half, has_ctx=has_ctx)
        y = _dot(a1_ref[rows, :], w1_ref[...]) + _dot(a2_ref[rows, :], w2_ref[...])
        xn = x_ref[rows, :] + sel(gl_ref, gc_ref) * y
        xo_ref[rows, :] = xn
        h = _rms(xn) * gain_ref[...] * (1.0 + sel(scl_ref, scc_ref)) + sel(shl_ref, shc_ref)
        h_hi = h.astype(BF16)
        h_ref[rows, :] = h_hi
        h_lo = (h - h_hi.astype(F32)).astype(BF16)
        lg_ref[rows, :] = _dot(h_hi, rw_ref[...]) + _dot(h_lo, rw_ref[...])


def _out_proj(a1, a2, w_out_bf16, x, mod_lat, mod_ctx, norm_gain, rw_split, tile, n_tiles):
    half = w_out_bf16.shape[0] // 2
    t_out = n_tiles * tile
    assert t_out in (T_LAT, T_ALL) and tile % 32 == 0
    row = lambda width: pl.BlockSpec((None, tile, width), lambda b, i: (b, i, 0))
    return pl.pallas_call(
        functools.partial(_out_kernel, tile=tile, has_ctx=t_out > T_LAT),
        grid=(BATCH, n_tiles),
        in_specs=[row(half), row(half),
                  pl.BlockSpec((half, D_MODEL), lambda b, i: (0, 0)),
                  pl.BlockSpec((half, D_MODEL), lambda b, i: (1, 0)),
                  row(D_MODEL), _lat_spec(2), _ctx_spec(2),
                  pl.BlockSpec((1, D_MODEL), lambda b, i: (0, 0)),
                  _lat_spec(4), _ctx_spec(4), _lat_spec(3), _ctx_spec(3),
                  pl.BlockSpec((D_MODEL, 2 * N_EXPERTS), lambda b, i: (0, 0))],
        out_specs=[row(D_MODEL), row(D_MODEL), row(2 * N_EXPERTS)],
        out_shape=[jax.ShapeDtypeStruct((BATCH, t_out, D_MODEL), F32),
                   jax.ShapeDtypeStruct((BATCH, t_out, D_MODEL), BF16),
                   jax.ShapeDtypeStruct((BATCH, t_out, 2 * N_EXPERTS), F32)],
        compiler_params=_cparams(2),
        name="out_proj",
    )(a1, a2, w_out_bf16, w_out_bf16, x, mod_lat, mod_ctx, norm_gain,
      mod_lat, mod_ctx, mod_lat, mod_ctx, rw_split)


def _moe_up_kernel(x_ref, wg_ref, wu_ref, o_ref):
    x = x_ref[...]
    a = _dot(x, wg_ref[...].astype(BF16))
    u = _dot(x, wu_ref[...].astype(BF16))
    o_ref[...] = (_silu(a) * u).astype(o_ref.dtype)


def _moe_down_kernel(h_ref, wd_ref, g_ref, o_ref):
    gcol = _dot_tn(g_ref[...], jnp.ones((GATE_ROWS, LANE), BF16))
    y = _dot(h_ref[...], wd_ref[...].astype(BF16))
    for c in range(o_ref.shape[1] // LANE):
        cols = pl.ds(c * LANE, LANE)
        o_ref[:, cols] = (y[:, c * LANE:(c + 1) * LANE] * gcol).astype(o_ref.dtype)


def _moe(xs, gates, w_gate, w_up, w_down, layer):
    e, m, _ = xs.shape
    tn = MOE_TN
    wspec = pl.BlockSpec((None, None, D_MODEL, tn), lambda e, f: (layer, e, 0, f))
    hmid = pl.pallas_call(
        _moe_up_kernel,
        grid=(e, D_EXPERT // tn),
        in_specs=[pl.BlockSpec((None, m, D_MODEL), lambda e, f: (e, 0, 0)), wspec, wspec],
        out_specs=pl.BlockSpec((None, m, tn), lambda e, f: (e, 0, f)),
        out_shape=jax.ShapeDtypeStruct((e, m, D_EXPERT), BF16),
        compiler_params=_cparams(2),
        name="moe_up",
    )(xs, w_gate, w_up)
    tn = MOE_DOWN_TN
    return pl.pallas_call(
        _moe_down_kernel,
        grid=(e, D_MODEL // tn),
        in_specs=[pl.BlockSpec((None, m, D_EXPERT), lambda e, f: (e, 0, 0)),
                  pl.BlockSpec((None, None, D_EXPERT, tn), lambda e, f: (layer, e, 0, f)),
                  pl.BlockSpec((None, GATE_ROWS, m), lambda e, f: (e, 0, 0))],
        out_specs=pl.BlockSpec((None, m, tn), lambda e, f: (e, 0, f)),
        out_shape=jax.ShapeDtypeStruct((e, m, D_MODEL), BF16),
        compiler_params=_cparams(2),
        name="moe_down",
    )(hmid, w_down, gates)


def _combine_moe(span_ref, pair_tok_ref, ys_hbm, acc_ref, buf_ref, sem, *, tile, n_pairs):
    b, i = pl.program_id(0), pl.program_id(1)
    n_tiles = pl.num_programs(1)
    step = b * n_tiles + i
    n_steps = pl.num_programs(0) * n_tiles
    w = PAIR_WIN
    shift = w.bit_length() - 1
    last_start = n_pairs // w - PAIR_CHUNK

    def first_window(s):
        return jnp.minimum(span_ref[2 * s] >> shift, last_start)

    def chunk_copy(sample, win, slot):
        return pltpu.make_async_copy(
            ys_hbm.at[pl.ds(sample * n_pairs + win * w, PAIR_CHUNK * w), :],
            buf_ref.at[slot], sem.at[slot])

    def window_copy(win, slot):
        return pltpu.make_async_copy(ys_hbm.at[pl.ds(b * n_pairs + win * w, w), :],
                                     buf_ref.at[slot, pl.ds(0, w), :], sem.at[slot])

    slot = step & 1
    w0 = first_window(step)

    @pl.when(step == 0)
    def _():
        chunk_copy(b, w0, slot).start()

    @pl.when(step + 1 < n_steps)
    def _():
        next_sample = jnp.where(i + 1 < n_tiles, b, b + 1)
        chunk_copy(next_sample, first_window(step + 1), 1 - slot).start()

    chunk_copy(b, w0, slot).wait()
    tok = i * tile + lax.broadcasted_iota(jnp.int32, (tile, 1), 0)

    def select(win):
        return jnp.where(tok == pair_tok_ref[pl.ds(win, 1), :], 1.0, 0.0).astype(BF16)

    sel = jnp.concatenate([select(w0 + c) for c in range(PAIR_CHUNK)], axis=1)
    acc_ref[...] = _dot(sel, buf_ref[slot])

    hi = span_ref[2 * step + 1]
    end_win = jnp.where(hi > 0, ((hi - 1) >> shift) + 1, 0)

    def body(win, carry):
        cp = window_copy(win, slot)
        cp.start()
        cp.wait()
        acc_ref[...] += _dot(select(win), buf_ref[slot, pl.ds(0, w), :])
        return carry

    lax.fori_loop(w0 + PAIR_CHUNK, end_win, body, 0)
    return acc_ref[...]


def _res_mid_kernel(span_ref, x_ref, pair_tok_ref, ys_hbm, gl_ref, gc_ref, gain_ref,
                    scl_ref, scc_ref, shl_ref, shc_ref, xo_ref, h_ref, acc_ref, buf_ref, sem,
                    *, tile, n_pairs):
    moe = _combine_moe(span_ref, pair_tok_ref, ys_hbm, acc_ref, buf_ref, sem,
                       tile=tile, n_pairs=n_pairs)
    xn = x_ref[...] + _region_select(tile, gl_ref, gc_ref) * moe
    xo_ref[...] = xn
    h_ref[...] = (_rms(xn) * gain_ref[...] * (1.0 + _region_select(tile, scl_ref, scc_ref))
                  + _region_select(tile, shl_ref, shc_ref)).astype(h_ref.dtype)


def _res_final_kernel(span_ref, x_ref, pair_tok_ref, ys_hbm, g_ref, gain_ref, o_ref,
                      acc_ref, buf_ref, sem, *, tile, n_pairs):
    moe = _combine_moe(span_ref, pair_tok_ref, ys_hbm, acc_ref, buf_ref, sem,
                       tile=tile, n_pairs=n_pairs)
    o_ref[...] = _rms(x_ref[...] + g_ref[...] * moe) * gain_ref[...]


def _combine_call(kernel_fn, name, tile, t_rows, spans, x, pair_tok, ys, params, param_specs,
                  out_dtypes):
    n_pairs = pair_tok.shape[1] * PAIR_WIN
    assert t_rows % tile == 0 and n_pairs // PAIR_WIN >= PAIR_CHUNK
    row = pl.BlockSpec((None, tile, D_MODEL), lambda b, i, *_: (b, i, 0))
    return pl.pallas_call(
        functools.partial(kernel_fn, tile=tile, n_pairs=n_pairs),
        grid_spec=pltpu.PrefetchScalarGridSpec(
            num_scalar_prefetch=1,
            grid=(BATCH, t_rows // tile),
            in_specs=[row,
                      pl.BlockSpec((None,) + pair_tok.shape[1:], lambda b, i, *_: (b, 0, 0)),
                      pl.BlockSpec(memory_space=pl.ANY)] + param_specs,
            out_specs=[row] * len(out_dtypes),
            scratch_shapes=[pltpu.VMEM((tile, D_MODEL), F32),
                            pltpu.VMEM((2, PAIR_CHUNK * PAIR_WIN, D_MODEL), BF16),
                            pltpu.SemaphoreType.DMA((2,))]),
        out_shape=[jax.ShapeDtypeStruct((BATCH, t_rows, D_MODEL), dt) for dt in out_dtypes],
        compiler_params=_cparams(2),
        name=name,
    )(spans, x, pair_tok, ys, *params)


def _res_mid(x, spans, pair_tok, ys, mod_lat, mod_ctx, next_lat, next_ctx, norm_gain_next):
    specs = [_lat_spec(5), _ctx_spec(5), pl.BlockSpec((1, D_MODEL), lambda b, i, *_: (0, 0)),
             _lat_spec(1), _ctx_spec(1), _lat_spec(0), _ctx_spec(0)]
    params = (mod_lat, mod_ctx, norm_gain_next, next_lat, next_ctx, next_lat, next_ctx)
    return _combine_call(_res_mid_kernel, "ffn_combine_residual_prenorm", COMB_TILE, T_ALL, spans,
                         x, pair_tok, ys, params, specs, (F32, BF16))


def _res_final(x, spans, pair_tok, ys, mod_lat, norm_final):
    specs = [_lat_spec(5), pl.BlockSpec((1, D_MODEL), lambda b, i, *_: (0, 0))]
    return _combine_call(_res_final_kernel, "ffn_combine_residual_final_norm", COMB_TILE, T_LAT,
                         spans, x, pair_tok, ys, (mod_lat, norm_final), specs, (F32,))[0]


def _rope_tables():
    half = HEAD_DIM // 2
    nf = half // 2
    t = np.arange(T_LAT)
    rows, cols = t // GRID_W, t % GRID_W
    freq = (np.float32(ROPE_BASE) ** (-np.arange(nf, dtype=np.float32) / np.float32(nf))).astype(np.float32)
    ang_r = rows.astype(np.float32)[:, None] * freq
    ang_c = cols.astype(np.float32)[:, None] * freq
    zeros = np.zeros_like(ang_r)
    cos = np.concatenate([np.cos(ang_r), np.cos(ang_r), np.cos(ang_c), np.cos(ang_c)], axis=1)
    s1 = np.concatenate([-np.sin(ang_r), zeros, -np.sin(ang_c), zeros], axis=1)
    s2 = np.concatenate([zeros, np.sin(ang_r), zeros, np.sin(ang_c)], axis=1)
    return tuple(jnp.asarray(a, F32) for a in (cos, s1, s2))


def _router_split(rw):
    hi = rw.astype(BF16)
    lo = (rw - hi.astype(F32)).astype(BF16)
    return jnp.concatenate([hi, lo], axis=1)


def _route(logits, t0, t1, cap):
    aff = jax.nn.softmax(logits[:, t0:t1], axis=-1)
    g, idx = lax.top_k(aff.transpose(0, 2, 1), cap)
    return g, idx + t0


def _expert_ffn(h, logits2, w_gate, w_up, w_down, layer, with_ctx, tile, later=()):
    logits = logits2[..., :N_EXPERTS] + logits2[..., N_EXPERTS:]
    g, idx = _route(logits, 0, T_LAT, EC_CAPACITY * T_LAT // N_EXPERTS)
    if with_ctx:
        gc, idxc = _route(logits, T_LAT, T_ALL, EC_CAPACITY * T_CTX // N_EXPERTS)
        g = jnp.concatenate([g, gc], axis=-1)
        idx = jnp.concatenate([idx, idxc], axis=-1)
    cap = idx.shape[-1]
    t_rows = h.shape[1]
    n_pairs = N_EXPERTS * cap
    assert n_pairs % PAIR_WIN == 0
    flat = (idx + jnp.arange(BATCH)[:, None, None] * t_rows).transpose(1, 0, 2).reshape(N_EXPERTS, -1)
    g_rows = g.transpose(1, 0, 2).reshape(N_EXPERTS, 1, BATCH * cap)
    g_hi = g_rows.astype(BF16)
    g_mid = (g_rows - g_hi.astype(F32)).astype(BF16)
    g_lo = (g_rows - g_hi.astype(F32) - g_mid.astype(F32)).astype(BF16)
    gates = jnp.concatenate([g_hi, g_mid, g_lo,
                             jnp.zeros((N_EXPERTS, GATE_ROWS - 3, BATCH * cap), BF16)], axis=1)
    xs = h.reshape(BATCH * t_rows, D_MODEL)[flat]
    y = _moe(xs, gates, w_gate, w_up, w_down, layer)
    y, later = lax.optimization_barrier((y, later))
    src = (jnp.arange(N_EXPERTS)[None, :, None] * (BATCH * cap)
           + jnp.arange(BATCH)[:, None, None] * cap + jnp.arange(cap)[None, None, :])
    cap_lat = EC_CAPACITY * T_LAT // N_EXPERTS
    parts = []
    for sl in ((slice(0, cap_lat), slice(cap_lat, cap)) if with_ctx else (slice(0, cap),)):
        parts.append(lax.sort((idx[..., sl].reshape(BATCH, -1), src[..., sl].reshape(BATCH, -1)),
                              dimension=1, num_keys=1))
    pair_tok = jnp.concatenate([p[0] for p in parts], axis=1)
    src = jnp.concatenate([p[1] for p in parts], axis=1)
    ys = y.reshape(-1, D_MODEL)[src.reshape(-1)]
    bounds = jnp.arange(0, t_rows + tile, tile)
    below = jnp.sum(pair_tok[:, :, None] < bounds[None, None, :], axis=1)
    spans = jnp.stack([below[:, :-1], below[:, 1:]], axis=-1).reshape(-1).astype(jnp.int32)
    return spans, pair_tok.reshape(BATCH, n_pairs // PAIR_WIN, PAIR_WIN), ys, later


def kernel(x, c, ctx, c_ctx, ada_w, ada_b, norm_mix, norm_ffn, norm_final, ev_w_in, ev_w_out,
           ret_gamma_logit, ret_norm, gmlp_norm, gmlp_ws, gmlp_bs, od_w_in, od_w_out, na_rpb,
           gla_w_up, gla_b_up, gla_norm, router_w, moe_w_gate, moe_w_up, moe_w_down):
    assert x.shape == (BATCH, T_LAT, D_MODEL) and ctx.shape == (BATCH, T_CTX, D_MODEL)
    assert ada_w.shape[0] == 2 and ev_w_in.shape[0] == 1 and od_w_in.shape[0] == 1

    cond = jnp.concatenate([jax.nn.silu(c), jax.nn.silu(c_ctx)[None],
                            jnp.zeros((11, D_MODEL), F32)], axis=0).astype(BF16)
    def modulation(cond_rows, l):
        m = _mod_matmul(cond_rows, ada_w, l) + ada_b[l]
        m = m[:BATCH + 1].reshape(BATCH + 1, 6, 1, D_MODEL)
        return m[:BATCH], m[BATCH]

    mod_lat, mod_ctx = [None, None], [None, None]
    mod_lat[0], mod_ctx[0] = modulation(cond, 0)

    cos, s1, s2 = _rope_tables()

    xa, h = _modnorm(x, ctx, norm_mix[0:1], mod_lat[0], mod_ctx[0], 1, 0)
    p = _proj_matmul(h.reshape(BATCH * T_ALL, D_MODEL), ev_w_in, 0, EVEN_COLS, 2176, 1024)
    p = p.reshape(BATCH, T_ALL, EVEN_COLS)
    log_g = jax.nn.log_sigmoid(ret_gamma_logit[0].astype(F32))
    ret = _retention(p, log_g, cos, s1, s2, ret_norm[0:1])
    gm = _gmlp(p, gmlp_norm[0:1], gmlp_ws[0].astype(BF16), gmlp_bs[0].T)
    xa, h2, logits = _out_proj(ret, gm, ev_w_out[0].astype(BF16), xa, mod_lat[0], mod_ctx[0],
                               norm_ffn[0:1], _router_split(router_w[0]), MIX_TILE,
                               T_ALL // MIX_TILE)
    spans, pair_tok, ys, (cond, na_rpb, gla_w_up) = _expert_ffn(
        h2, logits, moe_w_gate, moe_w_up, moe_w_down, 0, True, COMB_TILE,
        later=(cond, na_rpb, gla_w_up))
    mod_lat[1], mod_ctx[1] = modulation(cond, 1)
    xa, h = _res_mid(xa, spans, pair_tok, ys, mod_lat[0], mod_ctx[0], mod_lat[1], mod_ctx[1],
                     norm_mix[1:2])

    h2d = h.reshape(BATCH * T_ALL, D_MODEL)
    w_in_t = od_w_in[0].T
    p = _proj_matmul_t(h2d, w_in_t, ODD_MAIN_COLS, 2176, 1024)
    p = p.reshape(BATCH, T_ALL, ODD_MAIN_COLS)
    w_lr_t = jnp.pad(w_in_t[ODD_MAIN_COLS:], ((0, LANE - 2 * GLA_GATE_RANK), (0, 0))).astype(BF16)
    lr = _proj_matmul_t(h2d, w_lr_t, LANE, 2176, LANE).reshape(BATCH, T_ALL, LANE)
    o_na = _na(p, _na_bias_table(na_rpb[0]))
    w_up_full = jnp.zeros((2, LANE, GLA_QK_DIM), F32)
    for d in range(2):
        w_up_full = w_up_full.at[d, d * GLA_GATE_RANK:(d + 1) * GLA_GATE_RANK].set(gla_w_up[0, d])
    w_hi = w_up_full.astype(BF16)
    w_lo = (w_up_full - w_hi.astype(F32)).astype(BF16)
    n_pair = GLA_QK_DIM // LANE
    w_cat = jnp.stack([w_hi.reshape(2, LANE, n_pair, LANE), w_lo.reshape(2, LANE, n_pair, LANE)],
                      axis=3).reshape(2, LANE, 2 * GLA_QK_DIM)
    gla = _gla(p, lr, w_cat, gla_b_up[0][:, None, :], gla_norm[0:1])
    xl, h2, logits = _out_proj(o_na, gla, od_w_out[0].astype(BF16), xa, mod_lat[1], mod_ctx[1],
                               norm_ffn[1:2], _router_split(router_w[1]), LAT_TILE,
                               T_LAT // LAT_TILE)
    spans, pair_tok, ys, _ = _expert_ffn(h2, logits, moe_w_gate, moe_w_up, moe_w_down, 1, False,
                                         COMB_TILE)
    return _res_final(xl, spans, pair_tok, ys, mod_lat[1], norm_final[None])
```
